```python
import math
import jax, jax.numpy as jnp
from jax import lax
import numpy as np

D_MODEL = 1024
BATCH = 2
SEQ = 8192
DEPTH = 1
DEC_BATCH = 128
DEC_SEQ = 4
PAST_LEN = 2048
PAGE_SIZE = 128

H_A = 4
DH_A = 64
DV_A = 2 * DH_A
W_A = H_A * DV_A
H_B = 8
DH_B = 64
W_B = H_B * DH_B
Q_BLOCK = 128
EPS = 1e-6
NEG_INF = -1e30
FORGET_BIAS = 3.0
_SIZES = (2 * H_A * DH_A, 2 * H_A * DH_A, W_A, W_A, W_B, W_B, W_B, H_B, W_B, D_MODEL, D_MODEL)
D_IN = sum(_SIZES)

kernel_name = 'diff_fox_gated_hybrid_step'


def _rms(x):
    xf = x.astype(jnp.float32)
    return (xf * lax.rsqrt(jnp.mean(xf * xf, axis=-1, keepdims=True) + EPS)).astype(x.dtype)


def _alibi_slopes():
    return jnp.asarray(2.0 ** (-8.0 * np.arange(1, H_A + 1) / H_A), jnp.float32)


def _lambda_init(layer):
    return 0.8 - 0.6 * math.exp(-0.3 * layer)


def _layer_inputs(x, c, w_ada, b_ada, w_in, b_f, g_q_a, g_k_a, g_q_b, g_k_b):
    B, T, _ = x.shape
    shift, scale, gate_res = jnp.split(jax.nn.silu(c) @ w_ada + b_ada, 3, axis=-1)
    h = _rms(x) * (1 + scale[:, None, :]) + shift[:, None, :]
    z = h @ w_in
    q_a, k_a, v_a, gate_a, q_b, k_b, v_b, f_b, gate_b, m_a, m_b = jnp.split(
        z, np.cumsum(_SIZES)[:-1].tolist(), axis=-1)
    q_a = _rms(q_a.reshape(B, T, 2, H_A, DH_A)) * g_q_a
    k_a = _rms(k_a.reshape(B, T, 2, H_A, DH_A)) * g_k_a
    v_a = v_a.reshape(B, T, H_A, DV_A)
    q_b = _rms(q_b.reshape(B, T, H_B, DH_B)) * g_q_b
    k_b = _rms(k_b.reshape(B, T, H_B, DH_B)) * g_k_b
    v_b = v_b.reshape(B, T, H_B, DH_B)
    logf = jax.nn.log_sigmoid((f_b + b_f).astype(jnp.float32))
    return q_a, k_a, v_a, gate_a, q_b, k_b, v_b, logf, gate_b, m_a, m_b, gate_res


def _diff_attn(q, k, v, q_pos, k_pos, slopes, lam):
    s = jnp.einsum('bqmhd,bkmhd->bmhqk', q, k, preferred_element_type=jnp.float32) * (DH_A ** -0.5)
    dist = (q_pos[:, None] - k_pos[None, :]).astype(jnp.float32)
    s = jnp.where(dist >= 0, s - slopes[:, None, None] * dist, NEG_INF)
    p = jax.nn.softmax(s, axis=-1)
    p = p[:, 0] - lam * p[:, 1]
    return jnp.einsum('bhqk,bkhd->bqhd', p.astype(v.dtype), v)


def _forgetting_attn(q, k, v, F_q, F_k, q_pos, k_pos):
    s = jnp.einsum('bqhd,bkhd->bhqk', q, k, preferred_element_type=jnp.float32) * (DH_B ** -0.5)
    s = s + jnp.swapaxes(F_q, 1, 2)[..., :, None] - jnp.swapaxes(F_k, 1, 2)[..., None, :]
    s = jnp.where(q_pos[:, None] >= k_pos[None, :], s, NEG_INF)
    p = jax.nn.softmax(s, axis=-1)
    return jnp.einsum('bhqk,bkhd->bqhd', p.astype(v.dtype), v)


def _prompt_mix(q_a, k_a, v_a, q_b, k_b, v_b, logf, slopes, lam):
    B, T = q_a.shape[:2]
    F = jnp.cumsum(logf, axis=1)
    k_pos = jnp.arange(T, dtype=jnp.int32)

    def block(i):
        start = i * Q_BLOCK
        q_pos = start + jnp.arange(Q_BLOCK, dtype=jnp.int32)
        qa = lax.dynamic_slice_in_dim(q_a, start, Q_BLOCK, axis=1)
        qb = lax.dynamic_slice_in_dim(q_b, start, Q_BLOCK, axis=1)
        Fq = lax.dynamic_slice_in_dim(F, start, Q_BLOCK, axis=1)
        return (_diff_attn(qa, k_a, v_a, q_pos, k_pos, slopes, lam),
                _forgetting_attn(qb, k_b, v_b, Fq, F, q_pos, k_pos))

    o_a, o_b = lax.map(block, jnp.arange(T // Q_BLOCK, dtype=jnp.int32))
    o_a = jnp.moveaxis(o_a, 0, 1).reshape(B, T, H_A, DV_A)
    o_b = jnp.moveaxis(o_b, 0, 1).reshape(B, T, H_B, DH_B)
    return o_a, o_b


def _gather_pages(pool, page_table):
    g = pool[page_table]
    return g.reshape(g.shape[0], g.shape[1] * g.shape[2], *g.shape[3:])


def _sample_mix(q_a, k_a, v_a, q_b, k_b, v_b, logf, pool_a_k, pool_a_v, pool_b_k, pool_b_v,
                pool_b_logf, page_table, slopes, lam):
    T = q_a.shape[1]
    ka = jnp.concatenate([_gather_pages(pool_a_k, page_table), k_a], axis=1)
    va = jnp.concatenate([_gather_pages(pool_a_v, page_table), v_a], axis=1)
    kb = jnp.concatenate([_gather_pages(pool_b_k, page_table), k_b], axis=1)
    vb = jnp.concatenate([_gather_pages(pool_b_v, page_table), v_b], axis=1)
    F_past = jnp.cumsum(_gather_pages(pool_b_logf, page_table).astype(jnp.float32), axis=1)
    F_new = F_past[:, -1:, :] + jnp.cumsum(logf, axis=1)
    F_k = jnp.concatenate([F_past, F_new], axis=1)
    past = ka.shape[1] - T
    q_pos = past + jnp.arange(T, dtype=jnp.int32)
    k_pos = jnp.arange(past + T, dtype=jnp.int32)
    o_a = _diff_attn(q_a, ka, va, q_pos, k_pos, slopes, lam)
    o_b = _forgetting_attn(q_b, kb, vb, F_new, F_k, q_pos, k_pos)
    return o_a, o_b


def _merge(x, o_a, o_b, gate_a, gate_b, m_a, m_b, gate_res, g_sub_a, w_out_a, w_out_b, w_o, lam_init):
    B, T = x.shape[:2]
    o_a = _rms(o_a) * g_sub_a * (1.0 - lam_init)
    y_a = (o_a.reshape(B, T, W_A) * jax.nn.silu(gate_a)) @ w_out_a
    y_b = (o_b.reshape(B, T, W_B) * jax.nn.silu(gate_b)) @ w_out_b
    y = (jax.nn.sigmoid(m_a) * y_a + jax.nn.sigmoid(m_b) * y_b) @ w_o
    return x + gate_res[:, None, :] * y


def setup_inputs(seed: int = 0) -> dict:
    key = jax.random.key(seed)
    ks = jax.random.split(key, 32)
    n_pages = PAST_LEN // PAGE_SIZE
    n_used = DEC_BATCH * n_pages
    n_pool = n_used + (n_used + 3) // 4

    def nrm(k, shape, s=1.0):
        return s * jax.random.normal(k, shape, jnp.float32)

    return {
        'x_prompt': nrm(ks[0], (BATCH, SEQ, D_MODEL)),
        'x_sample': nrm(ks[1], (DEC_BATCH, DEC_SEQ, D_MODEL)),
        'cache_a_k': nrm(ks[2], (DEPTH, n_pool, PAGE_SIZE, 2, H_A, DH_A)),
        'cache_a_v': nrm(ks[3], (DEPTH, n_pool, PAGE_SIZE, H_A, DV_A)),
        'cache_b_k': nrm(ks[4], (DEPTH, n_pool, PAGE_SIZE, H_B, DH_B)),
        'cache_b_v': nrm(ks[5], (DEPTH, n_pool, PAGE_SIZE, H_B, DH_B)),
        'cache_b_logf': jax.nn.log_sigmoid(FORGET_BIAS + nrm(ks[6], (DEPTH, n_pool, PAGE_SIZE, H_B))),
        'page_table': jax.random.permutation(ks[7], n_pool)[:n_used].reshape(DEC_BATCH, n_pages).astype(jnp.int32),
        'c_prompt': nrm(ks[8], (BATCH, D_MODEL)),
        'c_sample': nrm(ks[9], (DEC_BATCH, D_MODEL)),
        'w_ada': nrm(ks[10], (DEPTH, D_MODEL, 3 * D_MODEL), D_MODEL ** -0.5),
        'b_ada': nrm(ks[11], (DEPTH, 3 * D_MODEL), 0.1),
        'w_in': nrm(ks[12], (DEPTH, D_MODEL, D_IN), D_MODEL ** -0.5),
        'b_f': FORGET_BIAS + nrm(ks[13], (DEPTH, H_B), 0.5),
        'g_q_a': 1.0 + nrm(ks[14], (DEPTH, DH_A), 0.02),
        'g_k_a': 1.0 + nrm(ks[15], (DEPTH, DH_A), 0.02),
        'g_q_b': 1.0 + nrm(ks[16], (DEPTH, DH_B), 0.02),
        'g_k_b': 1.0 + nrm(ks[17], (DEPTH, DH_B), 0.02),
        'lambda_q1': nrm(ks[18], (DEPTH, DH_A), 0.1),
        'lambda_k1': nrm(ks[19], (DEPTH, DH_A), 0.1),
        'lambda_q2': nrm(ks[20], (DEPTH, DH_A), 0.1),
        'lambda_k2': nrm(ks[21], (DEPTH, DH_A), 0.1),
        'g_sub_a': 1.0 + nrm(ks[22], (DEPTH, DV_A), 0.02),
        'w_out_a': nrm(ks[23], (DEPTH, W_A, D_MODEL), W_A ** -0.5),
        'w_out_b': nrm(ks[24], (DEPTH, W_B, D_MODEL), W_B ** -0.5),
        'w_o': nrm(ks[25], (DEPTH, D_MODEL, D_MODEL), D_MODEL ** -0.5),
    }


def reference(x_prompt, x_sample, cache_a_k, cache_a_v, cache_b_k, cache_b_v, cache_b_logf, page_table,
              c_prompt, c_sample, w_ada, b_ada, w_in, b_f, g_q_a, g_k_a, g_q_b, g_k_b,
              lambda_q1, lambda_k1, lambda_q2, lambda_k2, g_sub_a, w_out_a, w_out_b, w_o):
    slopes = _alibi_slopes()
    xp, xs = x_prompt, x_sample
    st = {n: [] for n in ('kap', 'vap', 'kbp', 'vbp', 'lfp', 'kas', 'vas', 'kbs', 'vbs', 'lfs')}
    for l in range(DEPTH):
        lam_init = _lambda_init(l)
        lam = (jnp.exp(jnp.sum(lambda_q1[l].astype(jnp.float32) * lambda_k1[l].astype(jnp.float32)))
               - jnp.exp(jnp.sum(lambda_q2[l].astype(jnp.float32) * lambda_k2[l].astype(jnp.float32)))
               + lam_init)
        proj = (w_ada[l], b_ada[l], w_in[l], b_f[l], g_q_a[l], g_k_a[l], g_q_b[l], g_k_b[l])
        out = (g_sub_a[l], w_out_a[l], w_out_b[l], w_o[l], lam_init)

        q_a, k_a, v_a, ga, q_b, k_b, v_b, lf, gb, ma, mb, gr = _layer_inputs(xp, c_prompt, *proj)
        o_a, o_b = _prompt_mix(q_a, k_a, v_a, q_b, k_b, v_b, lf, slopes, lam)
        xp = _merge(xp, o_a, o_b, ga, gb, ma, mb, gr, *out)
        st['kap'].append(k_a); st['vap'].append(v_a); st['kbp'].append(k_b)
        st['vbp'].append(v_b); st['lfp'].append(lf)

        q_a, k_a, v_a, ga, q_b, k_b, v_b, lf, gb, ma, mb, gr = _layer_inputs(xs, c_sample, *proj)
        o_a, o_b = _sample_mix(q_a, k_a, v_a, q_b, k_b, v_b, lf, cache_a_k[l], cache_a_v[l], cache_b_k[l],
                               cache_b_v[l], cache_b_logf[l], page_table, slopes, lam)
        xs = _merge(xs, o_a, o_b, ga, gb, ma, mb, gr, *out)
        st['kas'].append(k_a); st['vas'].append(v_a); st['kbs'].append(k_b)
        st['vbs'].append(v_b); st['lfs'].append(lf)
    return (xp, xs,
            jnp.stack(st['kap']), jnp.stack(st['vap']), jnp.stack(st['kbp']), jnp.stack(st['vbp']), jnp.stack(st['lfp']),
            jnp.stack(st['kas']), jnp.stack(st['vas']), jnp.stack(st['kbs']), jnp.stack(st['vbs']), jnp.stack(st['lfs']))
```

```python
import functools
import math

import numpy as np
import jax
import jax.numpy as jnp
from jax import lax
from jax.experimental import pallas as pl
from jax.experimental.pallas import tpu as pltpu

F32 = jnp.float32
BF16 = jnp.bfloat16

D_MODEL = 1024
H_A, DH_A, DV_A = 4, 64, 128
H_B, DH_B = 8, 64
W_A = H_A * DV_A
W_B = H_B * DH_B
N_GROUPS = 8
EPS = 1e-6
NEG = -1e30
SLOPES = tuple(2.0 ** (-8.0 * (h + 1) / H_A) for h in range(H_A))
LANES = 128
AUG_COL = 64

TM = 256
TQ = 256
TK = 256
VMEM_LIMIT = 56 * 1024 * 1024


def _dot(a, b):
    return jnp.dot(a, b, preferred_element_type=F32)


def _dot_nt(a, b):
    return lax.dot_general(a, b, (((1,), (1,)), ((), ())), preferred_element_type=F32)


def _split3(x):
    hi = x.astype(BF16)
    r1 = x - hi.astype(F32)
    mid = r1.astype(BF16)
    lo = (r1 - mid.astype(F32)).astype(BF16)
    return hi, mid, lo


def _silu(x):
    return x * jax.nn.sigmoid(x)


def _const_spec(shape):
    nd = len(shape)
    return pl.BlockSpec(shape, lambda *_: (0,) * nd, pipeline_mode=pl.Buffered(1))


def _ada_kernel(c_ref, w_ref, b_ref, o_ref):
    c = c_ref[...]
    o_ref[...] = _dot(_silu(c).astype(BF16), w_ref[...].astype(BF16)) + b_ref[...]


def _ada(c_all, w_ada, b_ada):
    rows = c_all.shape[0]
    n = w_ada.shape[1]
    bn = 1024
    return pl.pallas_call(
        _ada_kernel,
        grid=(n // bn,),
        in_specs=[pl.BlockSpec((rows, D_MODEL), lambda j: (0, 0)),
                  pl.BlockSpec((D_MODEL, bn), lambda j: (0, j)),
                  pl.BlockSpec((1, bn), lambda j: (0, j))],
        out_specs=pl.BlockSpec((rows, bn), lambda j: (0, j)),
        out_shape=jax.ShapeDtypeStruct((rows, n), F32),
        name="ada",
    )(c_all, w_ada, b_ada.reshape(1, n))


def _hidden(x_ref, scale_ref, shift_ref):
    x = x_ref[...]
    ms = jnp.mean(x * x, axis=-1, keepdims=True)
    h = x * lax.rsqrt(ms + EPS) * (1.0 + scale_ref[...]) + shift_ref[...]
    return h.astype(BF16)


def _group_rms(z, g_ref):
    zz = z * z
    hi = zz.astype(BF16)
    lo = (zz - hi.astype(F32)).astype(BF16)
    ss = _dot(hi, g_ref[...]) + _dot(lo, g_ref[...])
    return z * lax.rsqrt(ss * (1.0 / DH_A) + EPS)


def _log_sigmoid(z):
    return jnp.minimum(z, 0.0) - jnp.log1p(jnp.exp(-jnp.abs(z)))


def _logf(hb, wf_ref, bf_ref):
    z = _dot(hb, wf_ref[...]) + bf_ref[...]
    lane = lax.broadcasted_iota(jnp.int32, z.shape, 1)
    return jnp.where(lane < H_B, _log_sigmoid(z), 0.0)


def _proj_prompt_kernel(nb, mshift_ref, x_ref, scale_ref, shift_ref, wqt_ref, wn_ref, wf_ref, bf_ref,
                        gq_ref, gk_ref, g_ref, place_ref, self_ref, constb_ref, tri_ref,
                        ka_ref, va_ref, kb_ref, vb_ref, lf_ref,
                        qta_ref, kaa_ref, vta_ref, qtb_ref, kab_ref, vtb_ref,
                        sga_ref, sgb_ref, sma_ref, smb_ref, carry_ref):
    tm = x_ref.shape[0]
    ib = pl.program_id(0) % nb
    hb = _hidden(x_ref, scale_ref, shift_ref)

    lf = _logf(hb, wf_ref, bf_ref)
    lf_ref[...] = lf[:, :H_B]
    hi, mid, lo = _split3(lf)
    tri = tri_ref[...]
    f_local = _dot(tri, hi) + _dot(tri, mid) + _dot(tri, lo)

    @pl.when(ib == 0)
    def _():
        carry_ref[...] = jnp.zeros_like(carry_ref)

    f_nat = f_local + carry_ref[...]
    carry_ref[...] = f_nat[tm - 1:tm, :]
    f_t = f_nat.T[0:H_B, :]

    qt = _dot_nt(wqt_ref[...], hb)
    q3 = qt.reshape(2 * N_GROUPS, DH_A, tm)
    ssq = jnp.sum(q3 * q3, axis=1, keepdims=True)
    q3 = (qt * gq_ref[...]).reshape(2 * N_GROUPS, DH_A, tm) * lax.rsqrt(ssq * (1.0 / DH_A) + EPS)

    r8 = lax.broadcasted_iota(jnp.int32, (8, tm), 0)
    pos_q = ib * tm + lax.broadcasted_iota(jnp.int32, (8, tm), 1)
    a_q = (pos_q >> 7).astype(F32)
    b_q = (pos_q & 127).astype(F32)
    zeros_tail = jnp.zeros((LANES - DH_A - 8, tm), F32)
    m_a = mshift_ref[0]
    m_b = mshift_ref[1]
    for g in range(N_GROUPS):
        slope = SLOPES[g % H_A]
        aug = jnp.where(r8 == 0, -slope * 128.0 * a_q,
              jnp.where(r8 == 1, -slope * b_q,
              jnp.where(r8 == 2, slope * 128.0,
              jnp.where(r8 == 3, slope,
              jnp.where(r8 == 4, -m_a, 0.0)))))
        blk = jnp.concatenate([q3[g], aug, zeros_tail], axis=0)
        qta_ref[0, g * LANES:(g + 1) * LANES, :] = blk.astype(BF16)
    ft_hi, ft_mid, ft_lo = _split3(f_t)
    for g in range(N_GROUPS):
        bc = lambda v: jnp.broadcast_to(v[g:g + 1, :].astype(F32), (8, tm))
        aug = jnp.where(r8 == 0, bc(ft_hi),
              jnp.where(r8 == 1, bc(ft_mid),
              jnp.where(r8 == 2, bc(ft_lo),
              jnp.where(r8 <= 5, -1.0,
              jnp.where(r8 == 6, -m_b, 0.0)))))
        blk = jnp.concatenate([q3[N_GROUPS + g], aug, zeros_tail], axis=0)
        qtb_ref[0, g * LANES:(g + 1) * LANES, :] = blk.astype(BF16)

    ka = _group_rms(_dot(hb, wn_ref[:, 0:512]), g_ref) * gk_ref[:, 0:512]
    ka_ref[...] = ka
    lane = lax.broadcasted_iota(jnp.int32, (tm, LANES), 1)
    pos_k = ib * tm + lax.broadcasted_iota(jnp.int32, (tm, LANES), 0)
    aug_a = jnp.where(lane == AUG_COL + 2, (pos_k >> 7).astype(F32),
            jnp.where(lane == AUG_COL + 3, (pos_k & 127).astype(F32),
            jnp.where((lane == AUG_COL) | (lane == AUG_COL + 1) | (lane == AUG_COL + 4), 1.0, 0.0)))
    kaa = _dot(ka.astype(BF16), place_ref[...]) + jnp.concatenate([aug_a] * N_GROUPS, axis=1)
    kaa_ref[0] = kaa.astype(BF16)

    kb = _group_rms(_dot(hb, wn_ref[:, 512:1024]), g_ref) * gk_ref[:, 512:1024]
    kb_ref[...] = kb
    f_hi, f_mid, f_lo = _split3(f_nat)
    f3 = jnp.concatenate([f_hi, f_mid, f_lo], axis=1)
    kab = _dot(kb.astype(BF16), place_ref[...]) + _dot(f3, self_ref[...]) + constb_ref[...]
    kab_ref[0] = kab.astype(BF16)

    va = _dot(hb, wn_ref[:, 1024:1536])
    va_ref[...] = va
    vta_ref[0, 0] = va.T.astype(BF16)
    vb = _dot(hb, wn_ref[:, 1536:2048])
    vb_ref[...] = vb
    vtb_ref[0, 0] = vb.T.astype(BF16)

    sga_ref[...] = _silu(_dot(hb, wn_ref[:, 2048:2560])).astype(BF16)
    sgb_ref[...] = _silu(_dot(hb, wn_ref[:, 2560:3072])).astype(BF16)
    sma_ref[...] = jax.nn.sigmoid(_dot(hb, wn_ref[:, 3072:4096])).astype(BF16)
    smb_ref[...] = jax.nn.sigmoid(_dot(hb, wn_ref[:, 4096:5120])).astype(BF16)


def _proj_prompt(x2, scale, shift, mshift, wts, batch, seq):
    rows = batch * seq
    nb = seq // TM
    nk = seq // TK
    row = lambda w: pl.BlockSpec((TM, w), lambda i: (i, 0))
    mod = pl.BlockSpec((None, 1, D_MODEL), lambda i: (i // nb, 0, 0))
    consts = [wts["wqt"], wts["wn"], wts["wf"], wts["bf"], wts["gq_col"], wts["gk_row"], wts["gsum"],
              wts["place"], wts["self"], wts["constb"], wts["tri"]]
    out_shape = [jax.ShapeDtypeStruct((rows, 512), F32)] * 4 + [jax.ShapeDtypeStruct((rows, H_B), F32)] + [
        jax.ShapeDtypeStruct((batch, 1024, seq), BF16), jax.ShapeDtypeStruct((batch, seq, 1024), BF16),
        jax.ShapeDtypeStruct((batch, nk, W_A, TK), BF16),
        jax.ShapeDtypeStruct((batch, 1024, seq), BF16), jax.ShapeDtypeStruct((batch, seq, 1024), BF16),
        jax.ShapeDtypeStruct((batch, nk, W_B, TK), BF16),
        jax.ShapeDtypeStruct((rows, 512), BF16), jax.ShapeDtypeStruct((rows, 512), BF16),
        jax.ShapeDtypeStruct((rows, 1024), BF16), jax.ShapeDtypeStruct((rows, 1024), BF16)]
    qt_spec = pl.BlockSpec((1, 1024, TM), lambda i: (i // nb, 0, i % nb))
    k_spec = pl.BlockSpec((1, TM, 1024), lambda i: (i // nb, i % nb, 0))
    vt_spec = pl.BlockSpec((1, TM // TK, 512, TK), lambda i: (i // nb, i % nb, 0, 0))
    out_specs = [row(512)] * 4 + [row(H_B)] + [qt_spec, k_spec, vt_spec, qt_spec, k_spec, vt_spec,
                                               row(512), row(512), row(1024), row(1024)]
    return pl.pallas_call(
        functools.partial(_proj_prompt_kernel, nb),
        grid=(rows // TM,),
        in_specs=[pl.BlockSpec(memory_space=pltpu.SMEM), row(D_MODEL), mod, mod]
                 + [_const_spec(c.shape) for c in consts],
        out_specs=out_specs,
        out_shape=out_shape,
        scratch_shapes=[pltpu.VMEM((1, LANES), F32)],
        compiler_params=pltpu.CompilerParams(dimension_semantics=("arbitrary",),
                                             vmem_limit_bytes=VMEM_LIMIT),
        name="proj_prompt",
    )(mshift, x2, scale, shift, *consts)


def _proj_sample_kernel(x_ref, scale_ref, shift_ref, wq_ref, wn_ref, wf_ref, bf_ref, gq_ref, gk_ref, g_ref,
                        qa_ref, qb_ref, ka_ref, va_ref, kb_ref, vb_ref, lf_ref,
                        sga_ref, sgb_ref, sma_ref, smb_ref):
    hb = _hidden(x_ref, scale_ref, shift_ref)
    lf_ref[...] = _logf(hb, wf_ref, bf_ref)[:, :H_B]
    qa_ref[...] = _group_rms(_dot(hb, wq_ref[:, 0:512]), g_ref) * gq_ref[:, 0:512]
    qb_ref[...] = _group_rms(_dot(hb, wq_ref[:, 512:1024]), g_ref) * gq_ref[:, 512:1024]
    ka_ref[...] = _group_rms(_dot(hb, wn_ref[:, 0:512]), g_ref) * gk_ref[:, 0:512]
    kb_ref[...] = _group_rms(_dot(hb, wn_ref[:, 512:1024]), g_ref) * gk_ref[:, 512:1024]
    va_ref[...] = _dot(hb, wn_ref[:, 1024:1536])
    vb_ref[...] = _dot(hb, wn_ref[:, 1536:2048])
    sga_ref[...] = _silu(_dot(hb, wn_ref[:, 2048:2560])).astype(BF16)
    sgb_ref[...] = _silu(_dot(hb, wn_ref[:, 2560:3072])).astype(BF16)
    sma_ref[...] = jax.nn.sigmoid(_dot(hb, wn_ref[:, 3072:4096])).astype(BF16)
    smb_ref[...] = jax.nn.sigmoid(_dot(hb, wn_ref[:, 4096:5120])).astype(BF16)


def _proj_sample(x2, scale_rows, shift_rows, wts):
    rows = x2.shape[0]
    row = lambda w: pl.BlockSpec((TM, w), lambda i: (i, 0))
    consts = [wts["wq"], wts["wn"], wts["wf"], wts["bf"], wts["gq_row"], wts["gk_row"], wts["gsum"]]
    out_shape = [jax.ShapeDtypeStruct((rows, 512), F32)] * 6 + [jax.ShapeDtypeStruct((rows, H_B), F32)] + [
        jax.ShapeDtypeStruct((rows, 512), BF16), jax.ShapeDtypeStruct((rows, 512), BF16),
        jax.ShapeDtypeStruct((rows, 1024), BF16), jax.ShapeDtypeStruct((rows, 1024), BF16)]
    out_specs = [row(512)] * 6 + [row(H_B), row(512), row(512), row(1024), row(1024)]
    return pl.pallas_call(
        _proj_sample_kernel,
        grid=(rows // TM,),
        in_specs=[row(D_MODEL), row(D_MODEL), row(D_MODEL)] + [_const_spec(c.shape) for c in consts],
        out_specs=out_specs,
        out_shape=out_shape,
        compiler_params=pltpu.CompilerParams(dimension_semantics=("arbitrary",),
                                             vmem_limit_bytes=VMEM_LIMIT),
        name="proj_sample",
    )(x2, scale_rows, shift_rows, *consts)


def _attn_kernel(shared_v, lam_ref, qt0_ref, qt1_ref, k0_ref, k1_ref, vt_ref, o_ref, acc_ref):
    i = pl.program_id(2)
    qts = (qt0_ref[0], qt1_ref[0])
    k_refs = (k0_ref, k1_ref)
    dv = acc_ref.shape[1]
    acc_ref[...] = jnp.zeros_like(acc_ref)

    def values(u, j):
        blk = vt_ref[0, j]
        return blk if shared_v else blk[u * dv:(u + 1) * dv, :]

    def step(j, carry, masked):
        out = []
        for u in range(2):
            m, l = carry[u]
            kblk = k_refs[u][0, pl.ds(pl.multiple_of(j * TK, TK), TK), :]
            s = _dot(kblk, qts[u])
            if masked:
                kpos = lax.broadcasted_iota(jnp.int32, s.shape, 0)
                qpos = lax.broadcasted_iota(jnp.int32, s.shape, 1)
                s = jnp.where(kpos <= qpos, s, NEG)
            m_new = jnp.maximum(m, jnp.max(s, axis=0, keepdims=True))
            alpha = jnp.exp(m - m_new)
            p = jnp.exp(s - m_new)
            l = alpha * l + jnp.sum(p, axis=0, keepdims=True)
            acc_ref[u] = alpha * acc_ref[u] + _dot(values(u, j), p.astype(BF16))
            out.append((m_new, l))
        return tuple(out)

    init = tuple((jnp.full((1, TQ), NEG, F32), jnp.zeros((1, TQ), F32)) for _ in range(2))
    carry = lax.fori_loop(0, i, lambda j, c: step(j, c, False), init)
    (_, l0), (_, l1) = step(i, carry, True)

    o0 = acc_ref[0] * (1.0 / l0)
    o1 = acc_ref[1] * (1.0 / l1)
    o = o0 - lam_ref[0] * o1 if shared_v else jnp.concatenate([o0, o1], axis=0)
    o_ref[0] = o.T


def _attention(qt, kaug, vt, lam, shared_v):
    batch, _, seq = qt.shape
    nk = seq // TK
    n_steps = 4
    if shared_v:
        g0 = lambda h: h
        g1 = lambda h: H_A + h
    else:
        g0 = lambda h: 2 * h
        g1 = lambda h: 2 * h + 1
    dv = DV_A if shared_v else DH_B
    qspec = lambda g: pl.BlockSpec((1, LANES, TQ), lambda b, h, i: (b, g(h), i))
    kspec = lambda g: pl.BlockSpec((1, seq, LANES), lambda b, h, i: (b, 0, g(h)))
    return pl.pallas_call(
        functools.partial(_attn_kernel, shared_v),
        grid=(batch, n_steps, seq // TQ),
        in_specs=[pl.BlockSpec(memory_space=pltpu.SMEM), qspec(g0), qspec(g1), kspec(g0), kspec(g1),
                  pl.BlockSpec((1, nk, LANES, TK), lambda b, h, i: (b, 0, h, 0))],
        out_specs=pl.BlockSpec((1, TQ, LANES), lambda b, h, i: (b, i, h)),
        out_shape=jax.ShapeDtypeStruct((batch, seq, 512), F32),
        scratch_shapes=[pltpu.VMEM((2, dv, TQ), F32)],
        compiler_params=pltpu.CompilerParams(
            dimension_semantics=("arbitrary", "arbitrary", "arbitrary"), vmem_limit_bytes=VMEM_LIMIT),
        name="attn_a" if shared_v else "attn_b",
    )(lam, qt, qt, kaug, kaug, vt)


def _decode_kernel(past, pt_ref, lam_ref, qa_ref, qb_ref, ka_ref, va_ref, kb_ref, vb_ref, lft_ref,
                   kan_ref, van_ref, kbn_ref, vbn_ref, lfn_ref, upper_ref,
                   oa_ref, ob_ref, qbd_ref, m_ref, l_ref, acc_ref, fc_ref):
    del pt_ref
    p = pl.program_id(1)
    n_pages = pl.num_programs(1)
    rows = 32
    n_tok = 4
    new_pad = kan_ref.shape[1]

    @pl.when(p == 0)
    def _():
        g_row = lax.broadcasted_iota(jnp.int32, (8, 512), 0)
        g_col = lax.broadcasted_iota(jnp.int32, (8, 512), 1) >> 6
        for u, q_ref in enumerate((qa_ref, qb_ref)):
            pieces = [jnp.where(g_row == g_col, jnp.broadcast_to(q_ref[0, t:t + 1, :], (8, 512)), 0.0)
                      for t in range(n_tok)]
            qbd_ref[u] = jnp.concatenate(pieces, axis=0).astype(BF16)
        m_ref[...] = jnp.full(m_ref.shape, NEG, F32)
        l_ref[...] = jnp.zeros_like(l_ref)
        acc_ref[...] = jnp.zeros_like(acc_ref)
        fc_ref[...] = jnp.zeros_like(fc_ref)

    def update(u, s, v_bf):
        m_prev = m_ref[u][:, 0:1]
        l_prev = l_ref[u][:, 0:1]
        m_new = jnp.maximum(m_prev, jnp.max(s, axis=1, keepdims=True))
        alpha = jnp.exp(m_prev - m_new)
        pr = jnp.exp(s - m_new)
        l_new = alpha * l_prev + jnp.sum(pr, axis=1, keepdims=True)
        acc_ref[u] = alpha * acc_ref[u] + _dot(pr.astype(BF16), v_bf)
        m_ref[u] = jnp.broadcast_to(m_new, (rows, LANES))
        l_ref[u] = jnp.broadcast_to(l_new, (rows, LANES))

    def row_slope(shape):
        h = lax.broadcasted_iota(jnp.int32, shape, 0) & (H_A - 1)
        return jnp.where(h == 0, SLOPES[0], jnp.where(h == 1, SLOPES[1], jnp.where(h == 2, SLOPES[2], SLOPES[3])))

    r = lax.broadcasted_iota(jnp.int32, (rows, LANES), 0)
    lane = lax.broadcasted_iota(jnp.int32, (rows, LANES), 1)
    dist = (past + (r >> 3) - (p * LANES + lane)).astype(F32)
    s_a = _dot_nt(qbd_ref[0], ka_ref[0].astype(BF16)) - row_slope((rows, LANES)) * dist
    update(0, s_a, va_ref[0].astype(BF16))

    hi, mid, lo = (v.astype(F32) for v in _split3(lft_ref[0]))
    lf3 = jnp.concatenate([hi, mid, lo, jnp.zeros_like(hi)], axis=0).astype(BF16)
    cs = _dot(lf3, upper_ref[...])
    f_page = cs[0:8] + cs[8:16] + cs[16:24] + fc_ref[...][:, 0:1]
    f_tot = f_page[:, LANES - 1:LANES]
    fc_ref[...] = jnp.broadcast_to(f_tot, (8, LANES))
    s_b = _dot_nt(qbd_ref[1], kb_ref[0].astype(BF16)) - jnp.concatenate([f_page] * n_tok, axis=0)
    update(1, s_b, vb_ref[0].astype(BF16))

    @pl.when(p == n_pages - 1)
    def _():
        rn = lax.broadcasted_iota(jnp.int32, (rows, new_pad), 0)
        tn = lax.broadcasted_iota(jnp.int32, (rows, new_pad), 1)
        valid = (tn <= (rn >> 3)) & (tn < n_tok)
        dist_n = ((rn >> 3) - tn).astype(F32)
        s_an = _dot_nt(qbd_ref[0], kan_ref[0].astype(BF16)) - row_slope((rows, new_pad)) * dist_n
        update(0, jnp.where(valid, s_an, NEG), van_ref[0].astype(BF16))

        lfn = lfn_ref[0]
        t8 = lax.broadcasted_iota(jnp.int32, (8, new_pad), 1)
        f_new = jnp.broadcast_to(f_tot, (8, new_pad))
        for t in range(n_tok):
            f_new = f_new + jnp.where(t8 >= t, jnp.broadcast_to(lfn[:, t:t + 1], (8, new_pad)), 0.0)
        s_bn = _dot_nt(qbd_ref[1], kbn_ref[0].astype(BF16)) - jnp.concatenate([f_new] * n_tok, axis=0)
        update(1, jnp.where(valid, s_bn, NEG), vbn_ref[0].astype(BF16))

        g = lax.broadcasted_iota(jnp.int32, (rows, 512), 0) & 7
        col = lax.broadcasted_iota(jnp.int32, (rows, 512), 1)
        w_a = jnp.where((g & (H_A - 1)) == (col >> 7), jnp.where(g < H_A, 1.0, -lam_ref[0]), 0.0)
        w_b = jnp.where(g == (col >> 6), 1.0, 0.0)
        for u, (w, o_ref) in enumerate(((w_a, oa_ref), (w_b, ob_ref))):
            on = acc_ref[u] * (1.0 / l_ref[u][:, 0:1]) * w
            for t in range(n_tok):
                o_ref[0, t:t + 1, :] = jnp.sum(on[8 * t:8 * t + 8], axis=0, keepdims=True)


def _decode(page_table, lam, qa, qb, pools, lft_pool, news, lfn_t, upper):
    n_seq, n_pages = page_table.shape
    page = pools[0].shape[1]
    new_pad = news[0].shape[1]
    past = n_pages * page
    pool_spec = pl.BlockSpec((1, page, 512), lambda n, p, pt: (pt[n * n_pages + p], 0, 0))
    seq_spec = lambda r: pl.BlockSpec((1, r, 512), lambda n, p, pt: (n, 0, 0))
    grid_spec = pltpu.PrefetchScalarGridSpec(
        num_scalar_prefetch=1,
        grid=(n_seq, n_pages),
        in_specs=[pl.BlockSpec(memory_space=pltpu.SMEM), seq_spec(4), seq_spec(4),
                  pool_spec, pool_spec, pool_spec, pool_spec,
                  pl.BlockSpec((1, H_B, page), lambda n, p, pt: (pt[n * n_pages + p], 0, 0)),
                  seq_spec(new_pad), seq_spec(new_pad), seq_spec(new_pad), seq_spec(new_pad),
                  pl.BlockSpec((1, H_B, new_pad), lambda n, p, pt: (n, 0, 0)),
                  pl.BlockSpec((page, page), lambda n, p, pt: (0, 0))],
        out_specs=[seq_spec(4), seq_spec(4)],
        scratch_shapes=[pltpu.VMEM((2, 32, 512), BF16), pltpu.VMEM((2, 32, LANES), F32),
                        pltpu.VMEM((2, 32, LANES), F32), pltpu.VMEM((2, 32, 512), F32),
                        pltpu.VMEM((H_B, LANES), F32)],
    )
    return pl.pallas_call(
        functools.partial(_decode_kernel, past),
        grid_spec=grid_spec,
        out_shape=[jax.ShapeDtypeStruct((n_seq, 4, 512), F32)] * 2,
        compiler_params=pltpu.CompilerParams(dimension_semantics=("arbitrary", "arbitrary"),
                                             vmem_limit_bytes=VMEM_LIMIT),
        name="decode",
    )(page_table.reshape(-1), lam, qa, qb, *pools, lft_pool, *news, lfn_t, upper)


def _merge_kernel(x_ref, oa_ref, ob_ref, sga_ref, sgb_ref, sma_ref, smb_ref, gres_ref, gsub_ref,
                  woa_ref, wob_ref, wo_ref, y_ref):
    oa = oa_ref[...]
    heads = []
    for h in range(H_A):
        oh = oa[:, h * DV_A:(h + 1) * DV_A]
        heads.append(oh * lax.rsqrt(jnp.mean(oh * oh, axis=-1, keepdims=True) + EPS))
    oa = jnp.concatenate(heads, axis=1) * gsub_ref[...]
    ya = _dot((oa * sga_ref[...].astype(F32)).astype(BF16), woa_ref[...])
    yb = _dot((ob_ref[...] * sgb_ref[...].astype(F32)).astype(BF16), wob_ref[...])
    mix = sma_ref[...].astype(F32) * ya + smb_ref[...].astype(F32) * yb
    y = _dot(mix.astype(BF16), wo_ref[...])
    y_ref[...] = x_ref[...] + gres_ref[...] * y


def _merge(x2, oa, ob, sga, sgb, sma, smb, gres, gres_spec, wts):
    rows = x2.shape[0]
    row = lambda w: pl.BlockSpec((TM, w), lambda i: (i, 0))
    consts = [wts["gsub"], wts["woa"], wts["wob"], wts["wo"]]
    return pl.pallas_call(
        _merge_kernel,
        grid=(rows // TM,),
        in_specs=[row(D_MODEL), row(512), row(512), row(512), row(512), row(1024), row(1024), gres_spec]
                 + [_const_spec(c.shape) for c in consts],
        out_specs=row(D_MODEL),
        out_shape=jax.ShapeDtypeStruct((rows, D_MODEL), F32),
        compiler_params=pltpu.CompilerParams(dimension_semantics=("arbitrary",),
                                             vmem_limit_bytes=VMEM_LIMIT),
        name="merge",
    )(x2, oa, ob, sga, sgb, sma, smb, gres, *consts)


def _placement_constants():
    gsum = np.kron(np.eye(N_GROUPS), np.ones((DH_A, DH_A)))
    place = np.zeros((512, 1024))
    for g in range(N_GROUPS):
        place[g * DH_A + np.arange(DH_A), g * LANES + np.arange(DH_A)] = 1.0
    sel = np.zeros((3 * LANES, 1024))
    constb = np.zeros((1, 1024))
    for g in range(N_GROUPS):
        for part in range(3):
            sel[part * LANES + g, g * LANES + AUG_COL + 3 + part] = 1.0
        constb[0, g * LANES + AUG_COL + np.array([0, 1, 2, 6])] = 1.0
    tri = np.tril(np.ones((TM, TM)))
    upper = np.triu(np.ones((LANES, LANES)))
    bf = lambda a: jnp.asarray(a, BF16)
    return dict(gsum=bf(gsum), place=bf(place), self=bf(sel), constb=jnp.asarray(constb, F32),
                tri=bf(tri), upper=bf(upper))


def _bf16_ceil(x):
    y = x.astype(BF16).astype(F32)
    return jnp.where(y < x, y * (1.0 + 2.0 ** -7), y)


def kernel(x_prompt, x_sample, cache_a_k, cache_a_v, cache_b_k, cache_b_v, cache_b_logf, page_table,
           c_prompt, c_sample, w_ada, b_ada, w_in, b_f, g_q_a, g_k_a, g_q_b, g_k_b,
           lambda_q1, lambda_k1, lambda_q2, lambda_k2, g_sub_a, w_out_a, w_out_b, w_o):
    assert w_ada.shape[0] == 1, "single-layer step"
    batch, seq, _ = x_prompt.shape
    n_seq, n_tok, _ = x_sample.shape
    n_pool, page = cache_a_k.shape[1], cache_a_k.shape[2]
    layer = 0
    lam_init = 0.8 - 0.6 * math.exp(-0.3 * layer)
    lam = (jnp.exp(jnp.sum(lambda_q1[layer] * lambda_k1[layer]))
           - jnp.exp(jnp.sum(lambda_q2[layer] * lambda_k2[layer])) + lam_init).reshape(1).astype(F32)

    w = w_in[layer]
    sec = np.cumsum((0, 512, 512, 512, 512, 512, 512, 512, H_B, 512, 1024, 1024))
    cols = lambda k: w[:, sec[k]:sec[k + 1]]
    wq = jnp.concatenate([cols(0), cols(4)], axis=1).astype(BF16)
    wn = jnp.concatenate([cols(1), cols(5), cols(2), cols(6), cols(3), cols(8), cols(9), cols(10)],
                         axis=1).astype(BF16)
    wf = jnp.pad(cols(7), ((0, 0), (0, LANES - H_B))).astype(BF16)
    gq = jnp.concatenate([jnp.tile(g_q_a[layer], N_GROUPS) * DH_A ** -0.5,
                          jnp.tile(g_q_b[layer], N_GROUPS) * DH_B ** -0.5])
    gk = jnp.concatenate([jnp.tile(g_k_a[layer], N_GROUPS), jnp.tile(g_k_b[layer], N_GROUPS)])
    wts = dict(_placement_constants(),
               wq=wq, wqt=wq.T, wn=wn, wf=wf, bf=jnp.pad(b_f[layer], (0, LANES - H_B)).reshape(1, LANES),
               gq_col=gq.reshape(-1, 1), gq_row=gq.reshape(1, -1), gk_row=gk.reshape(1, -1),
               gsub=(jnp.tile(g_sub_a[layer], H_A) * (1.0 - lam_init)).reshape(1, W_A),
               woa=w_out_a[layer].astype(BF16), wob=w_out_b[layer].astype(BF16), wo=w_o[layer].astype(BF16))
    bound = lambda gq_, gk_: _bf16_ceil(8.1 * jnp.max(jnp.abs(gq_ * gk_)))
    mshift = jnp.stack([bound(g_q_a[layer], g_k_a[layer]), bound(g_q_b[layer], g_k_b[layer])]).astype(F32)

    n_c = batch + n_seq
    c_all = jnp.pad(jnp.concatenate([c_prompt, c_sample], axis=0), ((0, -n_c % 8), (0, 0)))
    mod = _ada(c_all, w_ada[layer], b_ada[layer])
    shift, scale, gres = (mod[:, k * D_MODEL:(k + 1) * D_MODEL] for k in range(3))

    xp2 = x_prompt.reshape(batch * seq, D_MODEL)
    p3 = lambda a: a[:batch].reshape(batch, 1, D_MODEL)
    (ka, va, kb, vb, lf, qta, kaa, vta, qtb, kab, vtb, sga, sgb, sma, smb) = _proj_prompt(
        xp2, p3(scale), p3(shift), mshift, wts, batch, seq)
    oa = _attention(qta, kaa, vta, lam, True).reshape(batch * seq, W_A)
    ob = _attention(qtb, kab, vtb, lam, False).reshape(batch * seq, W_B)
    nb = seq // TM
    gres_p_spec = pl.BlockSpec((None, 1, D_MODEL), lambda i: (i // nb, 0, 0))
    yp = _merge(xp2, oa, ob, sga, sgb, sma, smb, p3(gres), gres_p_spec, wts).reshape(batch, seq, D_MODEL)

    xs2 = x_sample.reshape(n_seq * n_tok, D_MODEL)
    rep = lambda a: jnp.repeat(a[batch:n_c], n_tok, axis=0)
    (qa_s, qb_s, ka_s, va_s, kb_s, vb_s, lf_s, sga_s, sgb_s, sma_s, smb_s) = _proj_sample(
        xs2, rep(scale), rep(shift), wts)
    new_pad = 16
    seq3 = lambda a: a.reshape(n_seq, n_tok, 512)
    padn = lambda a: jnp.pad(seq3(a), ((0, 0), (0, new_pad - n_tok), (0, 0)))
    pools = [c[layer].reshape(n_pool, page, 512) for c in (cache_a_k, cache_a_v, cache_b_k, cache_b_v)]
    lft_pool = jnp.swapaxes(cache_b_logf[layer], 1, 2)
    lfn_t = jnp.pad(jnp.swapaxes(lf_s.reshape(n_seq, n_tok, H_B), 1, 2), ((0, 0), (0, 0), (0, new_pad - n_tok)))
    oa_s, ob_s = _decode(page_table, lam, seq3(qa_s), seq3(qb_s), pools, lft_pool,
                         [padn(ka_s), padn(va_s), padn(kb_s), padn(vb_s)], lfn_t, wts["upper"])
    gres_s_spec = pl.BlockSpec((TM, D_MODEL), lambda i: (i, 0))
    ys = _merge(xs2, oa_s.reshape(-1, W_A), ob_s.reshape(-1, W_B), sga_s, sgb_s, sma_s, smb_s,
                rep(gres), gres_s_spec, wts).reshape(n_seq, n_tok, D_MODEL)

    return (yp, ys,
            ka.reshape(1, batch, seq, 2, H_A, DH_A), va.reshape(1, batch, seq, H_A, DV_A),
            kb.reshape(1, batch, seq, H_B, DH_B), vb.reshape(1, batch, seq, H_B, DH_B),
            lf.reshape(1, batch, seq, H_B),
            ka_s.reshape(1, n_seq, n_tok, 2, H_A, DH_A), va_s.reshape(1, n_seq, n_tok, H_A, DV_A),
            kb_s.reshape(1, n_seq, n_tok, H_B, DH_B), vb_s.reshape(1, n_seq, n_tok, H_B, DH_B),
            lf_s.reshape(1, n_seq, n_tok, H_B))
```

```python
import functools
import math

import numpy as np
import jax
import jax.numpy as jnp
from jax import lax
from jax.experimental import pallas as pl
from jax.experimental.pallas import tpu as pltpu

F32 = jnp.float32
BF16 = jnp.bfloat16

D_MODEL = 1024
H_A, DH_A, DV_A = 4, 64, 128
H_B, DH_B = 8, 64
W_A = H_A * DV_A
W_B = H_B * DH_B
N_GROUPS = 8
EPS = 1e-6
NEG = -1e30
SLOPES = tuple(2.0 ** (-8.0 * (h + 1) / H_A) for h in range(H_A))
LANES = 128
AUG_COL = 64

TM = 256
TQ = 512
TK = 512
VT_CHUNK = TM
PAGES_PER_STEP = 4
FAST_PATH_MAX_BOUND = 30.0
VMEM_LIMIT = 56 * 1024 * 1024


def _dot(a, b):
    return jnp.dot(a, b, preferred_element_type=F32)


def _dot_nt(a, b):
    return lax.dot_general(a, b, (((1,), (1,)), ((), ())), preferred_element_type=F32)


def _split3(x):
    hi = x.astype(BF16)
    r1 = x - hi.astype(F32)
    mid = r1.astype(BF16)
    lo = (r1 - mid.astype(F32)).astype(BF16)
    return hi, mid, lo


def _silu(x):
    return x * jax.nn.sigmoid(x)


def _const_spec(shape):
    nd = len(shape)
    return pl.BlockSpec(shape, lambda *_: (0,) * nd, pipeline_mode=pl.Buffered(1))


def _ada_kernel(c_ref, w_ref, b_ref, o_ref):
    c = c_ref[...]
    o_ref[...] = _dot(_silu(c).astype(BF16), w_ref[...].astype(BF16)) + b_ref[...]


def _ada(c_all, w_ada, b_ada):
    rows = c_all.shape[0]
    n = w_ada.shape[1]
    bn = 1024
    return pl.pallas_call(
        _ada_kernel,
        grid=(n // bn,),
        in_specs=[pl.BlockSpec((rows, D_MODEL), lambda j: (0, 0)),
                  pl.BlockSpec((D_MODEL, bn), lambda j: (0, j)),
                  pl.BlockSpec((1, bn), lambda j: (0, j))],
        out_specs=pl.BlockSpec((rows, bn), lambda j: (0, j)),
        out_shape=jax.ShapeDtypeStruct((rows, n), F32),
        name="ada",
    )(c_all, w_ada, b_ada.reshape(1, n))


def _hidden(x_ref, scale_ref, shift_ref):
    x = x_ref[...]
    ms = jnp.mean(x * x, axis=-1, keepdims=True)
    h = x * lax.rsqrt(ms + EPS) * (1.0 + scale_ref[...]) + shift_ref[...]
    return h.astype(BF16)


def _group_rms(z, g_ref):
    zz = z * z
    hi = zz.astype(BF16)
    lo = (zz - hi.astype(F32)).astype(BF16)
    ss = _dot(hi, g_ref[...]) + _dot(lo, g_ref[...])
    return z * lax.rsqrt(ss * (1.0 / DH_A) + EPS)


def _log_sigmoid(z):
    return jnp.minimum(z, 0.0) - jnp.log1p(jnp.exp(-jnp.abs(z)))


def _logf(hb, wf_ref, bf_ref):
    z = _dot(hb, wf_ref[...]) + bf_ref[...]
    lane = lax.broadcasted_iota(jnp.int32, z.shape, 1)
    return jnp.where(lane < H_B, _log_sigmoid(z), 0.0)


def _proj_prompt_kernel(nb, mshift_ref, x_ref, scale_ref, shift_ref, wqt_ref, wn_ref, wf_ref, bf_ref,
                        gq_ref, gk_ref, g_ref, place_ref, self_ref, constb_ref, tri_ref,
                        ka_ref, va_ref, kb_ref, vb_ref, lf_ref,
                        qta_ref, kaa_ref, vta_ref, qtb_ref, kab_ref, vtb_ref,
                        sga_ref, sgb_ref, sma_ref, smb_ref, carry_ref):
    tm = x_ref.shape[0]
    ib = pl.program_id(0) % nb
    hb = _hidden(x_ref, scale_ref, shift_ref)

    lf = _logf(hb, wf_ref, bf_ref)
    lf_ref[...] = lf[:, :H_B]
    hi, mid, lo = _split3(lf)
    tri = tri_ref[...]
    f_local = _dot(tri, hi) + _dot(tri, mid) + _dot(tri, lo)

    @pl.when(ib == 0)
    def _():
        carry_ref[...] = jnp.zeros_like(carry_ref)

    f_nat = f_local + carry_ref[...]
    carry_ref[...] = f_nat[tm - 1:tm, :]
    f_t = f_nat.T[0:H_B, :]

    qt = _dot_nt(wqt_ref[...], hb)
    q3 = qt.reshape(2 * N_GROUPS, DH_A, tm)
    ssq = jnp.sum(q3 * q3, axis=1, keepdims=True)
    q3 = (qt * gq_ref[...]).reshape(2 * N_GROUPS, DH_A, tm) * lax.rsqrt(ssq * (1.0 / DH_A) + EPS)

    r8 = lax.broadcasted_iota(jnp.int32, (8, tm), 0)
    pos_q = ib * tm + lax.broadcasted_iota(jnp.int32, (8, tm), 1)
    a_q = (pos_q >> 7).astype(F32)
    b_q = (pos_q & 127).astype(F32)
    zeros_tail = jnp.zeros((LANES - DH_A - 8, tm), F32)
    m_a = mshift_ref[0]
    m_b = mshift_ref[1]
    for g in range(N_GROUPS):
        slope = SLOPES[g % H_A]
        aug = jnp.where(r8 == 0, -slope * 128.0 * a_q,
              jnp.where(r8 == 1, -slope * b_q,
              jnp.where(r8 == 2, slope * 128.0,
              jnp.where(r8 == 3, slope,
              jnp.where(r8 == 4, -m_a, 0.0)))))
        blk = jnp.concatenate([q3[g], aug, zeros_tail], axis=0)
        qta_ref[0, g * LANES:(g + 1) * LANES, :] = blk.astype(BF16)
    ft_hi, ft_mid, ft_lo = _split3(f_t)
    for g in range(N_GROUPS):
        bc = lambda v: jnp.broadcast_to(v[g:g + 1, :].astype(F32), (8, tm))
        aug = jnp.where(r8 == 0, bc(ft_hi),
              jnp.where(r8 == 1, bc(ft_mid),
              jnp.where(r8 == 2, bc(ft_lo),
              jnp.where(r8 <= 5, -1.0,
              jnp.where(r8 == 6, -m_b, 0.0)))))
        blk = jnp.concatenate([q3[N_GROUPS + g], aug, zeros_tail], axis=0)
        qtb_ref[0, g * LANES:(g + 1) * LANES, :] = blk.astype(BF16)

    ka = _group_rms(_dot(hb, wn_ref[:, 0:512]), g_ref) * gk_ref[:, 0:512]
    ka_ref[...] = ka
    lane = lax.broadcasted_iota(jnp.int32, (tm, LANES), 1)
    pos_k = ib * tm + lax.broadcasted_iota(jnp.int32, (tm, LANES), 0)
    aug_a = jnp.where(lane == AUG_COL + 2, (pos_k >> 7).astype(F32),
            jnp.where(lane == AUG_COL + 3, (pos_k & 127).astype(F32),
            jnp.where((lane == AUG_COL) | (lane == AUG_COL + 1) | (lane == AUG_COL + 4), 1.0, 0.0)))
    kaa = _dot(ka.astype(BF16), place_ref[...]) + jnp.concatenate([aug_a] * N_GROUPS, axis=1)
    kaa_ref[0] = kaa.astype(BF16)

    kb = _group_rms(_dot(hb, wn_ref[:, 512:1024]), g_ref) * gk_ref[:, 512:1024]
    kb_ref[...] = kb
    f_hi, f_mid, f_lo = _split3(f_nat)
    f3 = jnp.concatenate([f_hi, f_mid, f_lo], axis=1)
    kab = _dot(kb.astype(BF16), place_ref[...]) + _dot(f3, self_ref[...]) + constb_ref[...]
    kab_ref[0] = kab.astype(BF16)

    va = _dot(hb, wn_ref[:, 1024:1536])
    va_ref[...] = va
    vta_ref[0, 0] = va.T.astype(BF16)
    vb = _dot(hb, wn_ref[:, 1536:2048])
    vb_ref[...] = vb
    vtb_ref[0, 0] = vb.T.astype(BF16)

    sga_ref[...] = _silu(_dot(hb, wn_ref[:, 2048:2560])).astype(BF16)
    sgb_ref[...] = _silu(_dot(hb, wn_ref[:, 2560:3072])).astype(BF16)
    sma_ref[...] = jax.nn.sigmoid(_dot(hb, wn_ref[:, 3072:4096])).astype(BF16)
    smb_ref[...] = jax.nn.sigmoid(_dot(hb, wn_ref[:, 4096:5120])).astype(BF16)


def _proj_prompt(x2, scale, shift, mshift, wts, batch, seq):
    rows = batch * seq
    nb = seq // TM
    nk = seq // VT_CHUNK
    row = lambda w: pl.BlockSpec((TM, w), lambda i: (i, 0))
    mod = pl.BlockSpec((None, 1, D_MODEL), lambda i: (i // nb, 0, 0))
    consts = [wts["wqt"], wts["wn"], wts["wf"], wts["bf"], wts["gq_col"], wts["gk_row"], wts["gsum"],
              wts["place"], wts["self"], wts["constb"], wts["tri"]]
    out_shape = [jax.ShapeDtypeStruct((rows, 512), F32)] * 4 + [jax.ShapeDtypeStruct((rows, H_B), F32)] + [
        jax.ShapeDtypeStruct((batch, 1024, seq), BF16), jax.ShapeDtypeStruct((batch, seq, 1024), BF16),
        jax.ShapeDtypeStruct((batch, nk, W_A, VT_CHUNK), BF16),
        jax.ShapeDtypeStruct((batch, 1024, seq), BF16), jax.ShapeDtypeStruct((batch, seq, 1024), BF16),
        jax.ShapeDtypeStruct((batch, nk, W_B, VT_CHUNK), BF16),
        jax.ShapeDtypeStruct((rows, 512), BF16), jax.ShapeDtypeStruct((rows, 512), BF16),
        jax.ShapeDtypeStruct((rows, 1024), BF16), jax.ShapeDtypeStruct((rows, 1024), BF16)]
    qt_spec = pl.BlockSpec((1, 1024, TM), lambda i: (i // nb, 0, i % nb))
    k_spec = pl.BlockSpec((1, TM, 1024), lambda i: (i // nb, i % nb, 0))
    vt_spec = pl.BlockSpec((1, TM // VT_CHUNK, 512, VT_CHUNK), lambda i: (i // nb, i % nb, 0, 0))
    out_specs = [row(512)] * 4 + [row(H_B)] + [qt_spec, k_spec, vt_spec, qt_spec, k_spec, vt_spec,
                                               row(512), row(512), row(1024), row(1024)]
    return pl.pallas_call(
        functools.partial(_proj_prompt_kernel, nb),
        grid=(rows // TM,),
        in_specs=[pl.BlockSpec(memory_space=pltpu.SMEM), row(D_MODEL), mod, mod]
                 + [_const_spec(c.shape) for c in consts],
        out_specs=out_specs,
        out_shape=out_shape,
        scratch_shapes=[pltpu.VMEM((1, LANES), F32)],
        compiler_params=pltpu.CompilerParams(dimension_semantics=("arbitrary",),
                                             vmem_limit_bytes=VMEM_LIMIT),
        name="proj_prompt",
    )(mshift, x2, scale, shift, *consts)


def _proj_sample_kernel(x_ref, scale_ref, shift_ref, wq_ref, wn_ref, wf_ref, bf_ref, gq_ref, gk_ref, g_ref,
                        qa_ref, qb_ref, ka_ref, va_ref, kb_ref, vb_ref, lf_ref,
                        sga_ref, sgb_ref, sma_ref, smb_ref):
    hb = _hidden(x_ref, scale_ref, shift_ref)
    lf_ref[...] = _logf(hb, wf_ref, bf_ref)[:, :H_B]
    qa_ref[...] = _group_rms(_dot(hb, wq_ref[:, 0:512]), g_ref) * gq_ref[:, 0:512]
    qb_ref[...] = _group_rms(_dot(hb, wq_ref[:, 512:1024]), g_ref) * gq_ref[:, 512:1024]
    ka_ref[...] = _group_rms(_dot(hb, wn_ref[:, 0:512]), g_ref) * gk_ref[:, 0:512]
    kb_ref[...] = _group_rms(_dot(hb, wn_ref[:, 512:1024]), g_ref) * gk_ref[:, 512:1024]
    va_ref[...] = _dot(hb, wn_ref[:, 1024:1536])
    vb_ref[...] = _dot(hb, wn_ref[:, 1536:2048])
    sga_ref[...] = _silu(_dot(hb, wn_ref[:, 2048:2560])).astype(BF16)
    sgb_ref[...] = _silu(_dot(hb, wn_ref[:, 2560:3072])).astype(BF16)
    sma_ref[...] = jax.nn.sigmoid(_dot(hb, wn_ref[:, 3072:4096])).astype(BF16)
    smb_ref[...] = jax.nn.sigmoid(_dot(hb, wn_ref[:, 4096:5120])).astype(BF16)


def _proj_sample(x2, scale_rows, shift_rows, wts):
    rows = x2.shape[0]
    row = lambda w: pl.BlockSpec((TM, w), lambda i: (i, 0))
    consts = [wts["wq"], wts["wn"], wts["wf"], wts["bf"], wts["gq_row"], wts["gk_row"], wts["gsum"]]
    out_shape = [jax.ShapeDtypeStruct((rows, 512), F32)] * 6 + [jax.ShapeDtypeStruct((rows, H_B), F32)] + [
        jax.ShapeDtypeStruct((rows, 512), BF16), jax.ShapeDtypeStruct((rows, 512), BF16),
        jax.ShapeDtypeStruct((rows, 1024), BF16), jax.ShapeDtypeStruct((rows, 1024), BF16)]
    out_specs = [row(512)] * 6 + [row(H_B), row(512), row(512), row(1024), row(1024)]
    return pl.pallas_call(
        _proj_sample_kernel,
        grid=(rows // TM,),
        in_specs=[row(D_MODEL), row(D_MODEL), row(D_MODEL)] + [_const_spec(c.shape) for c in consts],
        out_specs=out_specs,
        out_shape=out_shape,
        compiler_params=pltpu.CompilerParams(dimension_semantics=("arbitrary",),
                                             vmem_limit_bytes=VMEM_LIMIT),
        name="proj_sample",
    )(x2, scale_rows, shift_rows, *consts)


def _attn_kernel(shared_v, online, lam_ref, qt0_ref, qt1_ref, k0_ref, k1_ref, vt_ref, o_ref, acc_ref):
    i = pl.program_id(2)
    qts = (qt0_ref[0], qt1_ref[0])
    k_refs = (k0_ref, k1_ref)
    dv = acc_ref.shape[1]
    n_chunks = TK // VT_CHUNK
    acc_ref[...] = jnp.zeros_like(acc_ref)

    def pv(u, j, p):
        out = None
        for c in range(n_chunks):
            blk = vt_ref[0, j * n_chunks + c]
            v = blk if shared_v else blk[u * dv:(u + 1) * dv, :]
            d = _dot(v, p[c * VT_CHUNK:(c + 1) * VT_CHUNK, :])
            out = d if out is None else out + d
        return out

    def step(j, carry, masked):
        out = []
        for u in range(2):
            kblk = k_refs[u][0, pl.ds(pl.multiple_of(j * TK, TK), TK), :]
            s = _dot(kblk, qts[u])
            if masked:
                kpos = lax.broadcasted_iota(jnp.int32, s.shape, 0)
                qpos = lax.broadcasted_iota(jnp.int32, s.shape, 1)
                s = jnp.where(kpos <= qpos, s, NEG)
            if online:
                m, l = carry[u]
                m_new = jnp.maximum(m, jnp.max(s, axis=0, keepdims=True))
                alpha = jnp.exp(m - m_new)
                p = jnp.exp(s - m_new)
                l = alpha * l + jnp.sum(p, axis=0, keepdims=True)
                acc_ref[u] = alpha * acc_ref[u] + pv(u, j, p.astype(BF16))
                out.append((m_new, l))
            else:
                p = jnp.exp(s)
                l = carry[u] + jnp.sum(p.reshape(TK // 8, 8, TQ), axis=0)
                acc_ref[u] += pv(u, j, p.astype(BF16))
                out.append(l)
        return tuple(out)

    if online:
        init = tuple((jnp.full((1, TQ), NEG, F32), jnp.zeros((1, TQ), F32)) for _ in range(2))
    else:
        init = tuple(jnp.zeros((8, TQ), F32) for _ in range(2))
    carry = lax.fori_loop(0, i, lambda j, c: step(j, c, False), init)
    carry = step(i, carry, True)
    if online:
        l0, l1 = carry[0][1], carry[1][1]
    else:
        l0, l1 = (jnp.sum(c, axis=0, keepdims=True) for c in carry)

    o0 = acc_ref[0] * (1.0 / l0)
    o1 = acc_ref[1] * (1.0 / l1)
    o = o0 - lam_ref[0] * o1 if shared_v else jnp.concatenate([o0, o1], axis=0)
    o_ref[0] = o.T


def _attention(qt, kaug, vt, lam, shared_v, online):
    batch, _, seq = qt.shape
    n_steps = 4
    if shared_v:
        g0 = lambda h: h
        g1 = lambda h: H_A + h
    else:
        g0 = lambda h: 2 * h
        g1 = lambda h: 2 * h + 1
    dv = DV_A if shared_v else DH_B
    qspec = lambda g: pl.BlockSpec((1, LANES, TQ), lambda b, h, i: (b, g(h), i))
    kspec = lambda g: pl.BlockSpec((1, seq, LANES), lambda b, h, i: (b, 0, g(h)))
    return pl.pallas_call(
        functools.partial(_attn_kernel, shared_v, online),
        grid=(batch, n_steps, seq // TQ),
        in_specs=[pl.BlockSpec(memory_space=pltpu.SMEM), qspec(g0), qspec(g1), kspec(g0), kspec(g1),
                  pl.BlockSpec((1, seq // VT_CHUNK, LANES, VT_CHUNK), lambda b, h, i: (b, 0, h, 0))],
        out_specs=pl.BlockSpec((1, TQ, LANES), lambda b, h, i: (b, i, h)),
        out_shape=jax.ShapeDtypeStruct((batch, seq, 512), F32),
        scratch_shapes=[pltpu.VMEM((2, dv, TQ), F32)],
        compiler_params=pltpu.CompilerParams(
            dimension_semantics=("arbitrary", "arbitrary", "arbitrary"), vmem_limit_bytes=VMEM_LIMIT),
        name=("attn_a" if shared_v else "attn_b") + ("_online" if online else ""),
    )(lam, qt, qt, kaug, kaug, vt)


def _decode_kernel(past, pps, pt_ref, lam_ref, qa_ref, qb_ref, *refs):
    del pt_ref
    kta_refs, va_refs, ktb_refs, vtb_refs, lft_refs = (refs[k * pps:(k + 1) * pps] for k in range(5))
    (kan_ref, van_ref, kbn_ref, vbn_ref, lfn_ref, upper_ref,
     oa_ref, ob_ref, qbd_ref, m_ref, l_ref, acca_ref, accb_ref, fc_ref) = refs[5 * pps:]
    step_i = pl.program_id(1)
    n_steps = pl.num_programs(1)
    rows = 32
    n_tok = 4
    new_pad = kan_ref.shape[1]
    page = LANES

    def row_ids(shape):
        r = lax.broadcasted_iota(jnp.int32, shape, 0)
        return ((r & 3, ((r >> 2) & 1) * H_A + (r >> 3), r >> 3),
                (r >> 3, r & 7, r & 7))

    def row_slope(head):
        return jnp.where(head == 0, SLOPES[0], jnp.where(head == 1, SLOPES[1],
               jnp.where(head == 2, SLOPES[2], SLOPES[3])))

    @pl.when(step_i == 0)
    def _():
        col_group = lax.broadcasted_iota(jnp.int32, (rows, 512), 1) >> 6
        for u, q_ref in enumerate((qa_ref, qb_ref)):
            tok, group, _ = row_ids((rows, 512))[u]
            qbd = jnp.zeros((rows, 512), F32)
            for t in range(n_tok):
                q_row = jnp.broadcast_to(q_ref[0, t:t + 1, :], (rows, 512))
                qbd = jnp.where((tok == t) & (group == col_group), q_row, qbd)
            qbd_ref[u] = qbd.astype(BF16)
        m_ref[...] = jnp.full(m_ref.shape, NEG, F32)
        l_ref[...] = jnp.zeros_like(l_ref)
        acca_ref[...] = jnp.zeros_like(acca_ref)
        accb_ref[...] = jnp.zeros_like(accb_ref)
        fc_ref[...] = jnp.zeros_like(fc_ref)

    def softmax_update(u, s):
        m_prev = m_ref[u][:, 0:1]
        l_prev = l_ref[u][:, 0:1]
        m_new = jnp.maximum(m_prev, jnp.max(s, axis=1, keepdims=True))
        alpha = jnp.exp(m_prev - m_new)
        pr = jnp.exp(s - m_new)
        l_new = alpha * l_prev + jnp.sum(pr, axis=1, keepdims=True)
        m_ref[u] = jnp.broadcast_to(m_new, (rows, LANES))
        l_ref[u] = jnp.broadcast_to(l_new, (rows, LANES))
        return alpha, pr

    def pv_a(pr, value_of_head):
        head = row_ids(pr.shape)[0][2]
        out = None
        for h in range(H_A):
            d = _dot(jnp.where(head == h, pr, 0.0).astype(BF16), value_of_head(h))
            out = d if out is None else out + d
        return out

    width = pps * page
    lane = lax.broadcasted_iota(jnp.int32, (rows, width), 1)
    (tok_a, _, head_a), _ = row_ids((rows, width))
    dist = (past + tok_a - (step_i * width + lane)).astype(F32)
    s_a = jnp.concatenate([_dot(qbd_ref[0], r[0].astype(BF16)) for r in kta_refs], axis=1)
    alpha, pr = softmax_update(0, s_a - row_slope(head_a) * dist)
    upd = None
    for j in range(pps):
        d = pv_a(pr[:, j * page:(j + 1) * page],
                 lambda h: va_refs[j][0, pl.ds(h, page, stride=H_A), :].astype(BF16))
        upd = d if upd is None else upd + d
    acca_ref[...] = alpha * acca_ref[...] + upd

    f_carry = fc_ref[...][:, 0:1]
    f_pages = []
    for j in range(pps):
        hi, mid, lo = (v.astype(F32) for v in _split3(lft_refs[j][0]))
        lf3 = jnp.concatenate([hi, mid, lo, jnp.zeros_like(hi)], axis=0).astype(BF16)
        cs = _dot(lf3, upper_ref[...])
        f_page = cs[0:8] + cs[8:16] + cs[16:24] + f_carry
        f_carry = f_page[:, page - 1:page]
        f_pages.append(jnp.concatenate([f_page] * n_tok, axis=0))
    fc_ref[...] = jnp.broadcast_to(f_carry, (8, LANES))
    s_b = jnp.concatenate([_dot(qbd_ref[1], r[0].astype(BF16)) for r in ktb_refs], axis=1)
    alpha, pr = softmax_update(1, s_b - jnp.concatenate(f_pages, axis=1))
    upd = None
    for j in range(pps):
        d = _dot_nt(pr[:, j * page:(j + 1) * page].astype(BF16), vtb_refs[j][0].astype(BF16))
        upd = d if upd is None else upd + d
    accb_ref[...] = alpha * accb_ref[...] + upd

    @pl.when(step_i == n_steps - 1)
    def _():
        tn = lax.broadcasted_iota(jnp.int32, (rows, new_pad), 1)
        (tok_a, _, head_a), (tok_b, _, _) = row_ids((rows, new_pad))
        s_an = _dot_nt(qbd_ref[0], kan_ref[0].astype(BF16)) - row_slope(head_a) * (tok_a - tn).astype(F32)
        alpha, pr = softmax_update(0, jnp.where((tn <= tok_a) & (tn < n_tok), s_an, NEG))
        van = van_ref[0].astype(BF16)
        acca_ref[...] = alpha * acca_ref[...] + pv_a(pr, lambda h: van[:, h * DV_A:(h + 1) * DV_A])

        lfn = lfn_ref[0]
        t8 = lax.broadcasted_iota(jnp.int32, (8, new_pad), 1)
        f_new = jnp.broadcast_to(f_carry, (8, new_pad))
        for t in range(n_tok):
            f_new = f_new + jnp.where(t8 >= t, jnp.broadcast_to(lfn[:, t:t + 1], (8, new_pad)), 0.0)
        s_bn = _dot_nt(qbd_ref[1], kbn_ref[0].astype(BF16)) - jnp.concatenate([f_new] * n_tok, axis=0)
        alpha, pr = softmax_update(1, jnp.where((tn <= tok_b) & (tn < n_tok), s_bn, NEG))
        accb_ref[...] = alpha * accb_ref[...] + _dot(pr.astype(BF16), vbn_ref[0].astype(BF16))

        on_a = acca_ref[...] * (1.0 / l_ref[0][:, 0:1])
        for h in range(H_A):
            oa_ref[0, :, h * DV_A:(h + 1) * DV_A] = (on_a[8 * h:8 * h + n_tok]
                                                     - lam_ref[0] * on_a[8 * h + n_tok:8 * h + 2 * n_tok])
        head_b = row_ids((rows, 512))[1][2]
        col_head = lax.broadcasted_iota(jnp.int32, (rows, 512), 1) >> 6
        on_b = jnp.where(head_b == col_head, accb_ref[...] * (1.0 / l_ref[1][:, 0:1]), 0.0)
        for t in range(n_tok):
            ob_ref[0, t:t + 1, :] = jnp.sum(on_b[8 * t:8 * t + 8], axis=0, keepdims=True)


def _decode(page_table, lam, qa, qb, pools, news, lfn_t, upper):
    n_seq, n_pages = page_table.shape
    page = pools[0].shape[2]
    pps = PAGES_PER_STEP
    new_pad = news[0].shape[1]
    past = n_pages * page
    pool_spec = lambda r, j: pl.BlockSpec((1, r, page), lambda n, s, pt: (pt[n * n_pages + s * pps + j], 0, 0))
    seq_spec = lambda r: pl.BlockSpec((1, r, 512), lambda n, s, pt: (n, 0, 0))
    pool_specs, pool_args = [], []
    for pool in pools:
        for j in range(pps):
            pool_specs.append(pool_spec(pool.shape[1], j))
            pool_args.append(pool)
    grid_spec = pltpu.PrefetchScalarGridSpec(
        num_scalar_prefetch=1,
        grid=(n_seq, n_pages // pps),
        in_specs=[pl.BlockSpec(memory_space=pltpu.SMEM), seq_spec(4), seq_spec(4)] + pool_specs
                 + [seq_spec(new_pad)] * 4
                 + [pl.BlockSpec((1, H_B, new_pad), lambda n, s, pt: (n, 0, 0)),
                    pl.BlockSpec((page, page), lambda n, s, pt: (0, 0))],
        out_specs=[seq_spec(4), seq_spec(4)],
        scratch_shapes=[pltpu.VMEM((2, 32, 512), BF16), pltpu.VMEM((2, 32, LANES), F32),
                        pltpu.VMEM((2, 32, LANES), F32), pltpu.VMEM((32, DV_A), F32),
                        pltpu.VMEM((32, 512), F32), pltpu.VMEM((H_B, LANES), F32)],
    )
    return pl.pallas_call(
        functools.partial(_decode_kernel, past, pps),
        grid_spec=grid_spec,
        out_shape=[jax.ShapeDtypeStruct((n_seq, 4, 512), F32)] * 2,
        compiler_params=pltpu.CompilerParams(dimension_semantics=("arbitrary", "arbitrary"),
                                             vmem_limit_bytes=VMEM_LIMIT),
        name="decode",
    )(page_table.reshape(-1), lam, qa, qb, *pool_args, *news, lfn_t, upper)


def _merge_kernel(x_ref, oa_ref, ob_ref, sga_ref, sgb_ref, sma_ref, smb_ref, gres_ref, gsub_ref,
                  woa_ref, wob_ref, wo_ref, y_ref):
    oa = oa_ref[...]
    heads = []
    for h in range(H_A):
        oh = oa[:, h * DV_A:(h + 1) * DV_A]
        heads.append(oh * lax.rsqrt(jnp.mean(oh * oh, axis=-1, keepdims=True) + EPS))
    oa = jnp.concatenate(heads, axis=1) * gsub_ref[...]
    ya = _dot((oa * sga_ref[...].astype(F32)).astype(BF16), woa_ref[...])
    yb = _dot((ob_ref[...] * sgb_ref[...].astype(F32)).astype(BF16), wob_ref[...])
    mix = sma_ref[...].astype(F32) * ya + smb_ref[...].astype(F32) * yb
    y = _dot(mix.astype(BF16), wo_ref[...])
    y_ref[...] = x_ref[...] + gres_ref[...] * y


def _merge(x2, oa, ob, sga, sgb, sma, smb, gres, gres_spec, wts):
    rows = x2.shape[0]
    row = lambda w: pl.BlockSpec((TM, w), lambda i: (i, 0))
    consts = [wts["gsub"], wts["woa"], wts["wob"], wts["wo"]]
    return pl.pallas_call(
        _merge_kernel,
        grid=(rows // TM,),
        in_specs=[row(D_MODEL), row(512), row(512), row(512), row(512), row(1024), row(1024), gres_spec]
                 + [_const_spec(c.shape) for c in consts],
        out_specs=row(D_MODEL),
        out_shape=jax.ShapeDtypeStruct((rows, D_MODEL), F32),
        compiler_params=pltpu.CompilerParams(dimension_semantics=("arbitrary",),
                                             vmem_limit_bytes=VMEM_LIMIT),
        name="merge",
    )(x2, oa, ob, sga, sgb, sma, smb, gres, *consts)


def _placement_constants():
    gsum = np.kron(np.eye(N_GROUPS), np.ones((DH_A, DH_A)))
    place = np.zeros((512, 1024))
    for g in range(N_GROUPS):
        place[g * DH_A + np.arange(DH_A), g * LANES + np.arange(DH_A)] = 1.0
    sel = np.zeros((3 * LANES, 1024))
    constb = np.zeros((1, 1024))
    for g in range(N_GROUPS):
        for part in range(3):
            sel[part * LANES + g, g * LANES + AUG_COL + 3 + part] = 1.0
        constb[0, g * LANES + AUG_COL + np.array([0, 1, 2, 6])] = 1.0
    tri = np.tril(np.ones((TM, TM)))
    upper = np.triu(np.ones((LANES, LANES)))
    bf = lambda a: jnp.asarray(a, BF16)
    return dict(gsum=bf(gsum), place=bf(place), self=bf(sel), constb=jnp.asarray(constb, F32),
                tri=bf(tri), upper=bf(upper))


def _bf16_ceil(x):
    y = x.astype(BF16).astype(F32)
    return jnp.where(y < x, y * (1.0 + 2.0 ** -7), y)


def kernel(x_prompt, x_sample, cache_a_k, cache_a_v, cache_b_k, cache_b_v, cache_b_logf, page_table,
           c_prompt, c_sample, w_ada, b_ada, w_in, b_f, g_q_a, g_k_a, g_q_b, g_k_b,
           lambda_q1, lambda_k1, lambda_q2, lambda_k2, g_sub_a, w_out_a, w_out_b, w_o):
    assert w_ada.shape[0] == 1, "single-layer step"
    batch, seq, _ = x_prompt.shape
    n_seq, n_tok, _ = x_sample.shape
    n_pool, page = cache_a_k.shape[1], cache_a_k.shape[2]
    layer = 0
    lam_init = 0.8 - 0.6 * math.exp(-0.3 * layer)
    lam = (jnp.exp(jnp.sum(lambda_q1[layer] * lambda_k1[layer]))
           - jnp.exp(jnp.sum(lambda_q2[layer] * lambda_k2[layer])) + lam_init).reshape(1).astype(F32)

    w = w_in[layer]
    sec = np.cumsum((0, 512, 512, 512, 512, 512, 512, 512, H_B, 512, 1024, 1024))
    cols = lambda k: w[:, sec[k]:sec[k + 1]]
    wq = jnp.concatenate([cols(0), cols(4)], axis=1).astype(BF16)
    wn = jnp.concatenate([cols(1), cols(5), cols(2), cols(6), cols(3), cols(8), cols(9), cols(10)],
                         axis=1).astype(BF16)
    wf = jnp.pad(cols(7), ((0, 0), (0, LANES - H_B))).astype(BF16)
    gq = jnp.concatenate([jnp.tile(g_q_a[layer], N_GROUPS) * DH_A ** -0.5,
                          jnp.tile(g_q_b[layer], N_GROUPS) * DH_B ** -0.5])
    gk = jnp.concatenate([jnp.tile(g_k_a[layer], N_GROUPS), jnp.tile(g_k_b[layer], N_GROUPS)])
    wts = dict(_placement_constants(),
               wq=wq, wqt=wq.T, wn=wn, wf=wf, bf=jnp.pad(b_f[layer], (0, LANES - H_B)).reshape(1, LANES),
               gq_col=gq.reshape(-1, 1), gq_row=gq.reshape(1, -1), gk_row=gk.reshape(1, -1),
               gsub=(jnp.tile(g_sub_a[layer], H_A) * (1.0 - lam_init)).reshape(1, W_A),
               woa=w_out_a[layer].astype(BF16), wob=w_out_b[layer].astype(BF16), wo=w_o[layer].astype(BF16))
    bound = lambda gq_, gk_: _bf16_ceil(8.1 * jnp.max(jnp.abs(gq_ * gk_)))
    mshift = jnp.stack([bound(g_q_a[layer], g_k_a[layer]), bound(g_q_b[layer], g_k_b[layer])]).astype(F32)

    n_c = batch + n_seq
    c_all = jnp.pad(jnp.concatenate([c_prompt, c_sample], axis=0), ((0, -n_c % 8), (0, 0)))
    mod = _ada(c_all, w_ada[layer], b_ada[layer])
    shift, scale, gres = (mod[:, k * D_MODEL:(k + 1) * D_MODEL] for k in range(3))

    xp2 = x_prompt.reshape(batch * seq, D_MODEL)
    p3 = lambda a: a[:batch].reshape(batch, 1, D_MODEL)
    (ka, va, kb, vb, lf, qta, kaa, vta, qtb, kab, vtb, sga, sgb, sma, smb) = _proj_prompt(
        xp2, p3(scale), p3(shift), mshift, wts, batch, seq)
    online = jnp.max(mshift) > FAST_PATH_MAX_BOUND
    attend = lambda *a: lax.cond(online, lambda: _attention(*a, True), lambda: _attention(*a, False))
    oa = attend(qta, kaa, vta, lam, True).reshape(batch * seq, W_A)
    ob = attend(qtb, kab, vtb, lam, False).reshape(batch * seq, W_B)
    nb = seq // TM
    gres_p_spec = pl.BlockSpec((None, 1, D_MODEL), lambda i: (i // nb, 0, 0))
    yp = _merge(xp2, oa, ob, sga, sgb, sma, smb, p3(gres), gres_p_spec, wts).reshape(batch, seq, D_MODEL)

    xs2 = x_sample.reshape(n_seq * n_tok, D_MODEL)
    rep = lambda a: jnp.repeat(a[batch:n_c], n_tok, axis=0)
    (qa_s, qb_s, ka_s, va_s, kb_s, vb_s, lf_s, sga_s, sgb_s, sma_s, smb_s) = _proj_sample(
        xs2, rep(scale), rep(shift), wts)
    new_pad = 16
    seq3 = lambda a: a.reshape(n_seq, n_tok, 512)
    padn = lambda a: jnp.pad(seq3(a), ((0, 0), (0, new_pad - n_tok), (0, 0)))
    pools = [jnp.transpose(cache_a_k[layer], (0, 2, 3, 4, 1)).reshape(n_pool, 512, page),
             cache_a_v[layer].reshape(n_pool, page * H_A, DV_A),
             jnp.transpose(cache_b_k[layer], (0, 2, 3, 1)).reshape(n_pool, 512, page),
             jnp.transpose(cache_b_v[layer], (0, 2, 3, 1)).reshape(n_pool, 512, page),
             jnp.swapaxes(cache_b_logf[layer], 1, 2)]
    lfn_t = jnp.pad(jnp.swapaxes(lf_s.reshape(n_seq, n_tok, H_B), 1, 2), ((0, 0), (0, 0), (0, new_pad - n_tok)))
    oa_s, ob_s = _decode(page_table, lam, seq3(qa_s), seq3(qb_s), pools,
                         [padn(ka_s), padn(va_s), padn(kb_s), padn(vb_s)], lfn_t, wts["upper"])
    gres_s_spec = pl.BlockSpec((TM, D_MODEL), lambda i: (i, 0))
    ys = _merge(xs2, oa_s.reshape(-1, W_A), ob_s.reshape(-1, W_B), sga_s, sgb_s, sma_s, smb_s,
                rep(gres), gres_s_spec, wts).reshape(n_seq, n_tok, D_MODEL)

    return (yp, ys,
            ka.reshape(1, batch, seq, 2, H_A, DH_A), va.reshape(1, batch, seq, H_A, DV_A),
            kb.reshape(1, batch, seq, H_B, DH_B), vb.reshape(1, batch, seq, H_B, DH_B),
            lf.reshape(1, batch, seq, H_B),
            ka_s.reshape(1, n_seq, n_tok, 2, H_A, DH_A), va_s.reshape(1, n_seq, n_tok, H_A, DV_A),
            kb_s.reshape(1, n_seq, n_tok, H_B, DH_B), vb_s.reshape(1, n_seq, n_tok, H_B, DH_B),
            lf_s.reshape(1, n_seq, n_tok, H_B))
```

```python
import functools
import math

import numpy as np
import jax
import jax.numpy as jnp
from jax import lax
from jax.experimental import pallas as pl
from jax.experimental.pallas import tpu as pltpu

F32 = jnp.float32
BF16 = jnp.bfloat16

D_MODEL = 1024
H_A, DH_A, DV_A = 4, 64, 128
H_B, DH_B = 8, 64
W_A = H_A * DV_A
W_B = H_B * DH_B
N_GROUPS = 8
EPS = 1e-6
NEG = -1e30
SLOPES = tuple(2.0 ** (-8.0 * (h + 1) / H_A) for h in range(H_A))
LANES = 128
AUG_COL = 64

TM = 256
TQ = 512
TK = 512
VT_CHUNK = TM
PAGES_PER_STEP = 8
FAST_PATH_MAX_BOUND = 30.0
VMEM_LIMIT = 56 * 1024 * 1024


def _dot(a, b):
    return jnp.dot(a, b, preferred_element_type=F32)


def _dot_nt(a, b):
    return lax.dot_general(a, b, (((1,), (1,)), ((), ())), preferred_element_type=F32)


def _split3(x):
    hi = x.astype(BF16)
    r1 = x - hi.astype(F32)
    mid = r1.astype(BF16)
    lo = (r1 - mid.astype(F32)).astype(BF16)
    return hi, mid, lo


def _silu(x):
    return x * jax.nn.sigmoid(x)


def _const_spec(shape):
    nd = len(shape)
    return pl.BlockSpec(shape, lambda *_: (0,) * nd, pipeline_mode=pl.Buffered(1))


def _ada_kernel(c_ref, w_ref, b_ref, o_ref):
    c = c_ref[...]
    o_ref[...] = _dot(_silu(c).astype(BF16), w_ref[...].astype(BF16)) + b_ref[...]


def _ada(c_all, w_ada, b_ada):
    rows = c_all.shape[0]
    n = w_ada.shape[1]
    bn = 1024
    return pl.pallas_call(
        _ada_kernel,
        grid=(n // bn,),
        in_specs=[pl.BlockSpec((rows, D_MODEL), lambda j: (0, 0)),
                  pl.BlockSpec((D_MODEL, bn), lambda j: (0, j)),
                  pl.BlockSpec((1, bn), lambda j: (0, j))],
        out_specs=pl.BlockSpec((rows, bn), lambda j: (0, j)),
        out_shape=jax.ShapeDtypeStruct((rows, n), F32),
        name="ada",
    )(c_all, w_ada, b_ada.reshape(1, n))


def _hidden(x_ref, scale_ref, shift_ref):
    x = x_ref[...]
    ms = jnp.mean(x * x, axis=-1, keepdims=True)
    h = x * lax.rsqrt(ms + EPS) * (1.0 + scale_ref[...]) + shift_ref[...]
    return h.astype(BF16)


def _group_rms(z, g_ref):
    zz = z * z
    hi = zz.astype(BF16)
    lo = (zz - hi.astype(F32)).astype(BF16)
    ss = _dot(hi, g_ref[...]) + _dot(lo, g_ref[...])
    return z * lax.rsqrt(ss * (1.0 / DH_A) + EPS)


def _log_sigmoid(z):
    return jnp.minimum(z, 0.0) - jnp.log1p(jnp.exp(-jnp.abs(z)))


def _logf(hb, wf_ref, bf_ref):
    z = _dot(hb, wf_ref[...]) + bf_ref[...]
    lane = lax.broadcasted_iota(jnp.int32, z.shape, 1)
    return jnp.where(lane < H_B, _log_sigmoid(z), 0.0)


def _proj_prompt_kernel(nb, mshift_ref, x_ref, scale_ref, shift_ref, wqt_ref, wn_ref, wf_ref, bf_ref,
                        gq_ref, gk_ref, g_ref, place_ref, self_ref, constb_ref, tri_ref,
                        ka_ref, va_ref, kb_ref, vb_ref, lf_ref,
                        qta_ref, kaa_ref, vta_ref, qtb_ref, kab_ref, vtb_ref,
                        sga_ref, sgb_ref, sma_ref, smb_ref, carry_ref):
    tm = x_ref.shape[0]
    ib = pl.program_id(0) % nb
    hb = _hidden(x_ref, scale_ref, shift_ref)

    lf = _logf(hb, wf_ref, bf_ref)
    lf_ref[...] = lf[:, :H_B]
    hi, mid, lo = _split3(lf)
    tri = tri_ref[...]
    f_local = _dot(tri, hi) + _dot(tri, mid) + _dot(tri, lo)

    @pl.when(ib == 0)
    def _():
        carry_ref[...] = jnp.zeros_like(carry_ref)

    f_nat = f_local + carry_ref[...]
    carry_ref[...] = f_nat[tm - 1:tm, :]
    f_t = f_nat.T[0:H_B, :]

    qt = _dot_nt(wqt_ref[...], hb)
    q3 = qt.reshape(2 * N_GROUPS, DH_A, tm)
    ssq = jnp.sum(q3 * q3, axis=1, keepdims=True)
    q3 = (qt * gq_ref[...]).reshape(2 * N_GROUPS, DH_A, tm) * lax.rsqrt(ssq * (1.0 / DH_A) + EPS)

    r8 = lax.broadcasted_iota(jnp.int32, (8, tm), 0)
    pos_q = ib * tm + lax.broadcasted_iota(jnp.int32, (8, tm), 1)
    a_q = (pos_q >> 7).astype(F32)
    b_q = (pos_q & 127).astype(F32)
    zeros_tail = jnp.zeros((LANES - DH_A - 8, tm), F32)
    m_a = mshift_ref[0]
    m_b = mshift_ref[1]
    for g in range(N_GROUPS):
        slope = SLOPES[g % H_A]
        aug = jnp.where(r8 == 0, -slope * 128.0 * a_q,
              jnp.where(r8 == 1, -slope * b_q,
              jnp.where(r8 == 2, slope * 128.0,
              jnp.where(r8 == 3, slope,
              jnp.where(r8 == 4, -m_a, 0.0)))))
        blk = jnp.concatenate([q3[g], aug, zeros_tail], axis=0)
        qta_ref[0, g * LANES:(g + 1) * LANES, :] = blk.astype(BF16)
    ft_hi, ft_mid, ft_lo = _split3(f_t)
    for g in range(N_GROUPS):
        bc = lambda v: jnp.broadcast_to(v[g:g + 1, :].astype(F32), (8, tm))
        aug = jnp.where(r8 == 0, bc(ft_hi),
              jnp.where(r8 == 1, bc(ft_mid),
              jnp.where(r8 == 2, bc(ft_lo),
              jnp.where(r8 <= 5, -1.0,
              jnp.where(r8 == 6, -m_b, 0.0)))))
        blk = jnp.concatenate([q3[N_GROUPS + g], aug, zeros_tail], axis=0)
        qtb_ref[0, g * LANES:(g + 1) * LANES, :] = blk.astype(BF16)

    ka = _group_rms(_dot(hb, wn_ref[:, 0:512]), g_ref) * gk_ref[:, 0:512]
    ka_ref[...] = ka
    lane = lax.broadcasted_iota(jnp.int32, (tm, LANES), 1)
    pos_k = ib * tm + lax.broadcasted_iota(jnp.int32, (tm, LANES), 0)
    aug_a = jnp.where(lane == AUG_COL + 2, (pos_k >> 7).astype(F32),
            jnp.where(lane == AUG_COL + 3, (pos_k & 127).astype(F32),
            jnp.where((lane == AUG_COL) | (lane == AUG_COL + 1) | (lane == AUG_COL + 4), 1.0, 0.0)))
    kaa = _dot(ka.astype(BF16), place_ref[...]) + jnp.concatenate([aug_a] * N_GROUPS, axis=1)
    kaa_ref[0] = kaa.astype(BF16)

    kb = _group_rms(_dot(hb, wn_ref[:, 512:1024]), g_ref) * gk_ref[:, 512:1024]
    kb_ref[...] = kb
    f_hi, f_mid, f_lo = _split3(f_nat)
    f3 = jnp.concatenate([f_hi, f_mid, f_lo], axis=1)
    kab = _dot(kb.astype(BF16), place_ref[...]) + _dot(f3, self_ref[...]) + constb_ref[...]
    kab_ref[0] = kab.astype(BF16)

    va = _dot(hb, wn_ref[:, 1024:1536])
    va_ref[...] = va
    vta_ref[0, 0] = va.T.astype(BF16)
    vb = _dot(hb, wn_ref[:, 1536:2048])
    vb_ref[...] = vb
    vtb_ref[0, 0] = vb.T.astype(BF16)

    sga_ref[...] = _silu(_dot(hb, wn_ref[:, 2048:2560])).astype(BF16)
    sgb_ref[...] = _silu(_dot(hb, wn_ref[:, 2560:3072])).astype(BF16)
    sma_ref[...] = jax.nn.sigmoid(_dot(hb, wn_ref[:, 3072:4096])).astype(BF16)
    smb_ref[...] = jax.nn.sigmoid(_dot(hb, wn_ref[:, 4096:5120])).astype(BF16)


def _proj_prompt(x2, scale, shift, mshift, wts, batch, seq):
    rows = batch * seq
    nb = seq // TM
    nk = seq // VT_CHUNK
    row = lambda w: pl.BlockSpec((TM, w), lambda i: (i, 0))
    mod = pl.BlockSpec((None, 1, D_MODEL), lambda i: (i // nb, 0, 0))
    consts = [wts["wqt"], wts["wn"], wts["wf"], wts["bf"], wts["gq_col"], wts["gk_row"], wts["gsum"],
              wts["place"], wts["self"], wts["constb"], wts["tri"]]
    out_shape = [jax.ShapeDtypeStruct((rows, 512), F32)] * 4 + [jax.ShapeDtypeStruct((rows, H_B), F32)] + [
        jax.ShapeDtypeStruct((batch, 1024, seq), BF16), jax.ShapeDtypeStruct((batch, seq, 1024), BF16),
        jax.ShapeDtypeStruct((batch, nk, W_A, VT_CHUNK), BF16),
        jax.ShapeDtypeStruct((batch, 1024, seq), BF16), jax.ShapeDtypeStruct((batch, seq, 1024), BF16),
        jax.ShapeDtypeStruct((batch, nk, W_B, VT_CHUNK), BF16),
        jax.ShapeDtypeStruct((rows, 512), BF16), jax.ShapeDtypeStruct((rows, 512), BF16),
        jax.ShapeDtypeStruct((rows, 1024), BF16), jax.ShapeDtypeStruct((rows, 1024), BF16)]
    qt_spec = pl.BlockSpec((1, 1024, TM), lambda i: (i // nb, 0, i % nb))
    k_spec = pl.BlockSpec((1, TM, 1024), lambda i: (i // nb, i % nb, 0))
    vt_spec = pl.BlockSpec((1, TM // VT_CHUNK, 512, VT_CHUNK), lambda i: (i // nb, i % nb, 0, 0))
    out_specs = [row(512)] * 4 + [row(H_B)] + [qt_spec, k_spec, vt_spec, qt_spec, k_spec, vt_spec,
                                               row(512), row(512), row(1024), row(1024)]
    return pl.pallas_call(
        functools.partial(_proj_prompt_kernel, nb),
        grid=(rows // TM,),
        in_specs=[pl.BlockSpec(memory_space=pltpu.SMEM), row(D_MODEL), mod, mod]
                 + [_const_spec(c.shape) for c in consts],
        out_specs=out_specs,
        out_shape=out_shape,
        scratch_shapes=[pltpu.VMEM((1, LANES), F32)],
        compiler_params=pltpu.CompilerParams(dimension_semantics=("arbitrary",),
                                             vmem_limit_bytes=VMEM_LIMIT),
        name="proj_prompt",
    )(mshift, x2, scale, shift, *consts)


def _proj_sample_kernel(x_ref, scale_ref, shift_ref, wq_ref, wn_ref, wf_ref, bf_ref, gq_ref, gk_ref, g_ref,
                        qa_ref, qb_ref, ka_ref, va_ref, kb_ref, vb_ref, lf_ref,
                        sga_ref, sgb_ref, sma_ref, smb_ref):
    hb = _hidden(x_ref, scale_ref, shift_ref)
    lf_ref[...] = _logf(hb, wf_ref, bf_ref)[:, :H_B]
    qa_ref[...] = _group_rms(_dot(hb, wq_ref[:, 0:512]), g_ref) * gq_ref[:, 0:512]
    qb_ref[...] = _group_rms(_dot(hb, wq_ref[:, 512:1024]), g_ref) * gq_ref[:, 512:1024]
    ka_ref[...] = _group_rms(_dot(hb, wn_ref[:, 0:512]), g_ref) * gk_ref[:, 0:512]
    kb_ref[...] = _group_rms(_dot(hb, wn_ref[:, 512:1024]), g_ref) * gk_ref[:, 512:1024]
    va_ref[...] = _dot(hb, wn_ref[:, 1024:1536])
    vb_ref[...] = _dot(hb, wn_ref[:, 1536:2048])
    sga_ref[...] = _silu(_dot(hb, wn_ref[:, 2048:2560])).astype(BF16)
    sgb_ref[...] = _silu(_dot(hb, wn_ref[:, 2560:3072])).astype(BF16)
    sma_ref[...] = jax.nn.sigmoid(_dot(hb, wn_ref[:, 3072:4096])).astype(BF16)
    smb_ref[...] = jax.nn.sigmoid(_dot(hb, wn_ref[:, 4096:5120])).astype(BF16)


def _proj_sample(x2, scale_rows, shift_rows, wts):
    rows = x2.shape[0]
    row = lambda w: pl.BlockSpec((TM, w), lambda i: (i, 0))
    consts = [wts["wq"], wts["wn"], wts["wf"], wts["bf"], wts["gq_row"], wts["gk_row"], wts["gsum"]]
    out_shape = [jax.ShapeDtypeStruct((rows, 512), F32)] * 6 + [jax.ShapeDtypeStruct((rows, H_B), F32)] + [
        jax.ShapeDtypeStruct((rows, 512), BF16), jax.ShapeDtypeStruct((rows, 512), BF16),
        jax.ShapeDtypeStruct((rows, 1024), BF16), jax.ShapeDtypeStruct((rows, 1024), BF16)]
    out_specs = [row(512)] * 6 + [row(H_B), row(512), row(512), row(1024), row(1024)]
    return pl.pallas_call(
        _proj_sample_kernel,
        grid=(rows // TM,),
        in_specs=[row(D_MODEL), row(D_MODEL), row(D_MODEL)] + [_const_spec(c.shape) for c in consts],
        out_specs=out_specs,
        out_shape=out_shape,
        compiler_params=pltpu.CompilerParams(dimension_semantics=("arbitrary",),
                                             vmem_limit_bytes=VMEM_LIMIT),
        name="proj_sample",
    )(x2, scale_rows, shift_rows, *consts)


def _attn_kernel(shared_v, online, lam_ref, qt0_ref, qt1_ref, k0_ref, k1_ref, vt_ref, o_ref,
                 acc_ref, l_ref, p_ref, pd_ref):
    i = pl.program_id(2)
    qts = (qt0_ref[0], qt1_ref[0])
    k_refs = (k0_ref, k1_ref)
    dv = acc_ref.shape[1]
    n_chunks = TK // VT_CHUNK
    acc_ref[...] = jnp.zeros_like(acc_ref)

    def pv(u, chunk, p):
        blk = vt_ref[0, chunk]
        v = blk if shared_v else blk[u * dv:(u + 1) * dv, :]
        return _dot(v, p)

    def scores_to_probs(j, masked, dst_ref):
        for u in range(2):
            kblk = k_refs[u][0, pl.ds(pl.multiple_of(j * TK, TK), TK), :]
            s = _dot(kblk, qts[u])
            if masked:
                kpos = lax.broadcasted_iota(jnp.int32, s.shape, 0)
                qpos = lax.broadcasted_iota(jnp.int32, s.shape, 1)
                s = jnp.where(kpos <= qpos, s, NEG)
            p = jnp.exp(s)
            l_ref[u] += jnp.sum(p.reshape(TK // 8, 8, TQ), axis=0)
            dst_ref[u] = p.astype(BF16)

    def probs_times_values(j, src_ref):
        for u in range(2):
            upd = None
            for c in range(n_chunks):
                d = pv(u, j * n_chunks + c, src_ref[u, c * VT_CHUNK:(c + 1) * VT_CHUNK, :])
                upd = d if upd is None else upd + d
            acc_ref[u] += upd

    def step_online(j, carry, masked):
        out = []
        for u in range(2):
            kblk = k_refs[u][0, pl.ds(pl.multiple_of(j * TK, TK), TK), :]
            s = _dot(kblk, qts[u])
            if masked:
                kpos = lax.broadcasted_iota(jnp.int32, s.shape, 0)
                qpos = lax.broadcasted_iota(jnp.int32, s.shape, 1)
                s = jnp.where(kpos <= qpos, s, NEG)
            m, l = carry[u]
            m_new = jnp.maximum(m, jnp.max(s, axis=0, keepdims=True))
            alpha = jnp.exp(m - m_new)
            p = jnp.exp(s - m_new)
            l = alpha * l + jnp.sum(p, axis=0, keepdims=True)
            p = p.astype(BF16)
            upd = None
            for c in range(n_chunks):
                d = pv(u, j * n_chunks + c, p[c * VT_CHUNK:(c + 1) * VT_CHUNK, :])
                upd = d if upd is None else upd + d
            acc_ref[u] = alpha * acc_ref[u] + upd
            out.append((m_new, l))
        return tuple(out)

    if online:
        init = tuple((jnp.full((1, TQ), NEG, F32), jnp.zeros((1, TQ), F32)) for _ in range(2))
        carry = lax.fori_loop(0, i, lambda j, c: step_online(j, c, False), init)
        carry = step_online(i, carry, True)
        l0, l1 = carry[0][1], carry[1][1]
    else:
        l_ref[...] = jnp.zeros_like(l_ref)

        @pl.when(i == 0)
        def _():
            scores_to_probs(0, True, pd_ref)
            probs_times_values(0, pd_ref)

        @pl.when(i > 0)
        def _():
            scores_to_probs(0, False, p_ref)

            def body(j, _):
                probs_times_values(j - 1, p_ref)
                scores_to_probs(j, False, p_ref)
                return 0

            lax.fori_loop(1, i, body, 0)
            scores_to_probs(i, True, pd_ref)
            probs_times_values(i - 1, p_ref)
            probs_times_values(i, pd_ref)

        l0, l1 = (jnp.sum(l_ref[u], axis=0, keepdims=True) for u in range(2))

    o0 = acc_ref[0] * (1.0 / l0)
    o1 = acc_ref[1] * (1.0 / l1)
    o = o0 - lam_ref[0] * o1 if shared_v else jnp.concatenate([o0, o1], axis=0)
    o_ref[0] = o.T


def _attention(qt, kaug, vt, lam, shared_v, online):
    batch, _, seq = qt.shape
    n_steps = 4
    if shared_v:
        g0 = lambda h: h
        g1 = lambda h: H_A + h
    else:
        g0 = lambda h: 2 * h
        g1 = lambda h: 2 * h + 1
    dv = DV_A if shared_v else DH_B
    qspec = lambda g: pl.BlockSpec((1, LANES, TQ), lambda b, h, i: (b, g(h), i))
    kspec = lambda g: pl.BlockSpec((1, seq, LANES), lambda b, h, i: (b, 0, g(h)))
    return pl.pallas_call(
        functools.partial(_attn_kernel, shared_v, online),
        grid=(batch, n_steps, seq // TQ),
        in_specs=[pl.BlockSpec(memory_space=pltpu.SMEM), qspec(g0), qspec(g1), kspec(g0), kspec(g1),
                  pl.BlockSpec((1, seq // VT_CHUNK, LANES, VT_CHUNK), lambda b, h, i: (b, 0, h, 0))],
        out_specs=pl.BlockSpec((1, TQ, LANES), lambda b, h, i: (b, i, h)),
        out_shape=jax.ShapeDtypeStruct((batch, seq, 512), F32),
        scratch_shapes=[pltpu.VMEM((2, dv, TQ), F32), pltpu.VMEM((2, 8, TQ), F32),
                        pltpu.VMEM((2, TK, TQ), BF16), pltpu.VMEM((2, TK, TQ), BF16)],
        compiler_params=pltpu.CompilerParams(
            dimension_semantics=("arbitrary", "arbitrary", "arbitrary"), vmem_limit_bytes=VMEM_LIMIT),
        name=("attn_a" if shared_v else "attn_b") + ("_online" if online else ""),
    )(lam, qt, qt, kaug, kaug, vt)


def _decode_kernel(past, pps, pt_ref, lam_ref, qa_ref, qb_ref, *refs):
    del pt_ref
    kta_refs, va_refs, ktb_refs, vtb_refs, lft_refs = (refs[k * pps:(k + 1) * pps] for k in range(5))
    (kan_ref, van_ref, kbn_ref, vbn_ref, lfn_ref, upper_ref,
     oa_ref, ob_ref, qbd_ref, m_ref, l_ref, acca_ref, accb_ref, fc_ref) = refs[5 * pps:]
    step_i = pl.program_id(1)
    n_steps = pl.num_programs(1)
    rows = 32
    n_tok = 4
    new_pad = kan_ref.shape[1]
    page = LANES

    def row_ids(shape):
        r = lax.broadcasted_iota(jnp.int32, shape, 0)
        return ((r & 3, ((r >> 2) & 1) * H_A + (r >> 3), r >> 3),
                (r >> 3, r & 7, r & 7))

    def row_slope(head):
        return jnp.where(head == 0, SLOPES[0], jnp.where(head == 1, SLOPES[1],
               jnp.where(head == 2, SLOPES[2], SLOPES[3])))

    @pl.when(step_i == 0)
    def _():
        col_group = lax.broadcasted_iota(jnp.int32, (rows, 512), 1) >> 6
        for u, q_ref in enumerate((qa_ref, qb_ref)):
            tok, group, _ = row_ids((rows, 512))[u]
            qbd = jnp.zeros((rows, 512), F32)
            for t in range(n_tok):
                q_row = jnp.broadcast_to(q_ref[0, t:t + 1, :], (rows, 512))
                qbd = jnp.where((tok == t) & (group == col_group), q_row, qbd)
            qbd_ref[u] = qbd.astype(BF16)
        m_ref[...] = jnp.full(m_ref.shape, NEG, F32)
        l_ref[...] = jnp.zeros_like(l_ref)
        acca_ref[...] = jnp.zeros_like(acca_ref)
        accb_ref[...] = jnp.zeros_like(accb_ref)
        fc_ref[...] = jnp.zeros_like(fc_ref)

    def softmax_update(u, s):
        m_prev = m_ref[u][:, 0:1]
        l_prev = l_ref[u][:, 0:1]
        m_new = jnp.maximum(m_prev, jnp.max(s, axis=1, keepdims=True))
        alpha = jnp.exp(m_prev - m_new)
        pr = jnp.exp(s - m_new)
        l_new = alpha * l_prev + jnp.sum(pr, axis=1, keepdims=True)
        m_ref[u] = jnp.broadcast_to(m_new, (rows, LANES))
        l_ref[u] = jnp.broadcast_to(l_new, (rows, LANES))
        return alpha, pr

    def pv_a(pr, value_of_head):
        head = row_ids(pr.shape)[0][2]
        out = None
        for h in range(H_A):
            d = _dot(jnp.where(head == h, pr, 0.0).astype(BF16), value_of_head(h))
            out = d if out is None else out + d
        return out

    width = pps * page
    lane = lax.broadcasted_iota(jnp.int32, (rows, width), 1)
    (tok_a, _, head_a), _ = row_ids((rows, width))
    dist = (past + tok_a - (step_i * width + lane)).astype(F32)
    s_a = jnp.concatenate([_dot(qbd_ref[0], r[0].astype(BF16)) for r in kta_refs], axis=1)
    alpha, pr = softmax_update(0, s_a - row_slope(head_a) * dist)
    upd = None
    for j in range(pps):
        d = pv_a(pr[:, j * page:(j + 1) * page],
                 lambda h: va_refs[j][0, pl.ds(h, page, stride=H_A), :].astype(BF16))
        upd = d if upd is None else upd + d
    acca_ref[...] = alpha * acca_ref[...] + upd

    parts = []
    for j in range(pps):
        parts.extend(v.astype(F32) for v in _split3(lft_refs[j][0]))
        parts.append(jnp.zeros((8, page), F32))
    cs = _dot(jnp.concatenate(parts, axis=0).astype(BF16), upper_ref[...])
    f_carry = fc_ref[...][:, 0:1]
    f_pages = []
    for j in range(pps):
        local = cs[32 * j:32 * j + 8] + cs[32 * j + 8:32 * j + 16] + cs[32 * j + 16:32 * j + 24]
        f_page = local + f_carry
        f_carry = f_carry + local[:, page - 1:page]
        f_pages.append(jnp.concatenate([f_page] * n_tok, axis=0))
    fc_ref[...] = jnp.broadcast_to(f_carry, (8, LANES))
    s_b = jnp.concatenate([_dot(qbd_ref[1], r[0].astype(BF16)) for r in ktb_refs], axis=1)
    alpha, pr = softmax_update(1, s_b - jnp.concatenate(f_pages, axis=1))
    upd = None
    for j in range(pps):
        d = _dot_nt(pr[:, j * page:(j + 1) * page].astype(BF16), vtb_refs[j][0].astype(BF16))
        upd = d if upd is None else upd + d
    accb_ref[...] = alpha * accb_ref[...] + upd

    @pl.when(step_i == n_steps - 1)
    def _():
        tn = lax.broadcasted_iota(jnp.int32, (rows, new_pad), 1)
        (tok_a, _, head_a), (tok_b, _, _) = row_ids((rows, new_pad))
        s_an = _dot_nt(qbd_ref[0], kan_ref[0].astype(BF16)) - row_slope(head_a) * (tok_a - tn).astype(F32)
        alpha, pr = softmax_update(0, jnp.where((tn <= tok_a) & (tn < n_tok), s_an, NEG))
        van = van_ref[0].astype(BF16)
        acca_ref[...] = alpha * acca_ref[...] + pv_a(pr, lambda h: van[:, h * DV_A:(h + 1) * DV_A])

        lfn = lfn_ref[0]
        t8 = lax.broadcasted_iota(jnp.int32, (8, new_pad), 1)
        f_new = jnp.broadcast_to(f_carry, (8, new_pad))
        for t in range(n_tok):
            f_new = f_new + jnp.where(t8 >= t, jnp.broadcast_to(lfn[:, t:t + 1], (8, new_pad)), 0.0)
        s_bn = _dot_nt(qbd_ref[1], kbn_ref[0].astype(BF16)) - jnp.concatenate([f_new] * n_tok, axis=0)
        alpha, pr = softmax_update(1, jnp.where((tn <= tok_b) & (tn < n_tok), s_bn, NEG))
        accb_ref[...] = alpha * accb_ref[...] + _dot(pr.astype(BF16), vbn_ref[0].astype(BF16))

        on_a = acca_ref[...] * (1.0 / l_ref[0][:, 0:1])
        for h in range(H_A):
            oa_ref[0, :, h * DV_A:(h + 1) * DV_A] = (on_a[8 * h:8 * h + n_tok]
                                                     - lam_ref[0] * on_a[8 * h + n_tok:8 * h + 2 * n_tok])
        head_b = row_ids((rows, 512))[1][2]
        col_head = lax.broadcasted_iota(jnp.int32, (rows, 512), 1) >> 6
        on_b = jnp.where(head_b == col_head, accb_ref[...] * (1.0 / l_ref[1][:, 0:1]), 0.0)
        for t in range(n_tok):
            ob_ref[0, t:t + 1, :] = jnp.sum(on_b[8 * t:8 * t + 8], axis=0, keepdims=True)


def _decode(page_table, lam, qa, qb, pools, news, lfn_t, upper):
    n_seq, n_pages = page_table.shape
    page = pools[0].shape[2]
    pps = PAGES_PER_STEP
    new_pad = news[0].shape[1]
    past = n_pages * page
    pool_spec = lambda r, j: pl.BlockSpec((1, r, page), lambda n, s, pt: (pt[n * n_pages + s * pps + j], 0, 0))
    seq_spec = lambda r: pl.BlockSpec((1, r, 512), lambda n, s, pt: (n, 0, 0))
    pool_specs, pool_args = [], []
    for pool in pools:
        for j in range(pps):
            pool_specs.append(pool_spec(pool.shape[1], j))
            pool_args.append(pool)
    grid_spec = pltpu.PrefetchScalarGridSpec(
        num_scalar_prefetch=1,
        grid=(n_seq, n_pages // pps),
        in_specs=[pl.BlockSpec(memory_space=pltpu.SMEM), seq_spec(4), seq_spec(4)] + pool_specs
                 + [seq_spec(new_pad)] * 4
                 + [pl.BlockSpec((1, H_B, new_pad), lambda n, s, pt: (n, 0, 0)),
                    pl.BlockSpec((page, page), lambda n, s, pt: (0, 0))],
        out_specs=[seq_spec(4), seq_spec(4)],
        scratch_shapes=[pltpu.VMEM((2, 32, 512), BF16), pltpu.VMEM((2, 32, LANES), F32),
                        pltpu.VMEM((2, 32, LANES), F32), pltpu.VMEM((32, DV_A), F32),
                        pltpu.VMEM((32, 512), F32), pltpu.VMEM((H_B, LANES), F32)],
    )
    return pl.pallas_call(
        functools.partial(_decode_kernel, past, pps),
        grid_spec=grid_spec,
        out_shape=[jax.ShapeDtypeStruct((n_seq, 4, 512), F32)] * 2,
        compiler_params=pltpu.CompilerParams(dimension_semantics=("arbitrary", "arbitrary"),
                                             vmem_limit_bytes=VMEM_LIMIT),
        name="decode",
    )(page_table.reshape(-1), lam, qa, qb, *pool_args, *news, lfn_t, upper)


def _merge_kernel(x_ref, oa_ref, ob_ref, sga_ref, sgb_ref, sma_ref, smb_ref, gres_ref, gsub_ref,
                  woa_ref, wob_ref, wo_ref, y_ref):
    oa = oa_ref[...]
    heads = []
    for h in range(H_A):
        oh = oa[:, h * DV_A:(h + 1) * DV_A]
        heads.append(oh * lax.rsqrt(jnp.mean(oh * oh, axis=-1, keepdims=True) + EPS))
    oa = jnp.concatenate(heads, axis=1) * gsub_ref[...]
    ya = _dot((oa * sga_ref[...].astype(F32)).astype(BF16), woa_ref[...])
    yb = _dot((ob_ref[...] * sgb_ref[...].astype(F32)).astype(BF16), wob_ref[...])
    mix = sma_ref[...].astype(F32) * ya + smb_ref[...].astype(F32) * yb
    y = _dot(mix.astype(BF16), wo_ref[...])
    y_ref[...] = x_ref[...] + gres_ref[...] * y


def _merge(x2, oa, ob, sga, sgb, sma, smb, gres, gres_spec, wts):
    rows = x2.shape[0]
    row = lambda w: pl.BlockSpec((TM, w), lambda i: (i, 0))
    consts = [wts["gsub"], wts["woa"], wts["wob"], wts["wo"]]
    return pl.pallas_call(
        _merge_kernel,
        grid=(rows // TM,),
        in_specs=[row(D_MODEL), row(512), row(512), row(512), row(512), row(1024), row(1024), gres_spec]
                 + [_const_spec(c.shape) for c in consts],
        out_specs=row(D_MODEL),
        out_shape=jax.ShapeDtypeStruct((rows, D_MODEL), F32),
        compiler_params=pltpu.CompilerParams(dimension_semantics=("arbitrary",),
                                             vmem_limit_bytes=VMEM_LIMIT),
        name="merge",
    )(x2, oa, ob, sga, sgb, sma, smb, gres, *consts)


def _placement_constants():
    gsum = np.kron(np.eye(N_GROUPS), np.ones((DH_A, DH_A)))
    place = np.zeros((512, 1024))
    for g in range(N_GROUPS):
        place[g * DH_A + np.arange(DH_A), g * LANES + np.arange(DH_A)] = 1.0
    sel = np.zeros((3 * LANES, 1024))
    constb = np.zeros((1, 1024))
    for g in range(N_GROUPS):
        for part in range(3):
            sel[part * LANES + g, g * LANES + AUG_COL + 3 + part] = 1.0
        constb[0, g * LANES + AUG_COL + np.array([0, 1, 2, 6])] = 1.0
    tri = np.tril(np.ones((TM, TM)))
    upper = np.triu(np.ones((LANES, LANES)))
    bf = lambda a: jnp.asarray(a, BF16)
    return dict(gsum=bf(gsum), place=bf(place), self=bf(sel), constb=jnp.asarray(constb, F32),
                tri=bf(tri), upper=bf(upper))


def _bf16_ceil(x):
    y = x.astype(BF16).astype(F32)
    return jnp.where(y < x, y * (1.0 + 2.0 ** -7), y)


def kernel(x_prompt, x_sample, cache_a_k, cache_a_v, cache_b_k, cache_b_v, cache_b_logf, page_table,
           c_prompt, c_sample, w_ada, b_ada, w_in, b_f, g_q_a, g_k_a, g_q_b, g_k_b,
           lambda_q1, lambda_k1, lambda_q2, lambda_k2, g_sub_a, w_out_a, w_out_b, w_o):
    assert w_ada.shape[0] == 1, "single-layer step"
    batch, seq, _ = x_prompt.shape
    n_seq, n_tok, _ = x_sample.shape
    n_pool, page = cache_a_k.shape[1], cache_a_k.shape[2]
    layer = 0
    lam_init = 0.8 - 0.6 * math.exp(-0.3 * layer)
    lam = (jnp.exp(jnp.sum(lambda_q1[layer] * lambda_k1[layer]))
           - jnp.exp(jnp.sum(lambda_q2[layer] * lambda_k2[layer])) + lam_init).reshape(1).astype(F32)

    w = w_in[layer]
    sec = np.cumsum((0, 512, 512, 512, 512, 512, 512, 512, H_B, 512, 1024, 1024))
    cols = lambda k: w[:, sec[k]:sec[k + 1]]
    wq = jnp.concatenate([cols(0), cols(4)], axis=1).astype(BF16)
    wn = jnp.concatenate([cols(1), cols(5), cols(2), cols(6), cols(3), cols(8), cols(9), cols(10)],
                         axis=1).astype(BF16)
    wf = jnp.pad(cols(7), ((0, 0), (0, LANES - H_B))).astype(BF16)
    gq = jnp.concatenate([jnp.tile(g_q_a[layer], N_GROUPS) * DH_A ** -0.5,
                          jnp.tile(g_q_b[layer], N_GROUPS) * DH_B ** -0.5])
    gk = jnp.concatenate([jnp.tile(g_k_a[layer], N_GROUPS), jnp.tile(g_k_b[layer], N_GROUPS)])
    wts = dict(_placement_constants(),
               wq=wq, wqt=wq.T, wn=wn, wf=wf, bf=jnp.pad(b_f[layer], (0, LANES - H_B)).reshape(1, LANES),
               gq_col=gq.reshape(-1, 1), gq_row=gq.reshape(1, -1), gk_row=gk.reshape(1, -1),
               gsub=(jnp.tile(g_sub_a[layer], H_A) * (1.0 - lam_init)).reshape(1, W_A),
               woa=w_out_a[layer].astype(BF16), wob=w_out_b[layer].astype(BF16), wo=w_o[layer].astype(BF16))
    bound = lambda gq_, gk_: _bf16_ceil(8.1 * jnp.max(jnp.abs(gq_ * gk_)))
    mshift = jnp.stack([bound(g_q_a[layer], g_k_a[layer]), bound(g_q_b[layer], g_k_b[layer])]).astype(F32)

    n_c = batch + n_seq
    c_all = jnp.pad(jnp.concatenate([c_prompt, c_sample], axis=0), ((0, -n_c % 8), (0, 0)))
    mod = _ada(c_all, w_ada[layer], b_ada[layer])
    shift, scale, gres = (mod[:, k * D_MODEL:(k + 1) * D_MODEL] for k in range(3))

    xp2 = x_prompt.reshape(batch * seq, D_MODEL)
    p3 = lambda a: a[:batch].reshape(batch, 1, D_MODEL)
    (ka, va, kb, vb, lf, qta, kaa, vta, qtb, kab, vtb, sga, sgb, sma, smb) = _proj_prompt(
        xp2, p3(scale), p3(shift), mshift, wts, batch, seq)
    online = jnp.max(mshift) > FAST_PATH_MAX_BOUND
    attend = lambda *a: lax.cond(online, lambda: _attention(*a, True), lambda: _attention(*a, False))
    oa = attend(qta, kaa, vta, lam, True).reshape(batch * seq, W_A)
    ob = attend(qtb, kab, vtb, lam, False).reshape(batch * seq, W_B)
    nb = seq // TM
    gres_p_spec = pl.BlockSpec((None, 1, D_MODEL), lambda i: (i // nb, 0, 0))
    yp = _merge(xp2, oa, ob, sga, sgb, sma, smb, p3(gres), gres_p_spec, wts).reshape(batch, seq, D_MODEL)

    xs2 = x_sample.reshape(n_seq * n_tok, D_MODEL)
    rep = lambda a: jnp.repeat(a[batch:n_c], n_tok, axis=0)
    (qa_s, qb_s, ka_s, va_s, kb_s, vb_s, lf_s, sga_s, sgb_s, sma_s, smb_s) = _proj_sample(
        xs2, rep(scale), rep(shift), wts)
    new_pad = 16
    seq3 = lambda a: a.reshape(n_seq, n_tok, 512)
    padn = lambda a: jnp.pad(seq3(a), ((0, 0), (0, new_pad - n_tok), (0, 0)))
    pools = [jnp.transpose(cache_a_k[layer], (0, 2, 3, 4, 1)).reshape(n_pool, 512, page),
             cache_a_v[layer].reshape(n_pool, page * H_A, DV_A),
             jnp.transpose(cache_b_k[layer], (0, 2, 3, 1)).reshape(n_pool, 512, page),
             jnp.transpose(cache_b_v[layer], (0, 2, 3, 1)).reshape(n_pool, 512, page),
             jnp.swapaxes(cache_b_logf[layer], 1, 2)]
    lfn_t = jnp.pad(jnp.swapaxes(lf_s.reshape(n_seq, n_tok, H_B), 1, 2), ((0, 0), (0, 0), (0, new_pad - n_tok)))
    oa_s, ob_s = _decode(page_table, lam, seq3(qa_s), seq3(qb_s), pools,
                         [padn(ka_s), padn(va_s), padn(kb_s), padn(vb_s)], lfn_t, wts["upper"])
    gres_s_spec = pl.BlockSpec((TM, D_MODEL), lambda i: (i, 0))
    ys = _merge(xs2, oa_s.reshape(-1, W_A), ob_s.reshape(-1, W_B), sga_s, sgb_s, sma_s, smb_s,
                rep(gres), gres_s_spec, wts).reshape(n_seq, n_tok, D_MODEL)

    return (yp, ys,
            ka.reshape(1, batch, seq, 2, H_A, DH_A), va.reshape(1, batch, seq, H_A, DV_A),
            kb.reshape(1, batch, seq, H_B, DH_B), vb.reshape(1, batch, seq, H_B, DH_B),
            lf.reshape(1, batch, seq, H_B),
            ka_s.reshape(1, n_seq, n_tok, 2, H_A, DH_A), va_s.reshape(1, n_seq, n_tok, H_A, DV_A),
            kb_s.reshape(1, n_seq, n_tok, H_B, DH_B), vb_s.reshape(1, n_seq, n_tok, H_B, DH_B),
            lf_s.reshape(1, n_seq, n_tok, H_B))
```

```python
import functools
import math

import numpy as np
import jax
import jax.numpy as jnp
from jax import lax
from jax.experimental import pallas as pl
from jax.experimental.pallas import tpu as pltpu

F32 = jnp.float32
BF16 = jnp.bfloat16

D_MODEL = 1024
H_A, DH_A, DV_A = 4, 64, 128
H_B, DH_B = 8, 64
W_A = H_A * DV_A
W_B = H_B * DH_B
N_GROUPS = 8
EPS = 1e-6
NEG = -1e30
SLOPES = tuple(2.0 ** (-8.0 * (h + 1) / H_A) for h in range(H_A))
LANES = 128

TM = 256
TQ = 512
TK = 512
VT_CHUNK = TM
PAGES_PER_STEP = 8
FAST_PATH_MAX_BOUND = 30.0
VMEM_LIMIT = 56 * 1024 * 1024


def _dot(a, b):
    return jnp.dot(a, b, preferred_element_type=F32)


def _dot_nt(a, b):
    return lax.dot_general(a, b, (((1,), (1,)), ((), ())), preferred_element_type=F32)


def _split3(x):
    hi = x.astype(BF16)
    r1 = x - hi.astype(F32)
    mid = r1.astype(BF16)
    lo = (r1 - mid.astype(F32)).astype(BF16)
    return hi, mid, lo


def _silu(x):
    return x * jax.nn.sigmoid(x)


def _const_spec(shape):
    nd = len(shape)
    return pl.BlockSpec(shape, lambda *_: (0,) * nd, pipeline_mode=pl.Buffered(1))


def _ada_kernel(c_ref, w_ref, b_ref, o_ref):
    c = c_ref[...]
    o_ref[...] = _dot(_silu(c).astype(BF16), w_ref[...].astype(BF16)) + b_ref[...]


def _ada(c_all, w_ada, b_ada):
    rows = c_all.shape[0]
    n = w_ada.shape[1]
    bn = 1024
    return pl.pallas_call(
        _ada_kernel,
        grid=(n // bn,),
        in_specs=[pl.BlockSpec((rows, D_MODEL), lambda j: (0, 0)),
                  pl.BlockSpec((D_MODEL, bn), lambda j: (0, j)),
                  pl.BlockSpec((1, bn), lambda j: (0, j))],
        out_specs=pl.BlockSpec((rows, bn), lambda j: (0, j)),
        out_shape=jax.ShapeDtypeStruct((rows, n), F32),
        name="ada",
    )(c_all, w_ada, b_ada.reshape(1, n))


def _hidden(x_ref, scale_ref, shift_ref):
    x = x_ref[...]
    ms = jnp.mean(x * x, axis=-1, keepdims=True)
    h = x * lax.rsqrt(ms + EPS) * (1.0 + scale_ref[...]) + shift_ref[...]
    return h.astype(BF16)


def _group_rms(z, g_ref):
    zz = z * z
    hi = zz.astype(BF16)
    lo = (zz - hi.astype(F32)).astype(BF16)
    ss = _dot(hi, g_ref[...]) + _dot(lo, g_ref[...])
    return z * lax.rsqrt(ss * (1.0 / DH_A) + EPS)


def _log_sigmoid(z):
    return jnp.minimum(z, 0.0) - jnp.log1p(jnp.exp(-jnp.abs(z)))


def _logf(hb, wf_ref, bf_ref):
    z = _dot(hb, wf_ref[...]) + bf_ref[...]
    lane = lax.broadcasted_iota(jnp.int32, z.shape, 1)
    return jnp.where(lane < H_B, _log_sigmoid(z), 0.0)


def _proj_prompt_kernel(nb, mshift_ref, x_ref, scale_ref, shift_ref, wqkt_ref, wn_ref, wf_ref, bf_ref,
                        gqk_ref, tri_ref,
                        kta_ref, va_ref, ktb_ref, vtbo_ref, lft_ref,
                        qta_ref, kaa_ref, vta_ref, qtb_ref, kab_ref, vtb_ref,
                        sga_ref, sgb_ref, sma_ref, smb_ref, carry_ref):
    tm = x_ref.shape[0]
    ib = pl.program_id(0) % nb
    hb = _hidden(x_ref, scale_ref, shift_ref)

    lf = _logf(hb, wf_ref, bf_ref)
    lft_ref[0] = lf.T[0:H_B, :]
    hi, mid, lo = _split3(lf)
    tri = tri_ref[...]
    f_local = _dot(tri, hi) + _dot(tri, mid) + _dot(tri, lo)

    @pl.when(ib == 0)
    def _():
        carry_ref[...] = jnp.zeros_like(carry_ref)

    f_nat = f_local + carry_ref[...]
    carry_ref[...] = f_nat[tm - 1:tm, :]
    f_t = f_nat.T[0:H_B, :]

    n_g = 4 * N_GROUPS
    n_qk = n_g * DH_A
    qkt = jnp.concatenate([_dot_nt(wqkt_ref[0:n_qk // 2, :], hb),
                           _dot_nt(wqkt_ref[n_qk // 2:n_qk, :], hb)], axis=0)
    g3 = qkt.reshape(n_g, DH_A, tm)
    ssq = jnp.sum(g3 * g3, axis=1, keepdims=True)
    g3 = (qkt * gqk_ref[...]).reshape(n_g, DH_A, tm) * lax.rsqrt(ssq * (1.0 / DH_A) + EPS)
    kta_ref[0] = g3[2 * N_GROUPS:3 * N_GROUPS].reshape(N_GROUPS * DH_A, tm)
    ktb_ref[0] = g3[3 * N_GROUPS:4 * N_GROUPS].reshape(N_GROUPS * DH_A, tm)

    r8 = lax.broadcasted_iota(jnp.int32, (8, tm), 0)
    pos = ib * tm + lax.broadcasted_iota(jnp.int32, (8, tm), 1)
    pos_hi = (pos >> 7).astype(F32)
    pos_lo = (pos & 127).astype(F32)
    zeros_tail = jnp.zeros((LANES - DH_A - 8, tm), F32)
    m_a = mshift_ref[0]
    m_b = mshift_ref[1]

    def operand(group, aug):
        return jnp.concatenate([g3[group], aug, zeros_tail], axis=0)

    k_aug_a = jnp.where(r8 == 2, pos_hi, jnp.where(r8 == 3, pos_lo,
              jnp.where((r8 <= 1) | (r8 == 4), 1.0, 0.0)))
    for g in range(N_GROUPS):
        slope = SLOPES[g % H_A]
        q_aug = jnp.where(r8 == 0, -slope * 128.0 * pos_hi,
                jnp.where(r8 == 1, -slope * pos_lo,
                jnp.where(r8 == 2, slope * 128.0,
                jnp.where(r8 == 3, slope,
                jnp.where(r8 == 4, -m_a, 0.0)))))
        qta_ref[0, g * LANES:(g + 1) * LANES, :] = operand(g, q_aug).astype(BF16)
        kaa_ref[0, :, g * LANES:(g + 1) * LANES] = operand(2 * N_GROUPS + g, k_aug_a).T.astype(BF16)
    ft_hi, ft_mid, ft_lo = _split3(f_t)
    for g in range(N_GROUPS):
        bc = lambda v: jnp.broadcast_to(v[g:g + 1, :].astype(F32), (8, tm))
        q_aug = jnp.where(r8 == 0, bc(ft_hi),
                jnp.where(r8 == 1, bc(ft_mid),
                jnp.where(r8 == 2, bc(ft_lo),
                jnp.where(r8 <= 5, -1.0,
                jnp.where(r8 == 6, -m_b, 0.0)))))
        k_aug = jnp.where(r8 <= 2, 1.0,
                jnp.where(r8 == 3, bc(ft_hi),
                jnp.where(r8 == 4, bc(ft_mid),
                jnp.where(r8 == 5, bc(ft_lo),
                jnp.where(r8 == 6, 1.0, 0.0)))))
        qtb_ref[0, g * LANES:(g + 1) * LANES, :] = operand(N_GROUPS + g, q_aug).astype(BF16)
        kab_ref[0, :, g * LANES:(g + 1) * LANES] = operand(3 * N_GROUPS + g, k_aug).T.astype(BF16)

    va = _dot(hb, wn_ref[:, 0:512])
    va_ref[...] = va
    vta_ref[0, 0] = va.T.astype(BF16)
    vbt = _dot_nt(wqkt_ref[n_qk:, :], hb)
    vtbo_ref[0] = vbt
    vtb_ref[0, 0] = vbt.astype(BF16)

    sga_ref[...] = _silu(_dot(hb, wn_ref[:, 512:1024])).astype(BF16)
    sgb_ref[...] = _silu(_dot(hb, wn_ref[:, 1024:1536])).astype(BF16)
    sma_ref[...] = jax.nn.sigmoid(_dot(hb, wn_ref[:, 1536:2560])).astype(BF16)
    smb_ref[...] = jax.nn.sigmoid(_dot(hb, wn_ref[:, 2560:3584])).astype(BF16)


def _proj_prompt(x2, scale, shift, mshift, wts, batch, seq):
    rows = batch * seq
    nb = seq // TM
    nk = seq // VT_CHUNK
    row = lambda w: pl.BlockSpec((TM, w), lambda i: (i, 0))
    mod = pl.BlockSpec((None, 1, D_MODEL), lambda i: (i // nb, 0, 0))
    consts = [wts["wqkt"], wts["wn_prompt"], wts["wf"], wts["bf"], wts["gqk_col"], wts["tri"]]
    seq_t = lambda r: jax.ShapeDtypeStruct((batch, r, seq), F32)
    out_shape = [seq_t(512), jax.ShapeDtypeStruct((rows, 512), F32), seq_t(512), seq_t(512), seq_t(H_B)] + [
        jax.ShapeDtypeStruct((batch, 1024, seq), BF16), jax.ShapeDtypeStruct((batch, seq, 1024), BF16),
        jax.ShapeDtypeStruct((batch, nk, W_A, VT_CHUNK), BF16),
        jax.ShapeDtypeStruct((batch, 1024, seq), BF16), jax.ShapeDtypeStruct((batch, seq, 1024), BF16),
        jax.ShapeDtypeStruct((batch, nk, W_B, VT_CHUNK), BF16),
        jax.ShapeDtypeStruct((rows, 512), BF16), jax.ShapeDtypeStruct((rows, 512), BF16),
        jax.ShapeDtypeStruct((rows, 1024), BF16), jax.ShapeDtypeStruct((rows, 1024), BF16)]
    qt_spec = pl.BlockSpec((1, 1024, TM), lambda i: (i // nb, 0, i % nb))
    k_spec = pl.BlockSpec((1, TM, 1024), lambda i: (i // nb, i % nb, 0))
    vt_spec = pl.BlockSpec((1, TM // VT_CHUNK, 512, VT_CHUNK), lambda i: (i // nb, i % nb, 0, 0))
    out_t = lambda r: pl.BlockSpec((1, r, TM), lambda i: (i // nb, 0, i % nb))
    out_specs = [out_t(512), row(512), out_t(512), out_t(512), out_t(H_B)] + [
        qt_spec, k_spec, vt_spec, qt_spec, k_spec, vt_spec, row(512), row(512), row(1024), row(1024)]
    return pl.pallas_call(
        functools.partial(_proj_prompt_kernel, nb),
        grid=(rows // TM,),
        in_specs=[pl.BlockSpec(memory_space=pltpu.SMEM), row(D_MODEL), mod, mod]
                 + [_const_spec(c.shape) for c in consts],
        out_specs=out_specs,
        out_shape=out_shape,
        scratch_shapes=[pltpu.VMEM((1, LANES), F32)],
        compiler_params=pltpu.CompilerParams(dimension_semantics=("arbitrary",),
                                             vmem_limit_bytes=VMEM_LIMIT),
        name="proj_prompt",
    )(mshift, x2, scale, shift, *consts)


def _proj_sample_kernel(x_ref, scale_ref, shift_ref, wq_ref, wn_ref, wf_ref, bf_ref, gq_ref, gk_ref, g_ref,
                        qa_ref, qb_ref, ka_ref, va_ref, kb_ref, vb_ref, lf_ref,
                        sga_ref, sgb_ref, sma_ref, smb_ref):
    hb = _hidden(x_ref, scale_ref, shift_ref)
    lf_ref[...] = _logf(hb, wf_ref, bf_ref)[:, :H_B]
    qa_ref[...] = _group_rms(_dot(hb, wq_ref[:, 0:512]), g_ref) * gq_ref[:, 0:512]
    qb_ref[...] = _group_rms(_dot(hb, wq_ref[:, 512:1024]), g_ref) * gq_ref[:, 512:1024]
    ka_ref[...] = _group_rms(_dot(hb, wn_ref[:, 0:512]), g_ref) * gk_ref[:, 0:512]
    kb_ref[...] = _group_rms(_dot(hb, wn_ref[:, 512:1024]), g_ref) * gk_ref[:, 512:1024]
    va_ref[...] = _dot(hb, wn_ref[:, 1024:1536])
    vb_ref[...] = _dot(hb, wn_ref[:, 1536:2048])
    sga_ref[...] = _silu(_dot(hb, wn_ref[:, 2048:2560])).astype(BF16)
    sgb_ref[...] = _silu(_dot(hb, wn_ref[:, 2560:3072])).astype(BF16)
    sma_ref[...] = jax.nn.sigmoid(_dot(hb, wn_ref[:, 3072:4096])).astype(BF16)
    smb_ref[...] = jax.nn.sigmoid(_dot(hb, wn_ref[:, 4096:5120])).astype(BF16)


def _proj_sample(x2, scale_rows, shift_rows, wts):
    rows = x2.shape[0]
    row = lambda w: pl.BlockSpec((TM, w), lambda i: (i, 0))
    consts = [wts["wq"], wts["wn"], wts["wf"], wts["bf"], wts["gq_row"], wts["gk_row"], wts["gsum"]]
    out_shape = [jax.ShapeDtypeStruct((rows, 512), F32)] * 6 + [jax.ShapeDtypeStruct((rows, H_B), F32)] + [
        jax.ShapeDtypeStruct((rows, 512), BF16), jax.ShapeDtypeStruct((rows, 512), BF16),
        jax.ShapeDtypeStruct((rows, 1024), BF16), jax.ShapeDtypeStruct((rows, 1024), BF16)]
    out_specs = [row(512)] * 6 + [row(H_B), row(512), row(512), row(1024), row(1024)]
    return pl.pallas_call(
        _proj_sample_kernel,
        grid=(rows // TM,),
        in_specs=[row(D_MODEL), row(D_MODEL), row(D_MODEL)] + [_const_spec(c.shape) for c in consts],
        out_specs=out_specs,
        out_shape=out_shape,
        compiler_params=pltpu.CompilerParams(dimension_semantics=("arbitrary",),
                                             vmem_limit_bytes=VMEM_LIMIT),
        name="proj_sample",
    )(x2, scale_rows, shift_rows, *consts)


def _attn_kernel(shared_v, online, lam_ref, qt0_ref, qt1_ref, k0_ref, k1_ref, vt_ref, o_ref,
                 acc_ref, l_ref, p_ref, pd_ref):
    i = pl.program_id(2)
    qts = (qt0_ref[0], qt1_ref[0])
    k_refs = (k0_ref, k1_ref)
    dv = acc_ref.shape[1]
    n_chunks = TK // VT_CHUNK
    acc_ref[...] = jnp.zeros_like(acc_ref)

    def pv(u, chunk, p):
        blk = vt_ref[0, chunk]
        v = blk if shared_v else blk[u * dv:(u + 1) * dv, :]
        return _dot(v, p)

    def scores_to_probs(j, masked, dst_ref):
        for u in range(2):
            kblk = k_refs[u][0, pl.ds(pl.multiple_of(j * TK, TK), TK), :]
            s = _dot(kblk, qts[u])
            if masked:
                kpos = lax.broadcasted_iota(jnp.int32, s.shape, 0)
                qpos = lax.broadcasted_iota(jnp.int32, s.shape, 1)
                s = jnp.where(kpos <= qpos, s, NEG)
            p = jnp.exp(s)
            l_ref[u] += jnp.sum(p.reshape(TK // 8, 8, TQ), axis=0)
            dst_ref[u] = p.astype(BF16)

    def probs_times_values(j, src_ref):
        for u in range(2):
            upd = None
            for c in range(n_chunks):
                d = pv(u, j * n_chunks + c, src_ref[u, c * VT_CHUNK:(c + 1) * VT_CHUNK, :])
                upd = d if upd is None else upd + d
            acc_ref[u] += upd

    def step_online(j, carry, masked):
        out = []
        for u in range(2):
            kblk = k_refs[u][0, pl.ds(pl.multiple_of(j * TK, TK), TK), :]
            s = _dot(kblk, qts[u])
            if masked:
                kpos = lax.broadcasted_iota(jnp.int32, s.shape, 0)
                qpos = lax.broadcasted_iota(jnp.int32, s.shape, 1)
                s = jnp.where(kpos <= qpos, s, NEG)
            m, l = carry[u]
            m_new = jnp.maximum(m, jnp.max(s, axis=0, keepdims=True))
            alpha = jnp.exp(m - m_new)
            p = jnp.exp(s - m_new)
            l = alpha * l + jnp.sum(p, axis=0, keepdims=True)
            p = p.astype(BF16)
            upd = None
            for c in range(n_chunks):
                d = pv(u, j * n_chunks + c, p[c * VT_CHUNK:(c + 1) * VT_CHUNK, :])
                upd = d if upd is None else upd + d
            acc_ref[u] = alpha * acc_ref[u] + upd
            out.append((m_new, l))
        return tuple(out)

    if online:
        init = tuple((jnp.full((1, TQ), NEG, F32), jnp.zeros((1, TQ), F32)) for _ in range(2))
        carry = lax.fori_loop(0, i, lambda j, c: step_online(j, c, False), init)
        carry = step_online(i, carry, True)
        l0, l1 = carry[0][1], carry[1][1]
    else:
        l_ref[...] = jnp.zeros_like(l_ref)

        @pl.when(i == 0)
        def _():
            scores_to_probs(0, True, pd_ref)
            probs_times_values(0, pd_ref)

        @pl.when(i > 0)
        def _():
            scores_to_probs(0, False, p_ref)

            def body(j, _):
                probs_times_values(j - 1, p_ref)
                scores_to_probs(j, False, p_ref)
                return 0

            lax.fori_loop(1, i, body, 0)
            scores_to_probs(i, True, pd_ref)
            probs_times_values(i - 1, p_ref)
            probs_times_values(i, pd_ref)

        l0, l1 = (jnp.sum(l_ref[u], axis=0, keepdims=True) for u in range(2))

    o0 = acc_ref[0] * (1.0 / l0)
    o1 = acc_ref[1] * (1.0 / l1)
    o = o0 - lam_ref[0] * o1 if shared_v else jnp.concatenate([o0, o1], axis=0)
    o_ref[0] = o.T


def _attention(qt, kaug, vt, lam, shared_v, online):
    batch, _, seq = qt.shape
    n_steps = 4
    if shared_v:
        g0 = lambda h: h
        g1 = lambda h: H_A + h
    else:
        g0 = lambda h: 2 * h
        g1 = lambda h: 2 * h + 1
    dv = DV_A if shared_v else DH_B
    qspec = lambda g: pl.BlockSpec((1, LANES, TQ), lambda b, h, i: (b, g(h), i))
    kspec = lambda g: pl.BlockSpec((1, seq, LANES), lambda b, h, i: (b, 0, g(h)))
    return pl.pallas_call(
        functools.partial(_attn_kernel, shared_v, online),
        grid=(batch, n_steps, seq // TQ),
        in_specs=[pl.BlockSpec(memory_space=pltpu.SMEM), qspec(g0), qspec(g1), kspec(g0), kspec(g1),
                  pl.BlockSpec((1, seq // VT_CHUNK, LANES, VT_CHUNK), lambda b, h, i: (b, 0, h, 0))],
        out_specs=pl.BlockSpec((1, TQ, LANES), lambda b, h, i: (b, i, h)),
        out_shape=jax.ShapeDtypeStruct((batch, seq, 512), F32),
        scratch_shapes=[pltpu.VMEM((2, dv, TQ), F32), pltpu.VMEM((2, 8, TQ), F32),
                        pltpu.VMEM((2, TK, TQ), BF16), pltpu.VMEM((2, TK, TQ), BF16)],
        compiler_params=pltpu.CompilerParams(
            dimension_semantics=("arbitrary", "arbitrary", "arbitrary"), vmem_limit_bytes=VMEM_LIMIT),
        name=("attn_a" if shared_v else "attn_b") + ("_online" if online else ""),
    )(lam, qt, qt, kaug, kaug, vt)


def _decode_kernel(past, pps, pt_ref, lam_ref, qa_ref, qb_ref, *refs):
    del pt_ref
    kta_refs, va_refs, ktb_refs, vtb_refs, lft_refs = (refs[k * pps:(k + 1) * pps] for k in range(5))
    (kan_ref, van_ref, kbn_ref, vbn_ref, lfn_ref, upper_ref,
     oa_ref, ob_ref, qbd_ref, m_ref, l_ref, acca_ref, accb_ref, fc_ref) = refs[5 * pps:]
    step_i = pl.program_id(1)
    n_steps = pl.num_programs(1)
    rows = 32
    n_tok = 4
    new_pad = kan_ref.shape[1]
    page = LANES

    def row_ids(shape):
        r = lax.broadcasted_iota(jnp.int32, shape, 0)
        return ((r & 3, ((r >> 2) & 1) * H_A + (r >> 3), r >> 3),
                (r >> 3, r & 7, r & 7))

    def row_slope(head):
        return jnp.where(head == 0, SLOPES[0], jnp.where(head == 1, SLOPES[1],
               jnp.where(head == 2, SLOPES[2], SLOPES[3])))

    @pl.when(step_i == 0)
    def _():
        col_group = lax.broadcasted_iota(jnp.int32, (rows, 512), 1) >> 6
        for u, q_ref in enumerate((qa_ref, qb_ref)):
            tok, group, _ = row_ids((rows, 512))[u]
            qbd = jnp.zeros((rows, 512), F32)
            for t in range(n_tok):
                q_row = jnp.broadcast_to(q_ref[0, t:t + 1, :], (rows, 512))
                qbd = jnp.where((tok == t) & (group == col_group), q_row, qbd)
            qbd_ref[u] = qbd.astype(BF16)
        m_ref[...] = jnp.full(m_ref.shape, NEG, F32)
        l_ref[...] = jnp.zeros_like(l_ref)
        acca_ref[...] = jnp.zeros_like(acca_ref)
        accb_ref[...] = jnp.zeros_like(accb_ref)
        fc_ref[...] = jnp.zeros_like(fc_ref)

    def softmax_update(u, s):
        m_prev = m_ref[u][:, 0:1]
        l_prev = l_ref[u][:, 0:1]
        m_new = jnp.maximum(m_prev, jnp.max(s, axis=1, keepdims=True))
        alpha = jnp.exp(m_prev - m_new)
        pr = jnp.exp(s - m_new)
        l_new = alpha * l_prev + jnp.sum(pr, axis=1, keepdims=True)
        m_ref[u] = jnp.broadcast_to(m_new, (rows, LANES))
        l_ref[u] = jnp.broadcast_to(l_new, (rows, LANES))
        return alpha, pr

    width = pps * page
    lane = lax.broadcasted_iota(jnp.int32, (rows, width), 1)
    (tok_a, _, head_a), _ = row_ids((rows, width))
    dist = (past + tok_a - (step_i * width + lane)).astype(F32)
    s_a = _dot(qbd_ref[0], jnp.concatenate([r[0].astype(BF16) for r in kta_refs], axis=1))
    alpha, pr = softmax_update(0, s_a - row_slope(head_a) * dist)
    v_all = jnp.concatenate(
        [jnp.concatenate([r[0, pl.ds(h, page, stride=H_A), :].astype(BF16) for h in range(H_A)], axis=1)
         for r in va_refs], axis=0)
    acca_ref[...] = alpha * acca_ref[...] + _dot(pr.astype(BF16), v_all)

    parts = []
    for j in range(pps):
        parts.extend(v.astype(F32) for v in _split3(lft_refs[j][0]))
        parts.append(jnp.zeros((8, page), F32))
    cs = _dot(jnp.concatenate(parts, axis=0).astype(BF16), upper_ref[...])
    f_carry = fc_ref[...][:, 0:1]
    f_pages = []
    for j in range(pps):
        local = cs[32 * j:32 * j + 8] + cs[32 * j + 8:32 * j + 16] + cs[32 * j + 16:32 * j + 24]
        f_page = local + f_carry
        f_carry = f_carry + local[:, page - 1:page]
        f_pages.append(jnp.concatenate([f_page] * n_tok, axis=0))
    fc_ref[...] = jnp.broadcast_to(f_carry, (8, LANES))
    s_b = _dot(qbd_ref[1], jnp.concatenate([r[0].astype(BF16) for r in ktb_refs], axis=1))
    alpha, pr = softmax_update(1, s_b - jnp.concatenate(f_pages, axis=1))
    vt_all = jnp.concatenate([r[0].astype(BF16) for r in vtb_refs], axis=1)
    accb_ref[...] = alpha * accb_ref[...] + _dot_nt(pr.astype(BF16), vt_all)

    @pl.when(step_i == n_steps - 1)
    def _():
        tn = lax.broadcasted_iota(jnp.int32, (rows, new_pad), 1)
        (tok_a, _, head_a), (tok_b, _, _) = row_ids((rows, new_pad))
        s_an = _dot_nt(qbd_ref[0], kan_ref[0].astype(BF16)) - row_slope(head_a) * (tok_a - tn).astype(F32)
        alpha, pr = softmax_update(0, jnp.where((tn <= tok_a) & (tn < n_tok), s_an, NEG))
        acca_ref[...] = alpha * acca_ref[...] + _dot(pr.astype(BF16), van_ref[0].astype(BF16))

        lfn = lfn_ref[0]
        t8 = lax.broadcasted_iota(jnp.int32, (8, new_pad), 1)
        f_new = jnp.broadcast_to(f_carry, (8, new_pad))
        for t in range(n_tok):
            f_new = f_new + jnp.where(t8 >= t, jnp.broadcast_to(lfn[:, t:t + 1], (8, new_pad)), 0.0)
        s_bn = _dot_nt(qbd_ref[1], kbn_ref[0].astype(BF16)) - jnp.concatenate([f_new] * n_tok, axis=0)
        alpha, pr = softmax_update(1, jnp.where((tn <= tok_b) & (tn < n_tok), s_bn, NEG))
        accb_ref[...] = alpha * accb_ref[...] + _dot(pr.astype(BF16), vbn_ref[0].astype(BF16))

        on_a = acca_ref[...] * (1.0 / l_ref[0][:, 0:1])
        for h in range(H_A):
            blk = on_a[8 * h:8 * h + 8, h * DV_A:(h + 1) * DV_A]
            oa_ref[0, :, h * DV_A:(h + 1) * DV_A] = blk[0:n_tok] - lam_ref[0] * blk[n_tok:2 * n_tok]
        head_b = row_ids((rows, 512))[1][2]
        col_head = lax.broadcasted_iota(jnp.int32, (rows, 512), 1) >> 6
        on_b = jnp.where(head_b == col_head, accb_ref[...] * (1.0 / l_ref[1][:, 0:1]), 0.0)
        for t in range(n_tok):
            ob_ref[0, t:t + 1, :] = jnp.sum(on_b[8 * t:8 * t + 8], axis=0, keepdims=True)


def _decode(page_table, lam, qa, qb, pools, news, lfn_t, upper):
    n_seq, n_pages = page_table.shape
    page = pools[0].shape[2]
    pps = PAGES_PER_STEP
    new_pad = news[0].shape[1]
    past = n_pages * page
    pool_spec = lambda r, j: pl.BlockSpec((1, r, page), lambda n, s, pt: (pt[n * n_pages + s * pps + j], 0, 0))
    seq_spec = lambda r: pl.BlockSpec((1, r, 512), lambda n, s, pt: (n, 0, 0))
    pool_specs, pool_args = [], []
    for pool in pools:
        for j in range(pps):
            pool_specs.append(pool_spec(pool.shape[1], j))
            pool_args.append(pool)
    grid_spec = pltpu.PrefetchScalarGridSpec(
        num_scalar_prefetch=1,
        grid=(n_seq, n_pages // pps),
        in_specs=[pl.BlockSpec(memory_space=pltpu.SMEM), seq_spec(4), seq_spec(4)] + pool_specs
                 + [seq_spec(new_pad)] * 4
                 + [pl.BlockSpec((1, H_B, new_pad), lambda n, s, pt: (n, 0, 0)),
                    pl.BlockSpec((page, page), lambda n, s, pt: (0, 0))],
        out_specs=[seq_spec(4), seq_spec(4)],
        scratch_shapes=[pltpu.VMEM((2, 32, 512), BF16), pltpu.VMEM((2, 32, LANES), F32),
                        pltpu.VMEM((2, 32, LANES), F32), pltpu.VMEM((32, 512), F32),
                        pltpu.VMEM((32, 512), F32), pltpu.VMEM((H_B, LANES), F32)],
    )
    return pl.pallas_call(
        functools.partial(_decode_kernel, past, pps),
        grid_spec=grid_spec,
        out_shape=[jax.ShapeDtypeStruct((n_seq, 4, 512), F32)] * 2,
        compiler_params=pltpu.CompilerParams(dimension_semantics=("arbitrary", "arbitrary"),
                                             vmem_limit_bytes=VMEM_LIMIT),
        name="decode",
    )(page_table.reshape(-1), lam, qa, qb, *pool_args, *news, lfn_t, upper)


def _merge_kernel(x_ref, oa_ref, ob_ref, sga_ref, sgb_ref, sma_ref, smb_ref, gres_ref, gsub_ref,
                  woa_ref, wob_ref, wo_ref, y_ref):
    oa = oa_ref[...]
    heads = []
    for h in range(H_A):
        oh = oa[:, h * DV_A:(h + 1) * DV_A]
        heads.append(oh * lax.rsqrt(jnp.mean(oh * oh, axis=-1, keepdims=True) + EPS))
    oa = jnp.concatenate(heads, axis=1) * gsub_ref[...]
    ya = _dot((oa * sga_ref[...].astype(F32)).astype(BF16), woa_ref[...])
    yb = _dot((ob_ref[...] * sgb_ref[...].astype(F32)).astype(BF16), wob_ref[...])
    mix = sma_ref[...].astype(F32) * ya + smb_ref[...].astype(F32) * yb
    y = _dot(mix.astype(BF16), wo_ref[...])
    y_ref[...] = x_ref[...] + gres_ref[...] * y


def _merge(x2, oa, ob, sga, sgb, sma, smb, gres, gres_spec, wts):
    rows = x2.shape[0]
    row = lambda w: pl.BlockSpec((TM, w), lambda i: (i, 0))
    consts = [wts["gsub"], wts["woa"], wts["wob"], wts["wo"]]
    return pl.pallas_call(
        _merge_kernel,
        grid=(rows // TM,),
        in_specs=[row(D_MODEL), row(512), row(512), row(512), row(512), row(1024), row(1024), gres_spec]
                 + [_const_spec(c.shape) for c in consts],
        out_specs=row(D_MODEL),
        out_shape=jax.ShapeDtypeStruct((rows, D_MODEL), F32),
        compiler_params=pltpu.CompilerParams(dimension_semantics=("arbitrary",),
                                             vmem_limit_bytes=VMEM_LIMIT),
        name="merge",
    )(x2, oa, ob, sga, sgb, sma, smb, gres, *consts)


def _ones_constants():
    gsum = np.kron(np.eye(N_GROUPS), np.ones((DH_A, DH_A)))
    tri = np.tril(np.ones((TM, TM)))
    upper = np.triu(np.ones((LANES, LANES)))
    bf = lambda a: jnp.asarray(a, BF16)
    return dict(gsum=bf(gsum), tri=bf(tri), upper=bf(upper))


def _bf16_ceil(x):
    y = x.astype(BF16).astype(F32)
    return jnp.where(y < x, y * (1.0 + 2.0 ** -7), y)


def kernel(x_prompt, x_sample, cache_a_k, cache_a_v, cache_b_k, cache_b_v, cache_b_logf, page_table,
           c_prompt, c_sample, w_ada, b_ada, w_in, b_f, g_q_a, g_k_a, g_q_b, g_k_b,
           lambda_q1, lambda_k1, lambda_q2, lambda_k2, g_sub_a, w_out_a, w_out_b, w_o):
    assert w_ada.shape[0] == 1, "single-layer step"
    batch, seq, _ = x_prompt.shape
    n_seq, n_tok, _ = x_sample.shape
    n_pool, page = cache_a_k.shape[1], cache_a_k.shape[2]
    layer = 0
    lam_init = 0.8 - 0.6 * math.exp(-0.3 * layer)
    lam = (jnp.exp(jnp.sum(lambda_q1[layer] * lambda_k1[layer]))
           - jnp.exp(jnp.sum(lambda_q2[layer] * lambda_k2[layer])) + lam_init).reshape(1).astype(F32)

    w = w_in[layer]
    sec = np.cumsum((0, 512, 512, 512, 512, 512, 512, 512, H_B, 512, 1024, 1024))
    cols = lambda k: w[:, sec[k]:sec[k + 1]]
    wq = jnp.concatenate([cols(0), cols(4)], axis=1).astype(BF16)
    wn = jnp.concatenate([cols(1), cols(5), cols(2), cols(6), cols(3), cols(8), cols(9), cols(10)],
                         axis=1).astype(BF16)
    wf = jnp.pad(cols(7), ((0, 0), (0, LANES - H_B))).astype(BF16)
    gq = jnp.concatenate([jnp.tile(g_q_a[layer], N_GROUPS) * DH_A ** -0.5,
                          jnp.tile(g_q_b[layer], N_GROUPS) * DH_B ** -0.5])
    gk = jnp.concatenate([jnp.tile(g_k_a[layer], N_GROUPS), jnp.tile(g_k_b[layer], N_GROUPS)])
    wts = dict(_ones_constants(),
               wq=wq, wqkt=jnp.concatenate([wq, wn[:, 0:1024], wn[:, 1536:2048]], axis=1).T,
               wn=wn, wn_prompt=jnp.concatenate([wn[:, 1024:1536], wn[:, 2048:]], axis=1),

               wf=wf, bf=jnp.pad(b_f[layer], (0, LANES - H_B)).reshape(1, LANES),
               gqk_col=jnp.concatenate([gq, gk]).reshape(-1, 1), gq_row=gq.reshape(1, -1), gk_row=gk.reshape(1, -1),
               gsub=(jnp.tile(g_sub_a[layer], H_A) * (1.0 - lam_init)).reshape(1, W_A),
               woa=w_out_a[layer].astype(BF16), wob=w_out_b[layer].astype(BF16), wo=w_o[layer].astype(BF16))
    bound = lambda gq_, gk_: _bf16_ceil(8.1 * jnp.max(jnp.abs(gq_ * gk_)))
    mshift = jnp.stack([bound(g_q_a[layer], g_k_a[layer]), bound(g_q_b[layer], g_k_b[layer])]).astype(F32)

    n_c = batch + n_seq
    c_all = jnp.pad(jnp.concatenate([c_prompt, c_sample], axis=0), ((0, -n_c % 8), (0, 0)))
    mod = _ada(c_all, w_ada[layer], b_ada[layer])
    shift, scale, gres = (mod[:, k * D_MODEL:(k + 1) * D_MODEL] for k in range(3))

    xp2 = x_prompt.reshape(batch * seq, D_MODEL)
    p3 = lambda a: a[:batch].reshape(batch, 1, D_MODEL)
    (kta, va, ktb, vtbo, lft, qta, kaa, vta, qtb, kab, vtb, sga, sgb, sma, smb) = _proj_prompt(
        xp2, p3(scale), p3(shift), mshift, wts, batch, seq)
    online = jnp.max(mshift) > FAST_PATH_MAX_BOUND
    attend = lambda *a: lax.cond(online, lambda: _attention(*a, True), lambda: _attention(*a, False))
    oa = attend(qta, kaa, vta, lam, True).reshape(batch * seq, W_A)
    ob = attend(qtb, kab, vtb, lam, False).reshape(batch * seq, W_B)
    nb = seq // TM
    gres_p_spec = pl.BlockSpec((None, 1, D_MODEL), lambda i: (i // nb, 0, 0))
    yp = _merge(xp2, oa, ob, sga, sgb, sma, smb, p3(gres), gres_p_spec, wts).reshape(batch, seq, D_MODEL)

    xs2 = x_sample.reshape(n_seq * n_tok, D_MODEL)
    rep = lambda a: jnp.repeat(a[batch:n_c], n_tok, axis=0)
    (qa_s, qb_s, ka_s, va_s, kb_s, vb_s, lf_s, sga_s, sgb_s, sma_s, smb_s) = _proj_sample(
        xs2, rep(scale), rep(shift), wts)
    new_pad = 16
    seq3 = lambda a: a.reshape(n_seq, n_tok, 512)
    padn = lambda a: jnp.pad(seq3(a), ((0, 0), (0, new_pad - n_tok), (0, 0)))
    pools = [jnp.transpose(cache_a_k[layer], (0, 2, 3, 4, 1)).reshape(n_pool, 512, page),
             cache_a_v[layer].reshape(n_pool, page * H_A, DV_A),
             jnp.transpose(cache_b_k[layer], (0, 2, 3, 1)).reshape(n_pool, 512, page),
             jnp.transpose(cache_b_v[layer], (0, 2, 3, 1)).reshape(n_pool, 512, page),
             jnp.swapaxes(cache_b_logf[layer], 1, 2)]
    lfn_t = jnp.pad(jnp.swapaxes(lf_s.reshape(n_seq, n_tok, H_B), 1, 2), ((0, 0), (0, 0), (0, new_pad - n_tok)))
    oa_s, ob_s = _decode(page_table, lam, seq3(qa_s), seq3(qb_s), pools,
                         [padn(ka_s), padn(va_s), padn(kb_s), padn(vb_s)], lfn_t, wts["upper"])
    gres_s_spec = pl.BlockSpec((TM, D_MODEL), lambda i: (i, 0))
    ys = _merge(xs2, oa_s.reshape(-1, W_A), ob_s.reshape(-1, W_B), sga_s, sgb_s, sma_s, smb_s,
                rep(gres), gres_s_spec, wts).reshape(n_seq, n_tok, D_MODEL)

    return (yp, ys,
            jnp.moveaxis(kta.reshape(1, batch, 2, H_A, DH_A, seq), -1, 2), va.reshape(1, batch, seq, H_A, DV_A),
            jnp.moveaxis(ktb.reshape(1, batch, H_B, DH_B, seq), -1, 2),
            jnp.moveaxis(vtbo.reshape(1, batch, H_B, DH_B, seq), -1, 2),
            jnp.moveaxis(lft.reshape(1, batch, H_B, seq), -1, 2),
            ka_s.reshape(1, n_seq, n_tok, 2, H_A, DH_A), va_s.reshape(1, n_seq, n_tok, H_A, DV_A),
            kb_s.reshape(1, n_seq, n_tok, H_B, DH_B), vb_s.reshape(1, n_seq, n_tok, H_B, DH_B),
            lf_s.reshape(1, n_seq, n_tok, H_B))
```

```python
import functools
import math

import numpy as np
import jax
import jax.numpy as jnp
from jax import lax
from jax.experimental import pallas as pl
from jax.experimental.pallas import tpu as pltpu

F32 = jnp.float32
BF16 = jnp.bfloat16

D_MODEL = 1024
H_A, DH_A, DV_A = 4, 64, 128
H_B, DH_B = 8, 64
W_A = H_A * DV_A
W_B = H_B * DH_B
N_GROUPS = 8
EPS = 1e-6
NEG = -1e30
SLOPES = tuple(2.0 ** (-8.0 * (h + 1) / H_A) for h in range(H_A))
LANES = 128

TM = 256
TM_MERGE = 512
TQ = 512
TK = 512
VT_CHUNK = TM
PAGES_PER_STEP = 8
FAST_PATH_MAX_BOUND = 30.0
VMEM_LIMIT = 56 * 1024 * 1024


def _dot(a, b):
    return jnp.dot(a, b, preferred_element_type=F32)


def _dot_nt(a, b):
    return lax.dot_general(a, b, (((1,), (1,)), ((), ())), preferred_element_type=F32)


def _split3(x):
    hi = x.astype(BF16)
    r1 = x - hi.astype(F32)
    mid = r1.astype(BF16)
    lo = (r1 - mid.astype(F32)).astype(BF16)
    return hi, mid, lo


def _silu(x):
    return x * jax.nn.sigmoid(x)


def _const_spec(shape):
    nd = len(shape)
    return pl.BlockSpec(shape, lambda *_: (0,) * nd, pipeline_mode=pl.Buffered(1))


def _ada_kernel(c_ref, w_ref, b_ref, o_ref):
    c = c_ref[...]
    o_ref[...] = _dot(_silu(c).astype(BF16), w_ref[...].astype(BF16)) + b_ref[...]


def _ada(c_all, w_ada, b_ada):
    rows = c_all.shape[0]
    n = w_ada.shape[1]
    bn = 1024
    return pl.pallas_call(
        _ada_kernel,
        grid=(n // bn,),
        in_specs=[pl.BlockSpec((rows, D_MODEL), lambda j: (0, 0)),
                  pl.BlockSpec((D_MODEL, bn), lambda j: (0, j)),
                  pl.BlockSpec((1, bn), lambda j: (0, j))],
        out_specs=pl.BlockSpec((rows, bn), lambda j: (0, j)),
        out_shape=jax.ShapeDtypeStruct((rows, n), F32),
        name="ada",
    )(c_all, w_ada, b_ada.reshape(1, n))


def _hidden(x_ref, scale_ref, shift_ref):
    x = x_ref[...]
    ms = jnp.mean(x * x, axis=-1, keepdims=True)
    h = x * lax.rsqrt(ms + EPS) * (1.0 + scale_ref[...]) + shift_ref[...]
    return h.astype(BF16)


def _group_rms(z, g_ref):
    zz = z * z
    hi = zz.astype(BF16)
    lo = (zz - hi.astype(F32)).astype(BF16)
    ss = _dot(hi, g_ref[...]) + _dot(lo, g_ref[...])
    return z * lax.rsqrt(ss * (1.0 / DH_A) + EPS)


def _log_sigmoid(z):
    return jnp.minimum(z, 0.0) - jnp.log1p(jnp.exp(-jnp.abs(z)))


def _logf(hb, wf_ref, bf_ref):
    z = _dot(hb, wf_ref[...]) + bf_ref[...]
    lane = lax.broadcasted_iota(jnp.int32, z.shape, 1)
    return jnp.where(lane < H_B, _log_sigmoid(z), 0.0)


def _proj_prompt_kernel(nb, mshift_ref, x_ref, scale_ref, shift_ref, wqkt_ref, wn_ref, wf_ref, bf_ref,
                        gqk_ref, tri_ref,
                        kta_ref, va_ref, ktb_ref, vtbo_ref, lft_ref,
                        qta_ref, kaa_ref, vta_ref, qtb_ref, kab_ref, vtb_ref,
                        sga_ref, sgb_ref, sma_ref, smb_ref, carry_ref):
    tm = x_ref.shape[0]
    ib = pl.program_id(0) % nb
    hb = _hidden(x_ref, scale_ref, shift_ref)

    lf = _logf(hb, wf_ref, bf_ref)
    lft_ref[0] = lf.T[0:H_B, :]
    hi, mid, lo = _split3(lf)
    tri = tri_ref[...]
    f_local = _dot(tri, hi) + _dot(tri, mid) + _dot(tri, lo)

    @pl.when(ib == 0)
    def _():
        carry_ref[...] = jnp.zeros_like(carry_ref)

    f_nat = f_local + carry_ref[...]
    carry_ref[...] = f_nat[tm - 1:tm, :]
    f_t = f_nat.T[0:H_B, :]

    n_g = 4 * N_GROUPS
    n_qk = n_g * DH_A
    qkt = jnp.concatenate([_dot_nt(wqkt_ref[0:n_qk // 2, :], hb),
                           _dot_nt(wqkt_ref[n_qk // 2:n_qk, :], hb)], axis=0)
    g3 = qkt.reshape(n_g, DH_A, tm)
    ssq = jnp.sum(g3 * g3, axis=1, keepdims=True)
    g3 = (qkt * gqk_ref[...]).reshape(n_g, DH_A, tm) * lax.rsqrt(ssq * (1.0 / DH_A) + EPS)
    kta_ref[0] = g3[2 * N_GROUPS:3 * N_GROUPS].reshape(N_GROUPS * DH_A, tm)
    ktb_ref[0] = g3[3 * N_GROUPS:4 * N_GROUPS].reshape(N_GROUPS * DH_A, tm)

    r8 = lax.broadcasted_iota(jnp.int32, (8, tm), 0)
    pos = ib * tm + lax.broadcasted_iota(jnp.int32, (8, tm), 1)
    pos_hi = (pos >> 7).astype(F32)
    pos_lo = (pos & 127).astype(F32)
    zeros_tail = jnp.zeros((LANES - DH_A - 8, tm), F32)
    m_a = mshift_ref[0]
    m_b = mshift_ref[1]

    def operand(group, aug):
        return jnp.concatenate([g3[group], aug, zeros_tail], axis=0)

    k_aug_a = jnp.where(r8 == 2, pos_hi, jnp.where(r8 == 3, pos_lo,
              jnp.where((r8 <= 1) | (r8 == 4), 1.0, 0.0)))
    for g in range(N_GROUPS):
        slope = SLOPES[g % H_A]
        q_aug = jnp.where(r8 == 0, -slope * 128.0 * pos_hi,
                jnp.where(r8 == 1, -slope * pos_lo,
                jnp.where(r8 == 2, slope * 128.0,
                jnp.where(r8 == 3, slope,
                jnp.where(r8 == 4, -m_a, 0.0)))))
        qta_ref[0, g * LANES:(g + 1) * LANES, :] = operand(g, q_aug).astype(BF16)
        kaa_ref[0, :, g * LANES:(g + 1) * LANES] = operand(2 * N_GROUPS + g, k_aug_a).T.astype(BF16)
    ft_hi, ft_mid, ft_lo = _split3(f_t)
    for g in range(N_GROUPS):
        bc = lambda v: jnp.broadcast_to(v[g:g + 1, :].astype(F32), (8, tm))
        q_aug = jnp.where(r8 == 0, bc(ft_hi),
                jnp.where(r8 == 1, bc(ft_mid),
                jnp.where(r8 == 2, bc(ft_lo),
                jnp.where(r8 <= 5, -1.0,
                jnp.where(r8 == 6, -m_b, 0.0)))))
        k_aug = jnp.where(r8 <= 2, 1.0,
                jnp.where(r8 == 3, bc(ft_hi),
                jnp.where(r8 == 4, bc(ft_mid),
                jnp.where(r8 == 5, bc(ft_lo),
                jnp.where(r8 == 6, 1.0, 0.0)))))
        qtb_ref[0, g * LANES:(g + 1) * LANES, :] = operand(N_GROUPS + g, q_aug).astype(BF16)
        kab_ref[0, :, g * LANES:(g + 1) * LANES] = operand(3 * N_GROUPS + g, k_aug).T.astype(BF16)

    va = _dot(hb, wn_ref[:, 0:512])
    for h in range(H_A):
        va_ref[pl.ds(h, tm, stride=H_A), :] = va[:, h * DV_A:(h + 1) * DV_A]
    vta_ref[0, 0] = va.T.astype(BF16)
    vbt = _dot_nt(wqkt_ref[n_qk:, :], hb)
    vtbo_ref[0] = vbt
    vtb_ref[0, 0] = vbt.astype(BF16)

    sga_ref[...] = _silu(_dot(hb, wn_ref[:, 512:1024])).astype(BF16)
    sgb_ref[...] = _silu(_dot(hb, wn_ref[:, 1024:1536])).astype(BF16)
    sma_ref[...] = jax.nn.sigmoid(_dot(hb, wn_ref[:, 1536:2560])).astype(BF16)
    smb_ref[...] = jax.nn.sigmoid(_dot(hb, wn_ref[:, 2560:3584])).astype(BF16)


def _proj_prompt(x2, scale, shift, mshift, wts, batch, seq):
    rows = batch * seq
    nb = seq // TM
    nk = seq // VT_CHUNK
    row = lambda w: pl.BlockSpec((TM, w), lambda i: (i, 0))
    mod = pl.BlockSpec((None, 1, D_MODEL), lambda i: (i // nb, 0, 0))
    consts = [wts["wqkt"], wts["wn_prompt"], wts["wf"], wts["bf"], wts["gqk_col"], wts["tri"]]
    seq_t = lambda r: jax.ShapeDtypeStruct((batch, r, seq), F32)
    out_shape = [seq_t(512), jax.ShapeDtypeStruct((rows * H_A, DV_A), F32), seq_t(512), seq_t(512),
                 seq_t(H_B)] + [
        jax.ShapeDtypeStruct((batch, 1024, seq), BF16), jax.ShapeDtypeStruct((batch, seq, 1024), BF16),
        jax.ShapeDtypeStruct((batch, nk, W_A, VT_CHUNK), BF16),
        jax.ShapeDtypeStruct((batch, 1024, seq), BF16), jax.ShapeDtypeStruct((batch, seq, 1024), BF16),
        jax.ShapeDtypeStruct((batch, nk, W_B, VT_CHUNK), BF16),
        jax.ShapeDtypeStruct((rows, 512), BF16), jax.ShapeDtypeStruct((rows, 512), BF16),
        jax.ShapeDtypeStruct((rows, 1024), BF16), jax.ShapeDtypeStruct((rows, 1024), BF16)]
    qt_spec = pl.BlockSpec((1, 1024, TM), lambda i: (i // nb, 0, i % nb))
    k_spec = pl.BlockSpec((1, TM, 1024), lambda i: (i // nb, i % nb, 0))
    vt_spec = pl.BlockSpec((1, TM // VT_CHUNK, 512, VT_CHUNK), lambda i: (i // nb, i % nb, 0, 0))
    out_t = lambda r: pl.BlockSpec((1, r, TM), lambda i: (i // nb, 0, i % nb))
    out_specs = [out_t(512), pl.BlockSpec((TM * H_A, DV_A), lambda i: (i, 0)), out_t(512), out_t(512),
                 out_t(H_B)] + [
        qt_spec, k_spec, vt_spec, qt_spec, k_spec, vt_spec, row(512), row(512), row(1024), row(1024)]
    return pl.pallas_call(
        functools.partial(_proj_prompt_kernel, nb),
        grid=(rows // TM,),
        in_specs=[pl.BlockSpec(memory_space=pltpu.SMEM), row(D_MODEL), mod, mod]
                 + [_const_spec(c.shape) for c in consts],
        out_specs=out_specs,
        out_shape=out_shape,
        scratch_shapes=[pltpu.VMEM((1, LANES), F32)],
        compiler_params=pltpu.CompilerParams(dimension_semantics=("arbitrary",),
                                             vmem_limit_bytes=VMEM_LIMIT),
        name="proj_prompt",
    )(mshift, x2, scale, shift, *consts)


def _proj_sample_kernel(x_ref, scale_ref, shift_ref, wq_ref, wn_ref, wf_ref, bf_ref, gq_ref, gk_ref, g_ref,
                        qa_ref, qb_ref, ka_ref, va_ref, kb_ref, vb_ref, lf_ref,
                        sga_ref, sgb_ref, sma_ref, smb_ref):
    hb = _hidden(x_ref, scale_ref, shift_ref)
    lf_ref[...] = _logf(hb, wf_ref, bf_ref)[:, :H_B]
    qa_ref[...] = _group_rms(_dot(hb, wq_ref[:, 0:512]), g_ref) * gq_ref[:, 0:512]
    qb_ref[...] = _group_rms(_dot(hb, wq_ref[:, 512:1024]), g_ref) * gq_ref[:, 512:1024]
    ka_ref[...] = _group_rms(_dot(hb, wn_ref[:, 0:512]), g_ref) * gk_ref[:, 0:512]
    kb_ref[...] = _group_rms(_dot(hb, wn_ref[:, 512:1024]), g_ref) * gk_ref[:, 512:1024]
    va_ref[...] = _dot(hb, wn_ref[:, 1024:1536])
    vb_ref[...] = _dot(hb, wn_ref[:, 1536:2048])
    sga_ref[...] = _silu(_dot(hb, wn_ref[:, 2048:2560])).astype(BF16)
    sgb_ref[...] = _silu(_dot(hb, wn_ref[:, 2560:3072])).astype(BF16)
    sma_ref[...] = jax.nn.sigmoid(_dot(hb, wn_ref[:, 3072:4096])).astype(BF16)
    smb_ref[...] = jax.nn.sigmoid(_dot(hb, wn_ref[:, 4096:5120])).astype(BF16)


def _proj_sample(x2, scale_rows, shift_rows, wts):
    rows = x2.shape[0]
    row = lambda w: pl.BlockSpec((TM, w), lambda i: (i, 0))
    consts = [wts["wq"], wts["wn"], wts["wf"], wts["bf"], wts["gq_row"], wts["gk_row"], wts["gsum"]]
    out_shape = [jax.ShapeDtypeStruct((rows, 512), F32)] * 6 + [jax.ShapeDtypeStruct((rows, H_B), F32)] + [
        jax.ShapeDtypeStruct((rows, 512), BF16), jax.ShapeDtypeStruct((rows, 512), BF16),
        jax.ShapeDtypeStruct((rows, 1024), BF16), jax.ShapeDtypeStruct((rows, 1024), BF16)]
    out_specs = [row(512)] * 6 + [row(H_B), row(512), row(512), row(1024), row(1024)]
    return pl.pallas_call(
        _proj_sample_kernel,
        grid=(rows // TM,),
        in_specs=[row(D_MODEL), row(D_MODEL), row(D_MODEL)] + [_const_spec(c.shape) for c in consts],
        out_specs=out_specs,
        out_shape=out_shape,
        compiler_params=pltpu.CompilerParams(dimension_semantics=("arbitrary",),
                                             vmem_limit_bytes=VMEM_LIMIT),
        name="proj_sample",
    )(x2, scale_rows, shift_rows, *consts)


def _attn_kernel(shared_v, online, lam_ref, qt0_ref, qt1_ref, k0_ref, k1_ref, vt_ref, o_ref,
                 acc_ref, l_ref, p_ref, p2_ref, pd_ref):
    i = pl.program_id(2)
    qts = (qt0_ref[0], qt1_ref[0])
    k_refs = (k0_ref, k1_ref)
    dv = acc_ref.shape[1]
    n_chunks = TK // VT_CHUNK
    acc_ref[...] = jnp.zeros_like(acc_ref)

    def pv(u, chunk, p):
        blk = vt_ref[0, chunk]
        v = blk if shared_v else blk[u * dv:(u + 1) * dv, :]
        return _dot(v, p)

    def scores_to_probs(j, masked, dst_ref):
        for u in range(2):
            kblk = k_refs[u][0, pl.ds(pl.multiple_of(j * TK, TK), TK), :]
            s = _dot(kblk, qts[u])
            if masked:
                kpos = lax.broadcasted_iota(jnp.int32, s.shape, 0)
                qpos = lax.broadcasted_iota(jnp.int32, s.shape, 1)
                s = jnp.where(kpos <= qpos, s, NEG)
            p = jnp.exp(s)
            l_ref[u] += jnp.sum(p.reshape(TK // 8, 8, TQ), axis=0)
            dst_ref[u] = p.astype(BF16)

    def probs_times_values(j, src_ref):
        for u in range(2):
            upd = None
            for c in range(n_chunks):
                d = pv(u, j * n_chunks + c, src_ref[u, c * VT_CHUNK:(c + 1) * VT_CHUNK, :])
                upd = d if upd is None else upd + d
            acc_ref[u] += upd

    def step_online(j, carry, masked):
        out = []
        for u in range(2):
            kblk = k_refs[u][0, pl.ds(pl.multiple_of(j * TK, TK), TK), :]
            s = _dot(kblk, qts[u])
            if masked:
                kpos = lax.broadcasted_iota(jnp.int32, s.shape, 0)
                qpos = lax.broadcasted_iota(jnp.int32, s.shape, 1)
                s = jnp.where(kpos <= qpos, s, NEG)
            m, l = carry[u]
            m_new = jnp.maximum(m, jnp.max(s, axis=0, keepdims=True))
            alpha = jnp.exp(m - m_new)
            p = jnp.exp(s - m_new)
            l = alpha * l + jnp.sum(p, axis=0, keepdims=True)
            p = p.astype(BF16)
            upd = None
            for c in range(n_chunks):
                d = pv(u, j * n_chunks + c, p[c * VT_CHUNK:(c + 1) * VT_CHUNK, :])
                upd = d if upd is None else upd + d
            acc_ref[u] = alpha * acc_ref[u] + upd
            out.append((m_new, l))
        return tuple(out)

    if online:
        init = tuple((jnp.full((1, TQ), NEG, F32), jnp.zeros((1, TQ), F32)) for _ in range(2))
        carry = lax.fori_loop(0, i, lambda j, c: step_online(j, c, False), init)
        carry = step_online(i, carry, True)
        l0, l1 = carry[0][1], carry[1][1]
    else:
        l_ref[...] = jnp.zeros_like(l_ref)

        @pl.when(i == 0)
        def _():
            scores_to_probs(0, True, pd_ref)
            probs_times_values(0, pd_ref)

        @pl.when(i > 0)
        def _():
            scores_to_probs(0, False, p_ref)

            def body(jj, _):
                j = 2 * jj + 1
                scores_to_probs(j, False, p2_ref)
                probs_times_values(j - 1, p_ref)
                scores_to_probs(j + 1, False, p_ref)
                probs_times_values(j, p2_ref)
                return 0

            lax.fori_loop(0, (i - 1) // 2, body, 0)

            @pl.when(i % 2 == 0)
            def _():
                scores_to_probs(i - 1, False, p2_ref)
                probs_times_values(i - 2, p_ref)
                scores_to_probs(i, True, pd_ref)
                probs_times_values(i - 1, p2_ref)
                probs_times_values(i, pd_ref)

            @pl.when(i % 2 == 1)
            def _():
                scores_to_probs(i, True, pd_ref)
                probs_times_values(i - 1, p_ref)
                probs_times_values(i, pd_ref)

        l0, l1 = (jnp.sum(l_ref[u], axis=0, keepdims=True) for u in range(2))

    o0 = acc_ref[0] * (1.0 / l0)
    o1 = acc_ref[1] * (1.0 / l1)
    o = o0 - lam_ref[0] * o1 if shared_v else jnp.concatenate([o0, o1], axis=0)
    o_ref[0] = o.T


def _attention(qt, kaug, vt, lam, shared_v, online):
    batch, _, seq = qt.shape
    n_steps = 4
    if shared_v:
        g0 = lambda h: h
        g1 = lambda h: H_A + h
    else:
        g0 = lambda h: 2 * h
        g1 = lambda h: 2 * h + 1
    dv = DV_A if shared_v else DH_B
    qspec = lambda g: pl.BlockSpec((1, LANES, TQ), lambda b, h, i: (b, g(h), i))
    kspec = lambda g: pl.BlockSpec((1, seq, LANES), lambda b, h, i: (b, 0, g(h)))
    return pl.pallas_call(
        functools.partial(_attn_kernel, shared_v, online),
        grid=(batch, n_steps, seq // TQ),
        in_specs=[pl.BlockSpec(memory_space=pltpu.SMEM), qspec(g0), qspec(g1), kspec(g0), kspec(g1),
                  pl.BlockSpec((1, seq // VT_CHUNK, LANES, VT_CHUNK), lambda b, h, i: (b, 0, h, 0))],
        out_specs=pl.BlockSpec((1, TQ, LANES), lambda b, h, i: (b, i, h)),
        out_shape=jax.ShapeDtypeStruct((batch, seq, 512), F32),
        scratch_shapes=[pltpu.VMEM((2, dv, TQ), F32), pltpu.VMEM((2, 8, TQ), F32),
                        pltpu.VMEM((2, TK, TQ), BF16), pltpu.VMEM((2, TK, TQ), BF16),
                        pltpu.VMEM((2, TK, TQ), BF16)],
        compiler_params=pltpu.CompilerParams(
            dimension_semantics=("arbitrary", "arbitrary", "arbitrary"), vmem_limit_bytes=VMEM_LIMIT),
        name=("attn_a" if shared_v else "attn_b") + ("_online" if online else ""),
    )(lam, qt, qt, kaug, kaug, vt)


def _decode_kernel(past, pps, pt_ref, lam_ref, qa_ref, qb_ref, *refs):
    del pt_ref
    kta_refs, va_refs, ktb_refs, vtb_refs, lft_refs = (refs[k * pps:(k + 1) * pps] for k in range(5))
    (kan_ref, van_ref, kbn_ref, vbn_ref, lfn_ref, upper_ref,
     oa_ref, ob_ref, qbd_ref, m_ref, l_ref, acca_ref, accb_ref, fc_ref) = refs[5 * pps:]
    step_i = pl.program_id(1)
    n_steps = pl.num_programs(1)
    rows = 32
    n_tok = 4
    new_pad = kan_ref.shape[1]
    page = LANES

    def row_ids(shape):
        r = lax.broadcasted_iota(jnp.int32, shape, 0)
        return ((r & 3, ((r >> 2) & 1) * H_A + (r >> 3), r >> 3),
                (r >> 3, r & 7, r & 7))

    def row_slope(head):
        return jnp.where(head == 0, SLOPES[0], jnp.where(head == 1, SLOPES[1],
               jnp.where(head == 2, SLOPES[2], SLOPES[3])))

    @pl.when(step_i == 0)
    def _():
        col_group = lax.broadcasted_iota(jnp.int32, (rows, 512), 1) >> 6
        for u, q_ref in enumerate((qa_ref, qb_ref)):
            tok, group, _ = row_ids((rows, 512))[u]
            qbd = jnp.zeros((rows, 512), F32)
            for t in range(n_tok):
                q_row = jnp.broadcast_to(q_ref[0, t:t + 1, :], (rows, 512))
                qbd = jnp.where((tok == t) & (group == col_group), q_row, qbd)
            qbd_ref[u] = qbd.astype(BF16)
        m_ref[...] = jnp.full(m_ref.shape, NEG, F32)
        l_ref[...] = jnp.zeros_like(l_ref)
        acca_ref[...] = jnp.zeros_like(acca_ref)
        accb_ref[...] = jnp.zeros_like(accb_ref)
        fc_ref[...] = jnp.zeros_like(fc_ref)

    def softmax_update(u, s):
        m_prev = m_ref[u][:, 0:1]
        l_prev = l_ref[u][:, 0:1]
        m_new = jnp.maximum(m_prev, jnp.max(s, axis=1, keepdims=True))
        alpha = jnp.exp(m_prev - m_new)
        pr = jnp.exp(s - m_new)
        l_new = alpha * l_prev + jnp.sum(pr, axis=1, keepdims=True)
        m_ref[u] = jnp.broadcast_to(m_new, (rows, LANES))
        l_ref[u] = jnp.broadcast_to(l_new, (rows, LANES))
        return alpha, pr

    is_last = step_i == n_steps - 1
    width = pps * page
    lane = lax.broadcasted_iota(jnp.int32, (rows, width), 1)
    tn = lax.broadcasted_iota(jnp.int32, (rows, page), 1)
    (tok_a, _, head_a), (tok_b, _, _) = row_ids((rows, page))
    pad_rows = lambda ref: jnp.concatenate(
        [ref[0], jnp.zeros((page - new_pad, 512), F32)], axis=0).astype(BF16)

    def valid(tok):
        return is_last & (tn <= tok) & (tn < n_tok)

    dist = (past + row_ids((rows, width))[0][0] - (step_i * width + lane)).astype(F32)
    s_a = _dot(qbd_ref[0], jnp.concatenate([r[0].astype(BF16) for r in kta_refs], axis=1))
    s_a = s_a - row_slope(row_ids((rows, width))[0][2]) * dist
    s_an = _dot_nt(qbd_ref[0], pad_rows(kan_ref)) - row_slope(head_a) * (tok_a - tn).astype(F32)
    alpha, pr = softmax_update(0, jnp.concatenate([s_a, jnp.where(valid(tok_a), s_an, NEG)], axis=1))
    pr = pr.astype(BF16)
    v_all = jnp.concatenate(
        [jnp.concatenate([r[0, pl.ds(h, page, stride=H_A), :].astype(BF16) for h in range(H_A)], axis=1)
         for r in va_refs], axis=0)
    acca_ref[...] = (alpha * acca_ref[...] + _dot(pr[:, 0:width], v_all)
                     + _dot(pr[:, width:], pad_rows(van_ref)))

    parts = []
    for j in range(pps):
        parts.extend(v.astype(F32) for v in _split3(lft_refs[j][0]))
        parts.append(jnp.zeros((8, page), F32))
    cs = _dot(jnp.concatenate(parts, axis=0).astype(BF16), upper_ref[...])
    f_carry = fc_ref[...][:, 0:1]
    f_pages = []
    for j in range(pps):
        local = cs[32 * j:32 * j + 8] + cs[32 * j + 8:32 * j + 16] + cs[32 * j + 16:32 * j + 24]
        f_page = local + f_carry
        f_carry = f_carry + local[:, page - 1:page]
        f_pages.append(jnp.concatenate([f_page] * n_tok, axis=0))
    fc_ref[...] = jnp.broadcast_to(f_carry, (8, LANES))
    lfn = lfn_ref[0]
    t8 = lax.broadcasted_iota(jnp.int32, (8, page), 1)
    f_new = jnp.broadcast_to(f_carry, (8, page))
    for t in range(n_tok):
        f_new = f_new + jnp.where(t8 >= t, jnp.broadcast_to(lfn[:, t:t + 1], (8, page)), 0.0)
    s_b = _dot(qbd_ref[1], jnp.concatenate([r[0].astype(BF16) for r in ktb_refs], axis=1))
    s_b = s_b - jnp.concatenate(f_pages, axis=1)
    s_bn = _dot_nt(qbd_ref[1], pad_rows(kbn_ref)) - jnp.concatenate([f_new] * n_tok, axis=0)
    alpha, pr = softmax_update(1, jnp.concatenate([s_b, jnp.where(valid(tok_b), s_bn, NEG)], axis=1))
    pr = pr.astype(BF16)
    vt_all = jnp.concatenate([r[0].astype(BF16) for r in vtb_refs], axis=1)
    accb_ref[...] = (alpha * accb_ref[...] + _dot_nt(pr[:, 0:width], vt_all)
                     + _dot(pr[:, width:], pad_rows(vbn_ref)))

    @pl.when(is_last)
    def _():
        on_a = acca_ref[...] * (1.0 / l_ref[0][:, 0:1])
        for h in range(H_A):
            blk = on_a[8 * h:8 * h + 8, h * DV_A:(h + 1) * DV_A]
            oa_ref[0, :, h * DV_A:(h + 1) * DV_A] = blk[0:n_tok] - lam_ref[0] * blk[n_tok:2 * n_tok]
        head_b = row_ids((rows, 512))[1][2]
        col_head = lax.broadcasted_iota(jnp.int32, (rows, 512), 1) >> 6
        on_b = jnp.where(head_b == col_head, accb_ref[...] * (1.0 / l_ref[1][:, 0:1]), 0.0)
        for t in range(n_tok):
            ob_ref[0, t:t + 1, :] = jnp.sum(on_b[8 * t:8 * t + 8], axis=0, keepdims=True)


def _decode(page_table, lam, qa, qb, pools, news, lfn_t, upper):
    n_seq, n_pages = page_table.shape
    page = pools[0].shape[2]
    pps = PAGES_PER_STEP
    new_pad = news[0].shape[1]
    past = n_pages * page
    pool_spec = lambda r, j: pl.BlockSpec((1, r, page), lambda n, s, pt: (pt[n * n_pages + s * pps + j], 0, 0))
    seq_spec = lambda r: pl.BlockSpec((1, r, 512), lambda n, s, pt: (n, 0, 0))
    pool_specs, pool_args = [], []
    for pool in pools:
        for j in range(pps):
            pool_specs.append(pool_spec(pool.shape[1], j))
            pool_args.append(pool)
    grid_spec = pltpu.PrefetchScalarGridSpec(
        num_scalar_prefetch=1,
        grid=(n_seq, n_pages // pps),
        in_specs=[pl.BlockSpec(memory_space=pltpu.SMEM), seq_spec(4), seq_spec(4)] + pool_specs
                 + [seq_spec(new_pad)] * 4
                 + [pl.BlockSpec((1, H_B, page), lambda n, s, pt: (n, 0, 0)),
                    pl.BlockSpec((page, page), lambda n, s, pt: (0, 0))],
        out_specs=[seq_spec(4), seq_spec(4)],
        scratch_shapes=[pltpu.VMEM((2, 32, 512), BF16), pltpu.VMEM((2, 32, LANES), F32),
                        pltpu.VMEM((2, 32, LANES), F32), pltpu.VMEM((32, 512), F32),
                        pltpu.VMEM((32, 512), F32), pltpu.VMEM((H_B, LANES), F32)],
    )
    return pl.pallas_call(
        functools.partial(_decode_kernel, past, pps),
        grid_spec=grid_spec,
        out_shape=[jax.ShapeDtypeStruct((n_seq, 4, 512), F32)] * 2,
        compiler_params=pltpu.CompilerParams(dimension_semantics=("arbitrary", "arbitrary"),
                                             vmem_limit_bytes=VMEM_LIMIT),
        name="decode",
    )(page_table.reshape(-1), lam, qa, qb, *pool_args, *news, lfn_t, upper)


def _merge_kernel(x_ref, oa_ref, ob_ref, sga_ref, sgb_ref, sma_ref, smb_ref, gres_ref, gsub_ref,
                  woa_ref, wob_ref, wo_ref, y_ref):
    oa = oa_ref[...]
    heads = []
    for h in range(H_A):
        oh = oa[:, h * DV_A:(h + 1) * DV_A]
        heads.append(oh * lax.rsqrt(jnp.mean(oh * oh, axis=-1, keepdims=True) + EPS))
    oa = jnp.concatenate(heads, axis=1) * gsub_ref[...]
    ya = _dot((oa * sga_ref[...].astype(F32)).astype(BF16), woa_ref[...])
    yb = _dot((ob_ref[...] * sgb_ref[...].astype(F32)).astype(BF16), wob_ref[...])
    mix = sma_ref[...].astype(F32) * ya + smb_ref[...].astype(F32) * yb
    y = _dot(mix.astype(BF16), wo_ref[...])
    y_ref[...] = x_ref[...] + gres_ref[...] * y


def _merge(x2, oa, ob, sga, sgb, sma, smb, gres, gres_spec, wts):
    rows = x2.shape[0]
    row = lambda w: pl.BlockSpec((TM_MERGE, w), lambda i: (i, 0))
    consts = [wts["gsub"], wts["woa"], wts["wob"], wts["wo"]]
    return pl.pallas_call(
        _merge_kernel,
        grid=(rows // TM_MERGE,),
        in_specs=[row(D_MODEL), row(512), row(512), row(512), row(512), row(1024), row(1024), gres_spec]
                 + [_const_spec(c.shape) for c in consts],
        out_specs=row(D_MODEL),
        out_shape=jax.ShapeDtypeStruct((rows, D_MODEL), F32),
        compiler_params=pltpu.CompilerParams(dimension_semantics=("arbitrary",),
                                             vmem_limit_bytes=VMEM_LIMIT),
        name="merge",
    )(x2, oa, ob, sga, sgb, sma, smb, gres, *consts)


def _ones_constants():
    gsum = np.kron(np.eye(N_GROUPS), np.ones((DH_A, DH_A)))
    tri = np.tril(np.ones((TM, TM)))
    upper = np.triu(np.ones((LANES, LANES)))
    bf = lambda a: jnp.asarray(a, BF16)
    return dict(gsum=bf(gsum), tri=bf(tri), upper=bf(upper))


def _bf16_ceil(x):
    y = x.astype(BF16).astype(F32)
    return jnp.where(y < x, y * (1.0 + 2.0 ** -7), y)


def kernel(x_prompt, x_sample, cache_a_k, cache_a_v, cache_b_k, cache_b_v, cache_b_logf, page_table,
           c_prompt, c_sample, w_ada, b_ada, w_in, b_f, g_q_a, g_k_a, g_q_b, g_k_b,
           lambda_q1, lambda_k1, lambda_q2, lambda_k2, g_sub_a, w_out_a, w_out_b, w_o):
    assert w_ada.shape[0] == 1, "single-layer step"
    batch, seq, _ = x_prompt.shape
    n_seq, n_tok, _ = x_sample.shape
    n_pool, page = cache_a_k.shape[1], cache_a_k.shape[2]
    layer = 0
    lam_init = 0.8 - 0.6 * math.exp(-0.3 * layer)
    lam = (jnp.exp(jnp.sum(lambda_q1[layer] * lambda_k1[layer]))
           - jnp.exp(jnp.sum(lambda_q2[layer] * lambda_k2[layer])) + lam_init).reshape(1).astype(F32)

    w = w_in[layer]
    sec = np.cumsum((0, 512, 512, 512, 512, 512, 512, 512, H_B, 512, 1024, 1024))
    cols = lambda k: w[:, sec[k]:sec[k + 1]]
    wq = jnp.concatenate([cols(0), cols(4)], axis=1).astype(BF16)
    wn = jnp.concatenate([cols(1), cols(5), cols(2), cols(6), cols(3), cols(8), cols(9), cols(10)],
                         axis=1).astype(BF16)
    wf = jnp.pad(cols(7), ((0, 0), (0, LANES - H_B))).astype(BF16)
    gq = jnp.concatenate([jnp.tile(g_q_a[layer], N_GROUPS) * DH_A ** -0.5,
                          jnp.tile(g_q_b[layer], N_GROUPS) * DH_B ** -0.5])
    gk = jnp.concatenate([jnp.tile(g_k_a[layer], N_GROUPS), jnp.tile(g_k_b[layer], N_GROUPS)])
    wts = dict(_ones_constants(),
               wq=wq, wqkt=jnp.concatenate([wq, wn[:, 0:1024], wn[:, 1536:2048]], axis=1).T,
               wn=wn, wn_prompt=jnp.concatenate([wn[:, 1024:1536], wn[:, 2048:]], axis=1),

               wf=wf, bf=jnp.pad(b_f[layer], (0, LANES - H_B)).reshape(1, LANES),
               gqk_col=jnp.concatenate([gq, gk]).reshape(-1, 1), gq_row=gq.reshape(1, -1), gk_row=gk.reshape(1, -1),
               gsub=(jnp.tile(g_sub_a[layer], H_A) * (1.0 - lam_init)).reshape(1, W_A),
               woa=w_out_a[layer].astype(BF16), wob=w_out_b[layer].astype(BF16), wo=w_o[layer].astype(BF16))
    bound = lambda gq_, gk_: _bf16_ceil(8.1 * jnp.max(jnp.abs(gq_ * gk_)))
    mshift = jnp.stack([bound(g_q_a[layer], g_k_a[layer]), bound(g_q_b[layer], g_k_b[layer])]).astype(F32)

    n_c = batch + n_seq
    c_all = jnp.pad(jnp.concatenate([c_prompt, c_sample], axis=0), ((0, -n_c % 8), (0, 0)))
    mod = _ada(c_all, w_ada[layer], b_ada[layer])
    shift, scale, gres = (mod[:, k * D_MODEL:(k + 1) * D_MODEL] for k in range(3))

    xp2 = x_prompt.reshape(batch * seq, D_MODEL)
    p3 = lambda a: a[:batch].reshape(batch, 1, D_MODEL)
    (kta, va, ktb, vtbo, lft, qta, kaa, vta, qtb, kab, vtb, sga, sgb, sma, smb) = _proj_prompt(
        xp2, p3(scale), p3(shift), mshift, wts, batch, seq)
    online = jnp.max(mshift) > FAST_PATH_MAX_BOUND
    attend = lambda *a: lax.cond(online, lambda: _attention(*a, True), lambda: _attention(*a, False))
    oa = attend(qta, kaa, vta, lam, True).reshape(batch * seq, W_A)
    ob = attend(qtb, kab, vtb, lam, False).reshape(batch * seq, W_B)
    nb = seq // TM_MERGE
    gres_p_spec = pl.BlockSpec((None, 1, D_MODEL), lambda i: (i // nb, 0, 0))
    yp = _merge(xp2, oa, ob, sga, sgb, sma, smb, p3(gres), gres_p_spec, wts).reshape(batch, seq, D_MODEL)

    xs2 = x_sample.reshape(n_seq * n_tok, D_MODEL)
    rep = lambda a: jnp.repeat(a[batch:n_c], n_tok, axis=0)
    (qa_s, qb_s, ka_s, va_s, kb_s, vb_s, lf_s, sga_s, sgb_s, sma_s, smb_s) = _proj_sample(
        xs2, rep(scale), rep(shift), wts)
    new_pad = 16
    seq3 = lambda a: a.reshape(n_seq, n_tok, 512)
    padn = lambda a: jnp.pad(seq3(a), ((0, 0), (0, new_pad - n_tok), (0, 0)))
    pools = [jnp.transpose(cache_a_k[layer], (0, 2, 3, 4, 1)).reshape(n_pool, 512, page),
             cache_a_v[layer].reshape(n_pool, page * H_A, DV_A),
             jnp.transpose(cache_b_k[layer], (0, 2, 3, 1)).reshape(n_pool, 512, page),
             jnp.transpose(cache_b_v[layer], (0, 2, 3, 1)).reshape(n_pool, 512, page),
             jnp.swapaxes(cache_b_logf[layer], 1, 2)]
    lfn_t = jnp.pad(jnp.swapaxes(lf_s.reshape(n_seq, n_tok, H_B), 1, 2), ((0, 0), (0, 0), (0, page - n_tok)))
    oa_s, ob_s = _decode(page_table, lam, seq3(qa_s), seq3(qb_s), pools,
                         [padn(ka_s), padn(va_s), padn(kb_s), padn(vb_s)], lfn_t, wts["upper"])
    gres_s_spec = pl.BlockSpec((TM_MERGE, D_MODEL), lambda i: (i, 0))
    ys = _merge(xs2, oa_s.reshape(-1, W_A), ob_s.reshape(-1, W_B), sga_s, sgb_s, sma_s, smb_s,
                rep(gres), gres_s_spec, wts).reshape(n_seq, n_tok, D_MODEL)

    return (yp, ys,
            jnp.moveaxis(kta.reshape(1, batch, 2, H_A, DH_A, seq), -1, 2), va.reshape(1, batch, seq, H_A, DV_A),
            jnp.moveaxis(ktb.reshape(1, batch, H_B, DH_B, seq), -1, 2),
            jnp.moveaxis(vtbo.reshape(1, batch, H_B, DH_B, seq), -1, 2),
            jnp.moveaxis(lft.reshape(1, batch, H_B, seq), -1, 2),
            ka_s.reshape(1, n_seq, n_tok, 2, H_A, DH_A), va_s.reshape(1, n_seq, n_tok, H_A, DV_A),
            kb_s.reshape(1, n_seq, n_tok, H_B, DH_B), vb_s.reshape(1, n_seq, n_tok, H_B, DH_B),
            lf_s.reshape(1, n_seq, n_tok, H_B))
```

```python
import functools
import math

import numpy as np
import jax
import jax.numpy as jnp
from jax import lax
from jax.experimental import pallas as pl
from jax.experimental.pallas import tpu as pltpu

F32 = jnp.float32
BF16 = jnp.bfloat16

D_MODEL = 1024
H_A, DH_A, DV_A = 4, 64, 128
H_B, DH_B = 8, 64
W_A = H_A * DV_A
W_B = H_B * DH_B
N_GROUPS = 8
EPS = 1e-6
NEG = -1e30
SLOPES = tuple(2.0 ** (-8.0 * (h + 1) / H_A) for h in range(H_A))
LANES = 128

TM = 256
TM_MERGE = 512
TQ = 512
TK = 512
VT_CHUNK = TM
PAGES_PER_STEP = 8
FAST_PATH_MAX_BOUND = 30.0
VMEM_LIMIT = 56 * 1024 * 1024


def _dot(a, b):
    return jnp.dot(a, b, preferred_element_type=F32)


def _dot_nt(a, b):
    return lax.dot_general(a, b, (((1,), (1,)), ((), ())), preferred_element_type=F32)


def _split3(x):
    hi = x.astype(BF16)
    r1 = x - hi.astype(F32)
    mid = r1.astype(BF16)
    lo = (r1 - mid.astype(F32)).astype(BF16)
    return hi, mid, lo


def _silu(x):
    return x * jax.nn.sigmoid(x)


def _const_spec(shape):
    nd = len(shape)
    return pl.BlockSpec(shape, lambda *_: (0,) * nd, pipeline_mode=pl.Buffered(1))


def _ada_kernel(c_ref, w_ref, b_ref, o_ref):
    c = c_ref[...]
    o_ref[...] = _dot(_silu(c).astype(BF16), w_ref[...].astype(BF16)) + b_ref[...]


def _ada(c_all, w_ada, b_ada):
    rows = c_all.shape[0]
    n = w_ada.shape[1]
    bn = 1024
    return pl.pallas_call(
        _ada_kernel,
        grid=(n // bn,),
        in_specs=[pl.BlockSpec((rows, D_MODEL), lambda j: (0, 0)),
                  pl.BlockSpec((D_MODEL, bn), lambda j: (0, j)),
                  pl.BlockSpec((1, bn), lambda j: (0, j))],
        out_specs=pl.BlockSpec((rows, bn), lambda j: (0, j)),
        out_shape=jax.ShapeDtypeStruct((rows, n), F32),
        name="ada",
    )(c_all, w_ada, b_ada.reshape(1, n))


def _hidden(x_ref, scale_ref, shift_ref):
    x = x_ref[...]
    ms = jnp.mean(x * x, axis=-1, keepdims=True)
    h = x * lax.rsqrt(ms + EPS) * (1.0 + scale_ref[...]) + shift_ref[...]
    return h.astype(BF16)


def _group_rms(z, g_ref):
    zz = z * z
    hi = zz.astype(BF16)
    lo = (zz - hi.astype(F32)).astype(BF16)
    ss = _dot(hi, g_ref[...]) + _dot(lo, g_ref[...])
    return z * lax.rsqrt(ss * (1.0 / DH_A) + EPS)


def _log_sigmoid(z):
    return jnp.minimum(z, 0.0) - jnp.log1p(jnp.exp(-jnp.abs(z)))


def _logf(hb, wf_ref, bf_ref):
    z = _dot(hb, wf_ref[...]) + bf_ref[...]
    lane = lax.broadcasted_iota(jnp.int32, z.shape, 1)
    return jnp.where(lane < H_B, _log_sigmoid(z), 0.0)


def _proj_prompt_kernel(nb, mshift_ref, x_ref, scale_ref, shift_ref, wqkt_ref, wn_ref, wf_ref, bf_ref,
                        gqk_ref, tri_ref,
                        kta_ref, va_ref, ktb_ref, vtbo_ref, lft_ref,
                        qta_ref, kaa_ref, vta_ref, qtb_ref, kab_ref, vtb_ref,
                        sga_ref, sgb_ref, sma_ref, smb_ref, carry_ref):
    tm = x_ref.shape[0]
    ib = pl.program_id(0) % nb
    hb = _hidden(x_ref, scale_ref, shift_ref)

    lf = _logf(hb, wf_ref, bf_ref)
    lft_ref[0] = lf.T[0:H_B, :]
    hi, mid, lo = _split3(lf)
    tri = tri_ref[...]
    f_local = _dot(tri, hi) + _dot(tri, mid) + _dot(tri, lo)

    @pl.when(ib == 0)
    def _():
        carry_ref[...] = jnp.zeros_like(carry_ref)

    f_nat = f_local + carry_ref[...]
    carry_ref[...] = f_nat[tm - 1:tm, :]
    f_t = f_nat.T[0:H_B, :]

    n_g = 4 * N_GROUPS
    n_qk = n_g * DH_A
    qkt = jnp.concatenate([_dot_nt(wqkt_ref[0:n_qk // 2, :], hb),
                           _dot_nt(wqkt_ref[n_qk // 2:n_qk, :], hb)], axis=0)
    g3 = qkt.reshape(n_g, DH_A, tm)
    ssq = jnp.sum(g3 * g3, axis=1, keepdims=True)
    g3 = (qkt * gqk_ref[...]).reshape(n_g, DH_A, tm) * lax.rsqrt(ssq * (1.0 / DH_A) + EPS)
    kta_ref[0] = g3[2 * N_GROUPS:3 * N_GROUPS].reshape(N_GROUPS * DH_A, tm)
    ktb_ref[0] = g3[3 * N_GROUPS:4 * N_GROUPS].reshape(N_GROUPS * DH_A, tm)

    r8 = lax.broadcasted_iota(jnp.int32, (8, tm), 0)
    pos = ib * tm + lax.broadcasted_iota(jnp.int32, (8, tm), 1)
    pos_hi = (pos >> 7).astype(F32)
    pos_lo = (pos & 127).astype(F32)
    zeros_tail = jnp.zeros((LANES - DH_A - 8, tm), F32)
    m_a = mshift_ref[0]
    m_b = mshift_ref[1]

    def operand(group, aug):
        return jnp.concatenate([g3[group], aug, zeros_tail], axis=0)

    k_aug_a = jnp.where(r8 == 2, pos_hi, jnp.where(r8 == 3, pos_lo,
              jnp.where((r8 <= 1) | (r8 == 4), 1.0, 0.0)))
    for g in range(N_GROUPS):
        slope = SLOPES[g % H_A]
        q_aug = jnp.where(r8 == 0, -slope * 128.0 * pos_hi,
                jnp.where(r8 == 1, -slope * pos_lo,
                jnp.where(r8 == 2, slope * 128.0,
                jnp.where(r8 == 3, slope,
                jnp.where(r8 == 4, -m_a, 0.0)))))
        qta_ref[0, g * LANES:(g + 1) * LANES, :] = operand(g, q_aug).astype(BF16)
        kaa_ref[0, :, g * LANES:(g + 1) * LANES] = operand(2 * N_GROUPS + g, k_aug_a).T.astype(BF16)
    ft_hi, ft_mid, ft_lo = _split3(f_t)
    for g in range(N_GROUPS):
        bc = lambda v: jnp.broadcast_to(v[g:g + 1, :].astype(F32), (8, tm))
        q_aug = jnp.where(r8 == 0, bc(ft_hi),
                jnp.where(r8 == 1, bc(ft_mid),
                jnp.where(r8 == 2, bc(ft_lo),
                jnp.where(r8 <= 5, -1.0,
                jnp.where(r8 == 6, -m_b, 0.0)))))
        k_aug = jnp.where(r8 <= 2, 1.0,
                jnp.where(r8 == 3, bc(ft_hi),
                jnp.where(r8 == 4, bc(ft_mid),
                jnp.where(r8 == 5, bc(ft_lo),
                jnp.where(r8 == 6, 1.0, 0.0)))))
        qtb_ref[0, g * LANES:(g + 1) * LANES, :] = operand(N_GROUPS + g, q_aug).astype(BF16)
        kab_ref[0, :, g * LANES:(g + 1) * LANES] = operand(3 * N_GROUPS + g, k_aug).T.astype(BF16)

    va = _dot(hb, wn_ref[:, 0:512])
    for h in range(H_A):
        va_ref[pl.ds(h, tm, stride=H_A), :] = va[:, h * DV_A:(h + 1) * DV_A]
    vta_ref[0, 0] = va.T.astype(BF16)
    vbt = _dot_nt(wqkt_ref[n_qk:, :], hb)
    vtbo_ref[0] = vbt
    vtb_ref[0, 0] = vbt.astype(BF16)

    sga_ref[...] = _silu(_dot(hb, wn_ref[:, 512:1024])).astype(BF16)
    sgb_ref[...] = _silu(_dot(hb, wn_ref[:, 1024:1536])).astype(BF16)
    sma_ref[...] = jax.nn.sigmoid(_dot(hb, wn_ref[:, 1536:2560])).astype(BF16)
    smb_ref[...] = jax.nn.sigmoid(_dot(hb, wn_ref[:, 2560:3584])).astype(BF16)


def _proj_prompt(x2, scale, shift, mshift, wts, batch, seq):
    rows = batch * seq
    nb = seq // TM
    nk = seq // VT_CHUNK
    row = lambda w: pl.BlockSpec((TM, w), lambda i: (i, 0))
    mod = pl.BlockSpec((None, 1, D_MODEL), lambda i: (i // nb, 0, 0))
    consts = [wts["wqkt"], wts["wn_prompt"], wts["wf"], wts["bf"], wts["gqk_col"], wts["tri"]]
    seq_t = lambda r: jax.ShapeDtypeStruct((batch, r, seq), F32)
    out_shape = [seq_t(512), jax.ShapeDtypeStruct((rows * H_A, DV_A), F32), seq_t(512), seq_t(512),
                 seq_t(H_B)] + [
        jax.ShapeDtypeStruct((batch, 1024, seq), BF16), jax.ShapeDtypeStruct((batch, seq, 1024), BF16),
        jax.ShapeDtypeStruct((batch, nk, W_A, VT_CHUNK), BF16),
        jax.ShapeDtypeStruct((batch, 1024, seq), BF16), jax.ShapeDtypeStruct((batch, seq, 1024), BF16),
        jax.ShapeDtypeStruct((batch, nk, W_B, VT_CHUNK), BF16),
        jax.ShapeDtypeStruct((rows, 512), BF16), jax.ShapeDtypeStruct((rows, 512), BF16),
        jax.ShapeDtypeStruct((rows, 1024), BF16), jax.ShapeDtypeStruct((rows, 1024), BF16)]
    qt_spec = pl.BlockSpec((1, 1024, TM), lambda i: (i // nb, 0, i % nb))
    k_spec = pl.BlockSpec((1, TM, 1024), lambda i: (i // nb, i % nb, 0))
    vt_spec = pl.BlockSpec((1, TM // VT_CHUNK, 512, VT_CHUNK), lambda i: (i // nb, i % nb, 0, 0))
    out_t = lambda r: pl.BlockSpec((1, r, TM), lambda i: (i // nb, 0, i % nb))
    out_specs = [out_t(512), pl.BlockSpec((TM * H_A, DV_A), lambda i: (i, 0)), out_t(512), out_t(512),
                 out_t(H_B)] + [
        qt_spec, k_spec, vt_spec, qt_spec, k_spec, vt_spec, row(512), row(512), row(1024), row(1024)]
    return pl.pallas_call(
        functools.partial(_proj_prompt_kernel, nb),
        grid=(rows // TM,),
        in_specs=[pl.BlockSpec(memory_space=pltpu.SMEM), row(D_MODEL), mod, mod]
                 + [_const_spec(c.shape) for c in consts],
        out_specs=out_specs,
        out_shape=out_shape,
        scratch_shapes=[pltpu.VMEM((1, LANES), F32)],
        compiler_params=pltpu.CompilerParams(dimension_semantics=("arbitrary",),
                                             vmem_limit_bytes=VMEM_LIMIT),
        name="proj_prompt",
    )(mshift, x2, scale, shift, *consts)


def _proj_sample_kernel(x_ref, scale_ref, shift_ref, wq_ref, wn_ref, wf_ref, bf_ref, gq_ref, gk_ref, g_ref,
                        qa_ref, qb_ref, ka_ref, va_ref, kb_ref, vb_ref, lf_ref,
                        sga_ref, sgb_ref, sma_ref, smb_ref):
    hb = _hidden(x_ref, scale_ref, shift_ref)
    lf_ref[...] = _logf(hb, wf_ref, bf_ref)[:, :H_B]
    qa_ref[...] = _group_rms(_dot(hb, wq_ref[:, 0:512]), g_ref) * gq_ref[:, 0:512]
    qb_ref[...] = _group_rms(_dot(hb, wq_ref[:, 512:1024]), g_ref) * gq_ref[:, 512:1024]
    ka_ref[...] = _group_rms(_dot(hb, wn_ref[:, 0:512]), g_ref) * gk_ref[:, 0:512]
    kb_ref[...] = _group_rms(_dot(hb, wn_ref[:, 512:1024]), g_ref) * gk_ref[:, 512:1024]
    va_ref[...] = _dot(hb, wn_ref[:, 1024:1536])
    vb_ref[...] = _dot(hb, wn_ref[:, 1536:2048])
    sga_ref[...] = _silu(_dot(hb, wn_ref[:, 2048:2560])).astype(BF16)
    sgb_ref[...] = _silu(_dot(hb, wn_ref[:, 2560:3072])).astype(BF16)
    sma_ref[...] = jax.nn.sigmoid(_dot(hb, wn_ref[:, 3072:4096])).astype(BF16)
    smb_ref[...] = jax.nn.sigmoid(_dot(hb, wn_ref[:, 4096:5120])).astype(BF16)


def _proj_sample(x2, scale_rows, shift_rows, wts):
    rows = x2.shape[0]
    row = lambda w: pl.BlockSpec((TM, w), lambda i: (i, 0))
    consts = [wts["wq"], wts["wn"], wts["wf"], wts["bf"], wts["gq_row"], wts["gk_row"], wts["gsum"]]
    out_shape = [jax.ShapeDtypeStruct((rows, 512), F32)] * 6 + [jax.ShapeDtypeStruct((rows, H_B), F32)] + [
        jax.ShapeDtypeStruct((rows, 512), BF16), jax.ShapeDtypeStruct((rows, 512), BF16),
        jax.ShapeDtypeStruct((rows, 1024), BF16), jax.ShapeDtypeStruct((rows, 1024), BF16)]
    out_specs = [row(512)] * 6 + [row(H_B), row(512), row(512), row(1024), row(1024)]
    return pl.pallas_call(
        _proj_sample_kernel,
        grid=(rows // TM,),
        in_specs=[row(D_MODEL), row(D_MODEL), row(D_MODEL)] + [_const_spec(c.shape) for c in consts],
        out_specs=out_specs,
        out_shape=out_shape,
        compiler_params=pltpu.CompilerParams(dimension_semantics=("arbitrary",),
                                             vmem_limit_bytes=VMEM_LIMIT),
        name="proj_sample",
    )(x2, scale_rows, shift_rows, *consts)


def _attn_kernel(shared_v, online, lam_ref, qt0_ref, qt1_ref, k0_ref, k1_ref, vt_ref, o_ref,
                 acc_ref, l_ref, p_ref, p2_ref, pd_ref):
    i = pl.program_id(2)
    qts = (qt0_ref[0], qt1_ref[0])
    k_refs = (k0_ref, k1_ref)
    dv = acc_ref.shape[1]
    n_chunks = TK // VT_CHUNK
    acc_ref[...] = jnp.zeros_like(acc_ref)

    def pv(u, chunk, p):
        blk = vt_ref[0, chunk]
        v = blk if shared_v else blk[u * dv:(u + 1) * dv, :]
        return _dot(v, p)

    def scores_to_probs(j, masked, dst_ref):
        for u in range(2):
            kblk = k_refs[u][0, pl.ds(pl.multiple_of(j * TK, TK), TK), :]
            s = _dot(kblk, qts[u])
            if masked:
                kpos = lax.broadcasted_iota(jnp.int32, s.shape, 0)
                qpos = lax.broadcasted_iota(jnp.int32, s.shape, 1)
                s = jnp.where(kpos <= qpos, s, NEG)
            p = jnp.exp(s)
            l_ref[u] += jnp.sum(p.reshape(TK // 8, 8, TQ), axis=0)
            dst_ref[u] = p.astype(BF16)

    def probs_times_values(j, src_ref):
        for u in range(2):
            upd = None
            for c in range(n_chunks):
                d = pv(u, j * n_chunks + c, src_ref[u, c * VT_CHUNK:(c + 1) * VT_CHUNK, :])
                upd = d if upd is None else upd + d
            acc_ref[u] += upd

    def step_online(j, carry, masked):
        out = []
        for u in range(2):
            kblk = k_refs[u][0, pl.ds(pl.multiple_of(j * TK, TK), TK), :]
            s = _dot(kblk, qts[u])
            if masked:
                kpos = lax.broadcasted_iota(jnp.int32, s.shape, 0)
                qpos = lax.broadcasted_iota(jnp.int32, s.shape, 1)
                s = jnp.where(kpos <= qpos, s, NEG)
            m, l = carry[u]
            m_new = jnp.maximum(m, jnp.max(s, axis=0, keepdims=True))
            alpha = jnp.exp(m - m_new)
            p = jnp.exp(s - m_new)
            l = alpha * l + jnp.sum(p, axis=0, keepdims=True)
            p = p.astype(BF16)
            upd = None
            for c in range(n_chunks):
                d = pv(u, j * n_chunks + c, p[c * VT_CHUNK:(c + 1) * VT_CHUNK, :])
                upd = d if upd is None else upd + d
            acc_ref[u] = alpha * acc_ref[u] + upd
            out.append((m_new, l))
        return tuple(out)

    if online:
        init = tuple((jnp.full((1, TQ), NEG, F32), jnp.zeros((1, TQ), F32)) for _ in range(2))
        carry = lax.fori_loop(0, i, lambda j, c: step_online(j, c, False), init)
        carry = step_online(i, carry, True)
        l0, l1 = carry[0][1], carry[1][1]
    else:
        l_ref[...] = jnp.zeros_like(l_ref)

        @pl.when(i == 0)
        def _():
            scores_to_probs(0, True, pd_ref)
            probs_times_values(0, pd_ref)

        @pl.when(i > 0)
        def _():
            scores_to_probs(0, False, p_ref)

            def body(jj, _):
                j = 2 * jj + 1
                scores_to_probs(j, False, p2_ref)
                probs_times_values(j - 1, p_ref)
                scores_to_probs(j + 1, False, p_ref)
                probs_times_values(j, p2_ref)
                return 0

            lax.fori_loop(0, (i - 1) // 2, body, 0)

            @pl.when(i % 2 == 0)
            def _():
                scores_to_probs(i - 1, False, p2_ref)
                probs_times_values(i - 2, p_ref)
                scores_to_probs(i, True, pd_ref)
                probs_times_values(i - 1, p2_ref)
                probs_times_values(i, pd_ref)

            @pl.when(i % 2 == 1)
            def _():
                scores_to_probs(i, True, pd_ref)
                probs_times_values(i - 1, p_ref)
                probs_times_values(i, pd_ref)

        l0, l1 = (jnp.sum(l_ref[u], axis=0, keepdims=True) for u in range(2))

    o0 = acc_ref[0] * (1.0 / l0)
    o1 = acc_ref[1] * (1.0 / l1)
    o = o0 - lam_ref[0] * o1 if shared_v else jnp.concatenate([o0, o1], axis=0)
    o_ref[0] = o.T


def _attention(qt, kaug, vt, lam, shared_v, online):
    batch, _, seq = qt.shape
    n_steps = 4
    if shared_v:
        g0 = lambda h: h
        g1 = lambda h: H_A + h
    else:
        g0 = lambda h: 2 * h
        g1 = lambda h: 2 * h + 1
    dv = DV_A if shared_v else DH_B
    qspec = lambda g: pl.BlockSpec((1, LANES, TQ), lambda b, h, i: (b, g(h), i))
    kspec = lambda g: pl.BlockSpec((1, seq, LANES), lambda b, h, i: (b, 0, g(h)))
    return pl.pallas_call(
        functools.partial(_attn_kernel, shared_v, online),
        grid=(batch, n_steps, seq // TQ),
        in_specs=[pl.BlockSpec(memory_space=pltpu.SMEM), qspec(g0), qspec(g1), kspec(g0), kspec(g1),
                  pl.BlockSpec((1, seq // VT_CHUNK, LANES, VT_CHUNK), lambda b, h, i: (b, 0, h, 0))],
        out_specs=pl.BlockSpec((1, TQ, LANES), lambda b, h, i: (b, i, h)),
        out_shape=jax.ShapeDtypeStruct((batch, seq, 512), F32),
        scratch_shapes=[pltpu.VMEM((2, dv, TQ), F32), pltpu.VMEM((2, 8, TQ), F32),
                        pltpu.VMEM((2, TK, TQ), BF16), pltpu.VMEM((2, TK, TQ), BF16),
                        pltpu.VMEM((2, TK, TQ), BF16)],
        compiler_params=pltpu.CompilerParams(
            dimension_semantics=("arbitrary", "arbitrary", "arbitrary"), vmem_limit_bytes=VMEM_LIMIT),
        name=("attn_a" if shared_v else "attn_b") + ("_online" if online else ""),
    )(lam, qt, qt, kaug, kaug, vt)


def _decode_kernel(past, pps, pt_ref, lam_ref, qa_ref, qb_ref,
                   kta_hbm, va_hbm, ktb_hbm, vtb_hbm, lft_hbm,
                   kan_ref, van_ref, kbn_ref, vbn_ref, lfn_ref, upper_ref, oa_ref, ob_ref,
                   qbd_ref, m_ref, l_ref, acca_ref, accb_ref, fc_ref,
                   kta_buf, va_buf, ktb_buf, vtb_buf, lft_buf, sem):
    step_i = pl.program_id(1)
    n_steps = pl.num_programs(1)
    g = pl.program_id(0) * n_steps + step_i
    slot = g % 2
    pools = ((kta_hbm, kta_buf), (va_hbm, va_buf), (ktb_hbm, ktb_buf), (vtb_hbm, vtb_buf), (lft_hbm, lft_buf))

    def page_copies(gg, dst_slot):
        out = []
        for j in range(pps):
            pg = pt_ref[gg * pps + j]
            for k, (hbm, buf) in enumerate(pools):
                out.append(pltpu.make_async_copy(hbm.at[pg], buf.at[dst_slot, j], sem.at[dst_slot, k]))
        return out

    @pl.when(g == 0)
    def _():
        for c in page_copies(0, 0):
            c.start()

    @pl.when(g + 1 < pl.num_programs(0) * n_steps)
    def _():
        for c in page_copies(g + 1, 1 - slot):
            c.start()

    for c in page_copies(g, slot):
        c.wait()
    kta_refs, va_refs, ktb_refs, vtb_refs, lft_refs = (
        [buf.at[slot, j] for j in range(pps)] for _, buf in pools)
    rows = 32
    n_tok = 4
    new_pad = kan_ref.shape[1]
    page = LANES

    def row_ids(shape):
        r = lax.broadcasted_iota(jnp.int32, shape, 0)
        return ((r & 3, ((r >> 2) & 1) * H_A + (r >> 3), r >> 3),
                (r >> 3, r & 7, r & 7))

    def row_slope(head):
        return jnp.where(head == 0, SLOPES[0], jnp.where(head == 1, SLOPES[1],
               jnp.where(head == 2, SLOPES[2], SLOPES[3])))

    @pl.when(step_i == 0)
    def _():
        col_group = lax.broadcasted_iota(jnp.int32, (rows, 512), 1) >> 6
        for u, q_ref in enumerate((qa_ref, qb_ref)):
            tok, group, _ = row_ids((rows, 512))[u]
            qbd = jnp.zeros((rows, 512), F32)
            for t in range(n_tok):
                q_row = jnp.broadcast_to(q_ref[0, t:t + 1, :], (rows, 512))
                qbd = jnp.where((tok == t) & (group == col_group), q_row, qbd)
            qbd_ref[u] = qbd.astype(BF16)
        m_ref[...] = jnp.full(m_ref.shape, NEG, F32)
        l_ref[...] = jnp.zeros_like(l_ref)
        acca_ref[...] = jnp.zeros_like(acca_ref)
        accb_ref[...] = jnp.zeros_like(accb_ref)
        fc_ref[...] = jnp.zeros_like(fc_ref)

    def softmax_update(u, s):
        m_prev = m_ref[u][:, 0:1]
        l_prev = l_ref[u][:, 0:1]
        m_new = jnp.maximum(m_prev, jnp.max(s, axis=1, keepdims=True))
        alpha = jnp.exp(m_prev - m_new)
        pr = jnp.exp(s - m_new)
        l_new = alpha * l_prev + jnp.sum(pr, axis=1, keepdims=True)
        m_ref[u] = jnp.broadcast_to(m_new, (rows, LANES))
        l_ref[u] = jnp.broadcast_to(l_new, (rows, LANES))
        return alpha, pr

    is_last = step_i == n_steps - 1
    width = pps * page
    lane = lax.broadcasted_iota(jnp.int32, (rows, width), 1)
    tn = lax.broadcasted_iota(jnp.int32, (rows, page), 1)
    (tok_a, _, head_a), (tok_b, _, _) = row_ids((rows, page))
    pad_rows = lambda ref: jnp.concatenate(
        [ref[0], jnp.zeros((page - new_pad, 512), F32)], axis=0).astype(BF16)

    def valid(tok):
        return is_last & (tn <= tok) & (tn < n_tok)

    dist = (past + row_ids((rows, width))[0][0] - (step_i * width + lane)).astype(F32)
    s_a = _dot(qbd_ref[0], jnp.concatenate([r[...].astype(BF16) for r in kta_refs], axis=1))
    s_a = s_a - row_slope(row_ids((rows, width))[0][2]) * dist
    s_an = _dot_nt(qbd_ref[0], pad_rows(kan_ref)) - row_slope(head_a) * (tok_a - tn).astype(F32)
    alpha, pr = softmax_update(0, jnp.concatenate([s_a, jnp.where(valid(tok_a), s_an, NEG)], axis=1))
    pr = pr.astype(BF16)
    v_all = jnp.concatenate(
        [jnp.concatenate([r[pl.ds(h, page, stride=H_A), :].astype(BF16) for h in range(H_A)], axis=1)
         for r in va_refs], axis=0)
    acca_ref[...] = (alpha * acca_ref[...] + _dot(pr[:, 0:width], v_all)
                     + _dot(pr[:, width:], pad_rows(van_ref)))

    parts = []
    for j in range(pps):
        parts.extend(v.astype(F32) for v in _split3(lft_refs[j][...]))
        parts.append(jnp.zeros((8, page), F32))
    cs = _dot(jnp.concatenate(parts, axis=0).astype(BF16), upper_ref[...])
    f_carry = fc_ref[...][:, 0:1]
    f_pages = []
    for j in range(pps):
        local = cs[32 * j:32 * j + 8] + cs[32 * j + 8:32 * j + 16] + cs[32 * j + 16:32 * j + 24]
        f_page = local + f_carry
        f_carry = f_carry + local[:, page - 1:page]
        f_pages.append(jnp.concatenate([f_page] * n_tok, axis=0))
    fc_ref[...] = jnp.broadcast_to(f_carry, (8, LANES))
    lfn = lfn_ref[0]
    t8 = lax.broadcasted_iota(jnp.int32, (8, page), 1)
    f_new = jnp.broadcast_to(f_carry, (8, page))
    for t in range(n_tok):
        f_new = f_new + jnp.where(t8 >= t, jnp.broadcast_to(lfn[:, t:t + 1], (8, page)), 0.0)
    s_b = _dot(qbd_ref[1], jnp.concatenate([r[...].astype(BF16) for r in ktb_refs], axis=1))
    s_b = s_b - jnp.concatenate(f_pages, axis=1)
    s_bn = _dot_nt(qbd_ref[1], pad_rows(kbn_ref)) - jnp.concatenate([f_new] * n_tok, axis=0)
    alpha, pr = softmax_update(1, jnp.concatenate([s_b, jnp.where(valid(tok_b), s_bn, NEG)], axis=1))
    pr = pr.astype(BF16)
    vt_all = jnp.concatenate([r[...].astype(BF16) for r in vtb_refs], axis=1)
    accb_ref[...] = (alpha * accb_ref[...] + _dot_nt(pr[:, 0:width], vt_all)
                     + _dot(pr[:, width:], pad_rows(vbn_ref)))

    @pl.when(is_last)
    def _():
        on_a = acca_ref[...] * (1.0 / l_ref[0][:, 0:1])
        for h in range(H_A):
            blk = on_a[8 * h:8 * h + 8, h * DV_A:(h + 1) * DV_A]
            oa_ref[0, :, h * DV_A:(h + 1) * DV_A] = blk[0:n_tok] - lam_ref[0] * blk[n_tok:2 * n_tok]
        head_b = row_ids((rows, 512))[1][2]
        col_head = lax.broadcasted_iota(jnp.int32, (rows, 512), 1) >> 6
        on_b = jnp.where(head_b == col_head, accb_ref[...] * (1.0 / l_ref[1][:, 0:1]), 0.0)
        for t in range(n_tok):
            ob_ref[0, t:t + 1, :] = jnp.sum(on_b[8 * t:8 * t + 8], axis=0, keepdims=True)


def _decode(page_table, lam, qa, qb, pools, news, lfn_t, upper):
    n_seq, n_pages = page_table.shape
    page = pools[0].shape[2]
    pps = PAGES_PER_STEP
    new_pad = news[0].shape[1]
    past = n_pages * page
    seq_spec = lambda r: pl.BlockSpec((1, r, 512), lambda n, s, pt: (n, 0, 0))
    page_bufs = [pltpu.VMEM((2, pps) + pool.shape[1:], F32) for pool in pools]
    grid_spec = pltpu.PrefetchScalarGridSpec(
        num_scalar_prefetch=1,
        grid=(n_seq, n_pages // pps),
        in_specs=[pl.BlockSpec(memory_space=pltpu.SMEM), seq_spec(4), seq_spec(4)]
                 + [pl.BlockSpec(memory_space=pl.ANY)] * len(pools)
                 + [seq_spec(new_pad)] * 4
                 + [pl.BlockSpec((1, H_B, page), lambda n, s, pt: (n, 0, 0)),
                    pl.BlockSpec((page, page), lambda n, s, pt: (0, 0))],
        out_specs=[seq_spec(4), seq_spec(4)],
        scratch_shapes=[pltpu.VMEM((2, 32, 512), BF16), pltpu.VMEM((2, 32, LANES), F32),
                        pltpu.VMEM((2, 32, LANES), F32), pltpu.VMEM((32, 512), F32),
                        pltpu.VMEM((32, 512), F32), pltpu.VMEM((H_B, LANES), F32)]
                       + page_bufs + [pltpu.SemaphoreType.DMA((2, len(pools)))],
    )
    return pl.pallas_call(
        functools.partial(_decode_kernel, past, pps),
        grid_spec=grid_spec,
        out_shape=[jax.ShapeDtypeStruct((n_seq, 4, 512), F32)] * 2,
        compiler_params=pltpu.CompilerParams(dimension_semantics=("arbitrary", "arbitrary"),
                                             vmem_limit_bytes=VMEM_LIMIT),
        name="decode",
    )(page_table.reshape(-1), lam, qa, qb, *pools, *news, lfn_t, upper)


def _merge_kernel(x_ref, oa_ref, ob_ref, sga_ref, sgb_ref, sma_ref, smb_ref, gres_ref, gsub_ref,
                  woa_ref, wob_ref, wo_ref, y_ref):
    oa = oa_ref[...]
    heads = []
    for h in range(H_A):
        oh = oa[:, h * DV_A:(h + 1) * DV_A]
        heads.append(oh * lax.rsqrt(jnp.mean(oh * oh, axis=-1, keepdims=True) + EPS))
    oa = jnp.concatenate(heads, axis=1) * gsub_ref[...]
    ya = _dot((oa * sga_ref[...].astype(F32)).astype(BF16), woa_ref[...])
    yb = _dot((ob_ref[...] * sgb_ref[...].astype(F32)).astype(BF16), wob_ref[...])
    mix = sma_ref[...].astype(F32) * ya + smb_ref[...].astype(F32) * yb
    y = _dot(mix.astype(BF16), wo_ref[...])
    y_ref[...] = x_ref[...] + gres_ref[...] * y


def _merge(x2, oa, ob, sga, sgb, sma, smb, gres, gres_spec, wts):
    rows = x2.shape[0]
    row = lambda w: pl.BlockSpec((TM_MERGE, w), lambda i: (i, 0))
    consts = [wts["gsub"], wts["woa"], wts["wob"], wts["wo"]]
    return pl.pallas_call(
        _merge_kernel,
        grid=(rows // TM_MERGE,),
        in_specs=[row(D_MODEL), row(512), row(512), row(512), row(512), row(1024), row(1024), gres_spec]
                 + [_const_spec(c.shape) for c in consts],
        out_specs=row(D_MODEL),
        out_shape=jax.ShapeDtypeStruct((rows, D_MODEL), F32),
        compiler_params=pltpu.CompilerParams(dimension_semantics=("arbitrary",),
                                             vmem_limit_bytes=VMEM_LIMIT),
        name="merge",
    )(x2, oa, ob, sga, sgb, sma, smb, gres, *consts)


def _ones_constants():
    gsum = np.kron(np.eye(N_GROUPS), np.ones((DH_A, DH_A)))
    tri = np.tril(np.ones((TM, TM)))
    upper = np.triu(np.ones((LANES, LANES)))
    bf = lambda a: jnp.asarray(a, BF16)
    return dict(gsum=bf(gsum), tri=bf(tri), upper=bf(upper))


def _bf16_ceil(x):
    y = x.astype(BF16).astype(F32)
    return jnp.where(y < x, y * (1.0 + 2.0 ** -7), y)


def kernel(x_prompt, x_sample, cache_a_k, cache_a_v, cache_b_k, cache_b_v, cache_b_logf, page_table,
           c_prompt, c_sample, w_ada, b_ada, w_in, b_f, g_q_a, g_k_a, g_q_b, g_k_b,
           lambda_q1, lambda_k1, lambda_q2, lambda_k2, g_sub_a, w_out_a, w_out_b, w_o):
    assert w_ada.shape[0] == 1, "single-layer step"
    batch, seq, _ = x_prompt.shape
    n_seq, n_tok, _ = x_sample.shape
    n_pool, page = cache_a_k.shape[1], cache_a_k.shape[2]
    layer = 0
    lam_init = 0.8 - 0.6 * math.exp(-0.3 * layer)
    lam = (jnp.exp(jnp.sum(lambda_q1[layer] * lambda_k1[layer]))
           - jnp.exp(jnp.sum(lambda_q2[layer] * lambda_k2[layer])) + lam_init).reshape(1).astype(F32)

    w = w_in[layer]
    sec = np.cumsum((0, 512, 512, 512, 512, 512, 512, 512, H_B, 512, 1024, 1024))
    cols = lambda k: w[:, sec[k]:sec[k + 1]]
    wq = jnp.concatenate([cols(0), cols(4)], axis=1).astype(BF16)
    wn = jnp.concatenate([cols(1), cols(5), cols(2), cols(6), cols(3), cols(8), cols(9), cols(10)],
                         axis=1).astype(BF16)
    wf = jnp.pad(cols(7), ((0, 0), (0, LANES - H_B))).astype(BF16)
    gq = jnp.concatenate([jnp.tile(g_q_a[layer], N_GROUPS) * DH_A ** -0.5,
                          jnp.tile(g_q_b[layer], N_GROUPS) * DH_B ** -0.5])
    gk = jnp.concatenate([jnp.tile(g_k_a[layer], N_GROUPS), jnp.tile(g_k_b[layer], N_GROUPS)])
    wts = dict(_ones_constants(),
               wq=wq, wqkt=jnp.concatenate([wq, wn[:, 0:1024], wn[:, 1536:2048]], axis=1).T,
               wn=wn, wn_prompt=jnp.concatenate([wn[:, 1024:1536], wn[:, 2048:]], axis=1),

               wf=wf, bf=jnp.pad(b_f[layer], (0, LANES - H_B)).reshape(1, LANES),
               gqk_col=jnp.concatenate([gq, gk]).reshape(-1, 1), gq_row=gq.reshape(1, -1), gk_row=gk.reshape(1, -1),
               gsub=(jnp.tile(g_sub_a[layer], H_A) * (1.0 - lam_init)).reshape(1, W_A),
               woa=w_out_a[layer].astype(BF16), wob=w_out_b[layer].astype(BF16), wo=w_o[layer].astype(BF16))
    bound = lambda gq_, gk_: _bf16_ceil(8.1 * jnp.max(jnp.abs(gq_ * gk_)))
    mshift = jnp.stack([bound(g_q_a[layer], g_k_a[layer]), bound(g_q_b[layer], g_k_b[layer])]).astype(F32)

    n_c = batch + n_seq
    c_all = jnp.pad(jnp.concatenate([c_prompt, c_sample], axis=0), ((0, -n_c % 8), (0, 0)))
    mod = _ada(c_all, w_ada[layer], b_ada[layer])
    shift, scale, gres = (mod[:, k * D_MODEL:(k + 1) * D_MODEL] for k in range(3))

    xp2 = x_prompt.reshape(batch * seq, D_MODEL)
    p3 = lambda a: a[:batch].reshape(batch, 1, D_MODEL)
    (kta, va, ktb, vtbo, lft, qta, kaa, vta, qtb, kab, vtb, sga, sgb, sma, smb) = _proj_prompt(
        xp2, p3(scale), p3(shift), mshift, wts, batch, seq)
    online = jnp.max(mshift) > FAST_PATH_MAX_BOUND
    attend = lambda *a: lax.cond(online, lambda: _attention(*a, True), lambda: _attention(*a, False))
    oa = attend(qta, kaa, vta, lam, True).reshape(batch * seq, W_A)
    ob = attend(qtb, kab, vtb, lam, False).reshape(batch * seq, W_B)
    nb = seq // TM_MERGE
    gres_p_spec = pl.BlockSpec((None, 1, D_MODEL), lambda i: (i // nb, 0, 0))
    yp = _merge(xp2, oa, ob, sga, sgb, sma, smb, p3(gres), gres_p_spec, wts).reshape(batch, seq, D_MODEL)

    xs2 = x_sample.reshape(n_seq * n_tok, D_MODEL)
    rep = lambda a: jnp.repeat(a[batch:n_c], n_tok, axis=0)
    (qa_s, qb_s, ka_s, va_s, kb_s, vb_s, lf_s, sga_s, sgb_s, sma_s, smb_s) = _proj_sample(
        xs2, rep(scale), rep(shift), wts)
    new_pad = 16
    seq3 = lambda a: a.reshape(n_seq, n_tok, 512)
    padn = lambda a: jnp.pad(seq3(a), ((0, 0), (0, new_pad - n_tok), (0, 0)))
    pools = [jnp.transpose(cache_a_k[layer], (0, 2, 3, 4, 1)).reshape(n_pool, 512, page),
             cache_a_v[layer].reshape(n_pool, page * H_A, DV_A),
             jnp.transpose(cache_b_k[layer], (0, 2, 3, 1)).reshape(n_pool, 512, page),
             jnp.transpose(cache_b_v[layer], (0, 2, 3, 1)).reshape(n_pool, 512, page),
             jnp.swapaxes(cache_b_logf[layer], 1, 2)]
    lfn_t = jnp.pad(jnp.swapaxes(lf_s.reshape(n_seq, n_tok, H_B), 1, 2), ((0, 0), (0, 0), (0, page - n_tok)))
    oa_s, ob_s = _decode(page_table, lam, seq3(qa_s), seq3(qb_s), pools,
                         [padn(ka_s), padn(va_s), padn(kb_s), padn(vb_s)], lfn_t, wts["upper"])
    gres_s_spec = pl.BlockSpec((TM_MERGE, D_MODEL), lambda i: (i, 0))
    ys = _merge(xs2, oa_s.reshape(-1, W_A), ob_s.reshape(-1, W_B), sga_s, sgb_s, sma_s, smb_s,
                rep(gres), gres_s_spec, wts).reshape(n_seq, n_tok, D_MODEL)

    return (yp, ys,
            jnp.moveaxis(kta.reshape(1, batch, 2, H_A, DH_A, seq), -1, 2), va.reshape(1, batch, seq, H_A, DV_A),
            jnp.moveaxis(ktb.reshape(1, batch, H_B, DH_B, seq), -1, 2),
            jnp.moveaxis(vtbo.reshape(1, batch, H_B, DH_B, seq), -1, 2),
            jnp.moveaxis(lft.reshape(1, batch, H_B, seq), -1, 2),
            ka_s.reshape(1, n_seq, n_tok, 2, H_A, DH_A), va_s.reshape(1, n_seq, n_tok, H_A, DV_A),
            kb_s.reshape(1, n_seq, n_tok, H_B, DH_B), vb_s.reshape(1, n_seq, n_tok, H_B, DH_B),
            lf_s.reshape(1, n_seq, n_tok, H_B))
```

```python
import functools
import math

import numpy as np
import jax
import jax.numpy as jnp
from jax import lax
from jax.experimental import pallas as pl
from jax.experimental.pallas import tpu as pltpu

F32 = jnp.float32
BF16 = jnp.bfloat16

D_MODEL = 1024
H_A, DH_A, DV_A = 4, 64, 128
H_B, DH_B = 8, 64
W_A = H_A * DV_A
W_B = H_B * DH_B
N_GROUPS = 8
EPS = 1e-6
NEG = -1e30
SLOPES = tuple(2.0 ** (-8.0 * (h + 1) / H_A) for h in range(H_A))
LANES = 128

TM = 256
TM_MERGE = 512
TQ = 512
TK = 512
VT_CHUNK = TM
PAGES_PER_STEP = 8
FAST_PATH_MAX_BOUND = 30.0
VMEM_LIMIT = 56 * 1024 * 1024


def _dot(a, b):
    return jnp.dot(a, b, preferred_element_type=F32)


def _dot_nt(a, b):
    return lax.dot_general(a, b, (((1,), (1,)), ((), ())), preferred_element_type=F32)


def _split3(x):
    hi = x.astype(BF16)
    r1 = x - hi.astype(F32)
    mid = r1.astype(BF16)
    lo = (r1 - mid.astype(F32)).astype(BF16)
    return hi, mid, lo


def _silu(x):
    return x * jax.nn.sigmoid(x)


def _const_spec(shape):
    nd = len(shape)
    return pl.BlockSpec(shape, lambda *_: (0,) * nd, pipeline_mode=pl.Buffered(1))


def _ada_kernel(c_ref, w_ref, b_ref, o_ref):
    c = c_ref[...]
    o_ref[...] = _dot(_silu(c).astype(BF16), w_ref[...].astype(BF16)) + b_ref[...]


def _ada(c_all, w_ada, b_ada):
    rows = c_all.shape[0]
    n = w_ada.shape[1]
    bn = 1024
    return pl.pallas_call(
        _ada_kernel,
        grid=(n // bn,),
        in_specs=[pl.BlockSpec((rows, D_MODEL), lambda j: (0, 0)),
                  pl.BlockSpec((D_MODEL, bn), lambda j: (0, j)),
                  pl.BlockSpec((1, bn), lambda j: (0, j))],
        out_specs=pl.BlockSpec((rows, bn), lambda j: (0, j)),
        out_shape=jax.ShapeDtypeStruct((rows, n), F32),
        name="ada",
    )(c_all, w_ada, b_ada.reshape(1, n))


def _hidden(x_ref, scale_ref, shift_ref):
    x = x_ref[...]
    ms = jnp.mean(x * x, axis=-1, keepdims=True)
    h = x * lax.rsqrt(ms + EPS) * (1.0 + scale_ref[...]) + shift_ref[...]
    return h.astype(BF16)


def _group_rms(z, g_ref):
    zz = z * z
    hi = zz.astype(BF16)
    lo = (zz - hi.astype(F32)).astype(BF16)
    ss = _dot(hi, g_ref[...]) + _dot(lo, g_ref[...])
    return z * lax.rsqrt(ss * (1.0 / DH_A) + EPS)


def _log_sigmoid(z):
    return jnp.minimum(z, 0.0) - jnp.log1p(jnp.exp(-jnp.abs(z)))


def _logf(hb, wf_ref, bf_ref):
    z = _dot(hb, wf_ref[...]) + bf_ref[...]
    lane = lax.broadcasted_iota(jnp.int32, z.shape, 1)
    return jnp.where(lane < H_B, _log_sigmoid(z), 0.0)


def _proj_prompt_kernel(nb, mshift_ref, x_ref, scale_ref, shift_ref, wqkt_ref, wn_ref, wf_ref, bf_ref,
                        gqk_ref, tri_ref,
                        kta_ref, va_ref, ktb_ref, vtbo_ref, lft_ref,
                        qta_ref, kaa_ref, vta_ref, qtb_ref, kab_ref, vtb_ref,
                        sga_ref, sgb_ref, sma_ref, smb_ref, carry_ref):
    tm = x_ref.shape[0]
    ib = pl.program_id(0) % nb
    hb = _hidden(x_ref, scale_ref, shift_ref)

    lf = _logf(hb, wf_ref, bf_ref)
    lft_ref[0] = lf.T[0:H_B, :]
    hi, mid, lo = _split3(lf)
    tri = tri_ref[...]
    f_local = _dot(tri, hi) + _dot(tri, mid) + _dot(tri, lo)

    @pl.when(ib == 0)
    def _():
        carry_ref[...] = jnp.zeros_like(carry_ref)

    f_nat = f_local + carry_ref[...]
    carry_ref[...] = f_nat[tm - 1:tm, :]
    f_t = f_nat.T[0:H_B, :]

    n_g = 4 * N_GROUPS
    n_qk = n_g * DH_A
    qkt = jnp.concatenate([_dot_nt(wqkt_ref[0:n_qk // 2, :], hb),
                           _dot_nt(wqkt_ref[n_qk // 2:n_qk, :], hb)], axis=0)
    g3 = qkt.reshape(n_g, DH_A, tm)
    ssq = jnp.sum(g3 * g3, axis=1, keepdims=True)
    g3 = (qkt * gqk_ref[...]).reshape(n_g, DH_A, tm) * lax.rsqrt(ssq * (1.0 / DH_A) + EPS)
    kta_ref[0] = g3[2 * N_GROUPS:3 * N_GROUPS].reshape(N_GROUPS * DH_A, tm)
    ktb_ref[0] = g3[3 * N_GROUPS:4 * N_GROUPS].reshape(N_GROUPS * DH_A, tm)

    r8 = lax.broadcasted_iota(jnp.int32, (8, tm), 0)
    pos = ib * tm + lax.broadcasted_iota(jnp.int32, (8, tm), 1)
    pos_hi = (pos >> 7).astype(F32)
    pos_lo = (pos & 127).astype(F32)
    zeros_tail = jnp.zeros((LANES - DH_A - 8, tm), F32)
    m_a = mshift_ref[0]
    m_b = mshift_ref[1]

    def operand(group, aug):
        return jnp.concatenate([g3[group], aug, zeros_tail], axis=0)

    k_aug_a = jnp.where(r8 == 2, pos_hi, jnp.where(r8 == 3, pos_lo,
              jnp.where((r8 <= 1) | (r8 == 4), 1.0, 0.0)))
    for g in range(N_GROUPS):
        slope = SLOPES[g % H_A]
        q_aug = jnp.where(r8 == 0, -slope * 128.0 * pos_hi,
                jnp.where(r8 == 1, -slope * pos_lo,
                jnp.where(r8 == 2, slope * 128.0,
                jnp.where(r8 == 3, slope,
                jnp.where(r8 == 4, -m_a, 0.0)))))
        qta_ref[0, g * LANES:(g + 1) * LANES, :] = operand(g, q_aug).astype(BF16)
        kaa_ref[0, :, g * LANES:(g + 1) * LANES] = operand(2 * N_GROUPS + g, k_aug_a).T.astype(BF16)
    ft_hi, ft_mid, ft_lo = _split3(f_t)
    for g in range(N_GROUPS):
        bc = lambda v: jnp.broadcast_to(v[g:g + 1, :].astype(F32), (8, tm))
        q_aug = jnp.where(r8 == 0, bc(ft_hi),
                jnp.where(r8 == 1, bc(ft_mid),
                jnp.where(r8 == 2, bc(ft_lo),
                jnp.where(r8 <= 5, -1.0,
                jnp.where(r8 == 6, -m_b, 0.0)))))
        k_aug = jnp.where(r8 <= 2, 1.0,
                jnp.where(r8 == 3, bc(ft_hi),
                jnp.where(r8 == 4, bc(ft_mid),
                jnp.where(r8 == 5, bc(ft_lo),
                jnp.where(r8 == 6, 1.0, 0.0)))))
        qtb_ref[0, g * LANES:(g + 1) * LANES, :] = operand(N_GROUPS + g, q_aug).astype(BF16)
        kab_ref[0, :, g * LANES:(g + 1) * LANES] = operand(3 * N_GROUPS + g, k_aug).T.astype(BF16)

    va = _dot(hb, wn_ref[:, 0:512])
    for h in range(H_A):
        va_ref[pl.ds(h, tm, stride=H_A), :] = va[:, h * DV_A:(h + 1) * DV_A]
    vta_ref[0, 0] = va.T.astype(BF16)
    vbt = _dot_nt(wqkt_ref[n_qk:, :], hb)
    vtbo_ref[0] = vbt
    vtb_ref[0, 0] = vbt.astype(BF16)

    sga_ref[...] = _silu(_dot(hb, wn_ref[:, 512:1024])).astype(BF16)
    sgb_ref[...] = _silu(_dot(hb, wn_ref[:, 1024:1536])).astype(BF16)
    sma_ref[...] = jax.nn.sigmoid(_dot(hb, wn_ref[:, 1536:2560])).astype(BF16)
    smb_ref[...] = jax.nn.sigmoid(_dot(hb, wn_ref[:, 2560:3584])).astype(BF16)


def _proj_prompt(x2, scale, shift, mshift, wts, batch, seq):
    rows = batch * seq
    nb = seq // TM
    nk = seq // VT_CHUNK
    row = lambda w: pl.BlockSpec((TM, w), lambda i: (i, 0))
    mod = pl.BlockSpec((None, 1, D_MODEL), lambda i: (i // nb, 0, 0))
    consts = [wts["wqkt"], wts["wn_prompt"], wts["wf"], wts["bf"], wts["gqk_col"], wts["tri"]]
    seq_t = lambda r: jax.ShapeDtypeStruct((batch, r, seq), F32)
    out_shape = [seq_t(512), jax.ShapeDtypeStruct((rows * H_A, DV_A), F32), seq_t(512), seq_t(512),
                 seq_t(H_B)] + [
        jax.ShapeDtypeStruct((batch, 1024, seq), BF16), jax.ShapeDtypeStruct((batch, seq, 1024), BF16),
        jax.ShapeDtypeStruct((batch, nk, W_A, VT_CHUNK), BF16),
        jax.ShapeDtypeStruct((batch, 1024, seq), BF16), jax.ShapeDtypeStruct((batch, seq, 1024), BF16),
        jax.ShapeDtypeStruct((batch, nk, W_B, VT_CHUNK), BF16),
        jax.ShapeDtypeStruct((rows, 512), BF16), jax.ShapeDtypeStruct((rows, 512), BF16),
        jax.ShapeDtypeStruct((rows, 1024), BF16), jax.ShapeDtypeStruct((rows, 1024), BF16)]
    qt_spec = pl.BlockSpec((1, 1024, TM), lambda i: (i // nb, 0, i % nb))
    k_spec = pl.BlockSpec((1, TM, 1024), lambda i: (i // nb, i % nb, 0))
    vt_spec = pl.BlockSpec((1, TM // VT_CHUNK, 512, VT_CHUNK), lambda i: (i // nb, i % nb, 0, 0))
    out_t = lambda r: pl.BlockSpec((1, r, TM), lambda i: (i // nb, 0, i % nb))
    out_specs = [out_t(512), pl.BlockSpec((TM * H_A, DV_A), lambda i: (i, 0)), out_t(512), out_t(512),
                 out_t(H_B)] + [
        qt_spec, k_spec, vt_spec, qt_spec, k_spec, vt_spec, row(512), row(512), row(1024), row(1024)]
    return pl.pallas_call(
        functools.partial(_proj_prompt_kernel, nb),
        grid=(rows // TM,),
        in_specs=[pl.BlockSpec(memory_space=pltpu.SMEM), row(D_MODEL), mod, mod]
                 + [_const_spec(c.shape) for c in consts],
        out_specs=out_specs,
        out_shape=out_shape,
        scratch_shapes=[pltpu.VMEM((1, LANES), F32)],
        compiler_params=pltpu.CompilerParams(dimension_semantics=("arbitrary",),
                                             vmem_limit_bytes=VMEM_LIMIT),
        name="proj_prompt",
    )(mshift, x2, scale, shift, *consts)


def _proj_sample_kernel(x_ref, scale_ref, shift_ref, wq_ref, wn_ref, wf_ref, bf_ref, gq_ref, gk_ref, g_ref,
                        qa_ref, qb_ref, ka_ref, va_ref, kb_ref, vb_ref, lf_ref,
                        sga_ref, sgb_ref, sma_ref, smb_ref):
    hb = _hidden(x_ref, scale_ref, shift_ref)
    lf_ref[...] = _logf(hb, wf_ref, bf_ref)[:, :H_B]
    qa_ref[...] = _group_rms(_dot(hb, wq_ref[:, 0:512]), g_ref) * gq_ref[:, 0:512]
    qb_ref[...] = _group_rms(_dot(hb, wq_ref[:, 512:1024]), g_ref) * gq_ref[:, 512:1024]
    ka_ref[...] = _group_rms(_dot(hb, wn_ref[:, 0:512]), g_ref) * gk_ref[:, 0:512]
    kb_ref[...] = _group_rms(_dot(hb, wn_ref[:, 512:1024]), g_ref) * gk_ref[:, 512:1024]
    va_ref[...] = _dot(hb, wn_ref[:, 1024:1536])
    vb_ref[...] = _dot(hb, wn_ref[:, 1536:2048])
    sga_ref[...] = _silu(_dot(hb, wn_ref[:, 2048:2560])).astype(BF16)
    sgb_ref[...] = _silu(_dot(hb, wn_ref[:, 2560:3072])).astype(BF16)
    sma_ref[...] = jax.nn.sigmoid(_dot(hb, wn_ref[:, 3072:4096])).astype(BF16)
    smb_ref[...] = jax.nn.sigmoid(_dot(hb, wn_ref[:, 4096:5120])).astype(BF16)


def _proj_sample(x2, scale_rows, shift_rows, wts):
    rows = x2.shape[0]
    row = lambda w: pl.BlockSpec((TM, w), lambda i: (i, 0))
    consts = [wts["wq"], wts["wn"], wts["wf"], wts["bf"], wts["gq_row"], wts["gk_row"], wts["gsum"]]
    out_shape = [jax.ShapeDtypeStruct((rows, 512), F32)] * 6 + [jax.ShapeDtypeStruct((rows, H_B), F32)] + [
        jax.ShapeDtypeStruct((rows, 512), BF16), jax.ShapeDtypeStruct((rows, 512), BF16),
        jax.ShapeDtypeStruct((rows, 1024), BF16), jax.ShapeDtypeStruct((rows, 1024), BF16)]
    out_specs = [row(512)] * 6 + [row(H_B), row(512), row(512), row(1024), row(1024)]
    return pl.pallas_call(
        _proj_sample_kernel,
        grid=(rows // TM,),
        in_specs=[row(D_MODEL), row(D_MODEL), row(D_MODEL)] + [_const_spec(c.shape) for c in consts],
        out_specs=out_specs,
        out_shape=out_shape,
        compiler_params=pltpu.CompilerParams(dimension_semantics=("arbitrary",),
                                             vmem_limit_bytes=VMEM_LIMIT),
        name="proj_sample",
    )(x2, scale_rows, shift_rows, *consts)


def _attn_kernel(shared_v, online, lam_ref, qt0_ref, qt1_ref, k0_ref, k1_ref, vt_ref, o_ref,
                 acc_ref, l_ref, p_ref, p2_ref, pd_ref):
    i = pl.program_id(2)
    qts = (qt0_ref[0], qt1_ref[0])
    k_refs = (k0_ref, k1_ref)
    dv = acc_ref.shape[1]
    n_chunks = TK // VT_CHUNK
    acc_ref[...] = jnp.zeros_like(acc_ref)

    def pv(u, chunk, p):
        blk = vt_ref[0, chunk]
        v = blk if shared_v else blk[u * dv:(u + 1) * dv, :]
        return _dot(v, p)

    def scores_to_probs(j, masked, dst_ref):
        for u in range(2):
            kblk = k_refs[u][0, pl.ds(pl.multiple_of(j * TK, TK), TK), :]
            s = _dot(kblk, qts[u])
            if masked:
                kpos = lax.broadcasted_iota(jnp.int32, s.shape, 0)
                qpos = lax.broadcasted_iota(jnp.int32, s.shape, 1)
                s = jnp.where(kpos <= qpos, s, NEG)
            p = jnp.exp(s)
            l_ref[u] += jnp.sum(p.reshape(TK // 8, 8, TQ), axis=0)
            dst_ref[u] = p.astype(BF16)

    def probs_times_values(j, src_ref):
        for u in range(2):
            upd = None
            for c in range(n_chunks):
                d = pv(u, j * n_chunks + c, src_ref[u, c * VT_CHUNK:(c + 1) * VT_CHUNK, :])
                upd = d if upd is None else upd + d
            acc_ref[u] += upd

    def step_online(j, carry, masked):
        out = []
        for u in range(2):
            kblk = k_refs[u][0, pl.ds(pl.multiple_of(j * TK, TK), TK), :]
            s = _dot(kblk, qts[u])
            if masked:
                kpos = lax.broadcasted_iota(jnp.int32, s.shape, 0)
                qpos = lax.broadcasted_iota(jnp.int32, s.shape, 1)
                s = jnp.where(kpos <= qpos, s, NEG)
            m, l = carry[u]
            m_new = jnp.maximum(m, jnp.max(s, axis=0, keepdims=True))
            alpha = jnp.exp(m - m_new)
            p = jnp.exp(s - m_new)
            l = alpha * l + jnp.sum(p, axis=0, keepdims=True)
            p = p.astype(BF16)
            upd = None
            for c in range(n_chunks):
                d = pv(u, j * n_chunks + c, p[c * VT_CHUNK:(c + 1) * VT_CHUNK, :])
                upd = d if upd is None else upd + d
            acc_ref[u] = alpha * acc_ref[u] + upd
            out.append((m_new, l))
        return tuple(out)

    if online:
        init = tuple((jnp.full((1, TQ), NEG, F32), jnp.zeros((1, TQ), F32)) for _ in range(2))
        carry = lax.fori_loop(0, i, lambda j, c: step_online(j, c, False), init)
        carry = step_online(i, carry, True)
        l0, l1 = carry[0][1], carry[1][1]
    else:
        l_ref[...] = jnp.zeros_like(l_ref)

        @pl.when(i == 0)
        def _():
            scores_to_probs(0, True, pd_ref)
            probs_times_values(0, pd_ref)

        @pl.when(i > 0)
        def _():
            scores_to_probs(0, False, p_ref)

            def body(jj, _):
                j = 2 * jj + 1
                scores_to_probs(j, False, p2_ref)
                probs_times_values(j - 1, p_ref)
                scores_to_probs(j + 1, False, p_ref)
                probs_times_values(j, p2_ref)
                return 0

            lax.fori_loop(0, (i - 1) // 2, body, 0)

            @pl.when(i % 2 == 0)
            def _():
                scores_to_probs(i - 1, False, p2_ref)
                probs_times_values(i - 2, p_ref)
                scores_to_probs(i, True, pd_ref)
                probs_times_values(i - 1, p2_ref)
                probs_times_values(i, pd_ref)

            @pl.when(i % 2 == 1)
            def _():
                scores_to_probs(i, True, pd_ref)
                probs_times_values(i - 1, p_ref)
                probs_times_values(i, pd_ref)

        l0, l1 = (jnp.sum(l_ref[u], axis=0, keepdims=True) for u in range(2))

    o0 = acc_ref[0] * (1.0 / l0)
    o1 = acc_ref[1] * (1.0 / l1)
    o = o0 - lam_ref[0] * o1 if shared_v else jnp.concatenate([o0, o1], axis=0)
    o_ref[0] = o.T


def _attention(qt, kaug, vt, lam, shared_v, online):
    batch, _, seq = qt.shape
    n_steps = 4
    if shared_v:
        g0 = lambda h: h
        g1 = lambda h: H_A + h
    else:
        g0 = lambda h: 2 * h
        g1 = lambda h: 2 * h + 1
    dv = DV_A if shared_v else DH_B
    qspec = lambda g: pl.BlockSpec((1, LANES, TQ), lambda b, h, i: (b, g(h), i))
    kspec = lambda g: pl.BlockSpec((1, seq, LANES), lambda b, h, i: (b, 0, g(h)))
    return pl.pallas_call(
        functools.partial(_attn_kernel, shared_v, online),
        grid=(batch, n_steps, seq // TQ),
        in_specs=[pl.BlockSpec(memory_space=pltpu.SMEM), qspec(g0), qspec(g1), kspec(g0), kspec(g1),
                  pl.BlockSpec((1, seq // VT_CHUNK, LANES, VT_CHUNK), lambda b, h, i: (b, 0, h, 0))],
        out_specs=pl.BlockSpec((1, TQ, LANES), lambda b, h, i: (b, i, h)),
        out_shape=jax.ShapeDtypeStruct((batch, seq, 512), F32),
        scratch_shapes=[pltpu.VMEM((2, dv, TQ), F32), pltpu.VMEM((2, 8, TQ), F32),
                        pltpu.VMEM((2, TK, TQ), BF16), pltpu.VMEM((2, TK, TQ), BF16),
                        pltpu.VMEM((2, TK, TQ), BF16)],
        compiler_params=pltpu.CompilerParams(
            dimension_semantics=("arbitrary", "arbitrary", "arbitrary"), vmem_limit_bytes=VMEM_LIMIT),
        name=("attn_a" if shared_v else "attn_b") + ("_online" if online else ""),
    )(lam, qt, qt, kaug, kaug, vt)


def _decode_kernel(past, pps, pt_ref, lam_ref, qa_ref, qb_ref,
                   kta_hbm, va_hbm, ktb_hbm, vtb_hbm, lft_hbm,
                   kan_ref, van_ref, kbn_ref, vbn_ref, lfn_ref, upper_ref, oa_ref, ob_ref,
                   qbd_ref, m_ref, l_ref, acca_ref, accb_ref, fc_ref,
                   kta_buf, va_buf, ktb_buf, vtb_buf, lft_buf, sem):
    step_i = pl.program_id(1)
    n_steps = pl.num_programs(1)
    g = pl.program_id(0) * n_steps + step_i
    slot = g % 2
    pools = ((kta_hbm, kta_buf), (va_hbm, va_buf), (ktb_hbm, ktb_buf), (vtb_hbm, vtb_buf), (lft_hbm, lft_buf))

    def page_copies(gg, dst_slot):
        out = []
        for j in range(pps):
            pg = pt_ref[gg * pps + j]
            for k, (hbm, buf) in enumerate(pools):
                out.append(pltpu.make_async_copy(hbm.at[pg], buf.at[dst_slot, j], sem.at[dst_slot, k]))
        return out

    @pl.when(g == 0)
    def _():
        for c in page_copies(0, 0):
            c.start()

    @pl.when(g + 1 < pl.num_programs(0) * n_steps)
    def _():
        for c in page_copies(g + 1, 1 - slot):
            c.start()

    for c in page_copies(g, slot):
        c.wait()
    kta_refs, va_refs, ktb_refs, vtb_refs, lft_refs = (
        [buf.at[slot, j] for j in range(pps)] for _, buf in pools)
    rows = 32
    n_tok = 4
    pair_rows = kan_ref.shape[0]
    off = (pl.program_id(0) % 2) * n_tok
    page = LANES

    def row_ids(shape):
        r = lax.broadcasted_iota(jnp.int32, shape, 0)
        return ((r & 3, ((r >> 2) & 1) * H_A + (r >> 3), r >> 3),
                (r >> 3, r & 7, r & 7))

    def row_slope(head):
        return jnp.where(head == 0, SLOPES[0], jnp.where(head == 1, SLOPES[1],
               jnp.where(head == 2, SLOPES[2], SLOPES[3])))

    @pl.when(step_i == 0)
    def _():
        col_group = lax.broadcasted_iota(jnp.int32, (rows, 512), 1) >> 6
        for u, q_ref in enumerate((qa_ref, qb_ref)):
            tok, group, _ = row_ids((rows, 512))[u]
            qbd = jnp.zeros((rows, 512), F32)
            for t in range(n_tok):
                q_row = jnp.broadcast_to(q_ref[pl.ds(off + t, 1), :], (rows, 512))
                qbd = jnp.where((tok == t) & (group == col_group), q_row, qbd)
            qbd_ref[u] = qbd.astype(BF16)
        m_ref[...] = jnp.full(m_ref.shape, NEG, F32)
        l_ref[...] = jnp.zeros_like(l_ref)
        acca_ref[...] = jnp.zeros_like(acca_ref)
        accb_ref[...] = jnp.zeros_like(accb_ref)
        fc_ref[...] = jnp.zeros_like(fc_ref)

    def softmax_update(u, s):
        m_prev = m_ref[u][:, 0:1]
        l_prev = l_ref[u][:, 0:1]
        m_new = jnp.maximum(m_prev, jnp.max(s, axis=1, keepdims=True))
        alpha = jnp.exp(m_prev - m_new)
        pr = jnp.exp(s - m_new)
        l_new = alpha * l_prev + jnp.sum(pr, axis=1, keepdims=True)
        m_ref[u] = jnp.broadcast_to(m_new, (rows, LANES))
        l_ref[u] = jnp.broadcast_to(l_new, (rows, LANES))
        return alpha, pr

    is_last = step_i == n_steps - 1
    width = pps * page
    lane = lax.broadcasted_iota(jnp.int32, (rows, width), 1)
    tn = lax.broadcasted_iota(jnp.int32, (rows, page), 1)
    (tok_a, _, head_a), (tok_b, _, _) = row_ids((rows, page))
    pad_rows = lambda ref: jnp.concatenate(
        [ref[...], jnp.zeros((page - pair_rows, 512), F32)], axis=0).astype(BF16)

    def valid(tok):
        return is_last & (tn >= off) & (tn - off <= tok) & (tn < off + n_tok)

    dist = (past + row_ids((rows, width))[0][0] - (step_i * width + lane)).astype(F32)
    s_a = _dot(qbd_ref[0], jnp.concatenate([r[...].astype(BF16) for r in kta_refs], axis=1))
    s_a = s_a - row_slope(row_ids((rows, width))[0][2]) * dist
    s_an = _dot_nt(qbd_ref[0], pad_rows(kan_ref)) - row_slope(head_a) * (tok_a - (tn - off)).astype(F32)
    alpha, pr = softmax_update(0, jnp.concatenate([s_a, jnp.where(valid(tok_a), s_an, NEG)], axis=1))
    pr = pr.astype(BF16)
    v_all = jnp.concatenate(
        [jnp.concatenate([r[pl.ds(h, page, stride=H_A), :].astype(BF16) for h in range(H_A)], axis=1)
         for r in va_refs], axis=0)
    acca_ref[...] = (alpha * acca_ref[...] + _dot(pr[:, 0:width], v_all)
                     + _dot(pr[:, width:], pad_rows(van_ref)))

    parts = []
    for j in range(pps):
        parts.extend(v.astype(F32) for v in _split3(lft_refs[j][...]))
        parts.append(jnp.zeros((8, page), F32))
    cs = _dot(jnp.concatenate(parts, axis=0).astype(BF16), upper_ref[...])
    f_carry = fc_ref[...][:, 0:1]
    f_pages = []
    for j in range(pps):
        local = cs[32 * j:32 * j + 8] + cs[32 * j + 8:32 * j + 16] + cs[32 * j + 16:32 * j + 24]
        f_page = local + f_carry
        f_carry = f_carry + local[:, page - 1:page]
        f_pages.append(jnp.concatenate([f_page] * n_tok, axis=0))
    fc_ref[...] = jnp.broadcast_to(f_carry, (8, LANES))
    lfn = lfn_ref[0]
    t8 = lax.broadcasted_iota(jnp.int32, (8, page), 1)
    f_new = jnp.broadcast_to(f_carry, (8, page))
    for t in range(n_tok):
        lf_t = jnp.sum(jnp.where(t8 == off + t, lfn, 0.0), axis=1, keepdims=True)
        f_new = f_new + jnp.where(t8 >= off + t, lf_t, 0.0)
    s_b = _dot(qbd_ref[1], jnp.concatenate([r[...].astype(BF16) for r in ktb_refs], axis=1))
    s_b = s_b - jnp.concatenate(f_pages, axis=1)
    s_bn = _dot_nt(qbd_ref[1], pad_rows(kbn_ref)) - jnp.concatenate([f_new] * n_tok, axis=0)
    alpha, pr = softmax_update(1, jnp.concatenate([s_b, jnp.where(valid(tok_b), s_bn, NEG)], axis=1))
    pr = pr.astype(BF16)
    vt_all = jnp.concatenate([r[...].astype(BF16) for r in vtb_refs], axis=1)
    accb_ref[...] = (alpha * accb_ref[...] + _dot_nt(pr[:, 0:width], vt_all)
                     + _dot(pr[:, width:], pad_rows(vbn_ref)))

    @pl.when(is_last)
    def _():
        on_a = acca_ref[...] * (1.0 / l_ref[0][:, 0:1])
        head_b = row_ids((rows, 512))[1][2]
        col_head = lax.broadcasted_iota(jnp.int32, (rows, 512), 1) >> 6
        on_b = jnp.where(head_b == col_head, accb_ref[...] * (1.0 / l_ref[1][:, 0:1]), 0.0)

        def write(first):
            for h in range(H_A):
                blk = on_a[8 * h:8 * h + 8, h * DV_A:(h + 1) * DV_A]
                oa_ref[first:first + n_tok, h * DV_A:(h + 1) * DV_A] = (
                    blk[0:n_tok] - lam_ref[0] * blk[n_tok:2 * n_tok])
            for t in range(n_tok):
                ob_ref[first + t:first + t + 1, :] = jnp.sum(on_b[8 * t:8 * t + 8], axis=0, keepdims=True)

        pl.when(off == 0)(lambda: write(0))
        pl.when(off != 0)(lambda: write(n_tok))


def _decode(page_table, lam, qa, qb, pools, news, lfn_t, upper):
    n_seq, n_pages = page_table.shape
    page = pools[0].shape[2]
    pps = PAGES_PER_STEP
    n_rows = qa.shape[0]
    past = n_pages * page
    pair_spec = pl.BlockSpec((2 * (n_rows // n_seq), 512), lambda n, s, pt: (n // 2, 0))
    page_bufs = [pltpu.VMEM((2, pps) + pool.shape[1:], F32) for pool in pools]
    grid_spec = pltpu.PrefetchScalarGridSpec(
        num_scalar_prefetch=1,
        grid=(n_seq, n_pages // pps),
        in_specs=[pl.BlockSpec(memory_space=pltpu.SMEM), pair_spec, pair_spec]
                 + [pl.BlockSpec(memory_space=pl.ANY)] * len(pools)
                 + [pair_spec] * 4
                 + [pl.BlockSpec((1, H_B, page), lambda n, s, pt: (n // 2, 0, 0)),
                    pl.BlockSpec((page, page), lambda n, s, pt: (0, 0))],
        out_specs=[pair_spec, pair_spec],
        scratch_shapes=[pltpu.VMEM((2, 32, 512), BF16), pltpu.VMEM((2, 32, LANES), F32),
                        pltpu.VMEM((2, 32, LANES), F32), pltpu.VMEM((32, 512), F32),
                        pltpu.VMEM((32, 512), F32), pltpu.VMEM((H_B, LANES), F32)]
                       + page_bufs + [pltpu.SemaphoreType.DMA((2, len(pools)))],
    )
    return pl.pallas_call(
        functools.partial(_decode_kernel, past, pps),
        grid_spec=grid_spec,
        out_shape=[jax.ShapeDtypeStruct((n_rows, 512), F32)] * 2,
        compiler_params=pltpu.CompilerParams(dimension_semantics=("arbitrary", "arbitrary"),
                                             vmem_limit_bytes=VMEM_LIMIT),
        name="decode",
    )(page_table.reshape(-1), lam, qa, qb, *pools, *news, lfn_t, upper)


def _merge_kernel(x_ref, oa_ref, ob_ref, sga_ref, sgb_ref, sma_ref, smb_ref, gres_ref, gsub_ref,
                  woa_ref, wob_ref, wo_ref, y_ref):
    oa = oa_ref[...]
    heads = []
    for h in range(H_A):
        oh = oa[:, h * DV_A:(h + 1) * DV_A]
        heads.append(oh * lax.rsqrt(jnp.mean(oh * oh, axis=-1, keepdims=True) + EPS))
    oa = jnp.concatenate(heads, axis=1) * gsub_ref[...]
    ya = _dot((oa * sga_ref[...].astype(F32)).astype(BF16), woa_ref[...])
    yb = _dot((ob_ref[...] * sgb_ref[...].astype(F32)).astype(BF16), wob_ref[...])
    mix = sma_ref[...].astype(F32) * ya + smb_ref[...].astype(F32) * yb
    y = _dot(mix.astype(BF16), wo_ref[...])
    y_ref[...] = x_ref[...] + gres_ref[...] * y


def _merge(x2, oa, ob, sga, sgb, sma, smb, gres, gres_spec, wts):
    rows = x2.shape[0]
    row = lambda w: pl.BlockSpec((TM_MERGE, w), lambda i: (i, 0))
    consts = [wts["gsub"], wts["woa"], wts["wob"], wts["wo"]]
    return pl.pallas_call(
        _merge_kernel,
        grid=(rows // TM_MERGE,),
        in_specs=[row(D_MODEL), row(512), row(512), row(512), row(512), row(1024), row(1024), gres_spec]
                 + [_const_spec(c.shape) for c in consts],
        out_specs=row(D_MODEL),
        out_shape=jax.ShapeDtypeStruct((rows, D_MODEL), F32),
        compiler_params=pltpu.CompilerParams(dimension_semantics=("arbitrary",),
                                             vmem_limit_bytes=VMEM_LIMIT),
        name="merge",
    )(x2, oa, ob, sga, sgb, sma, smb, gres, *consts)


def _ones_constants():
    gsum = np.kron(np.eye(N_GROUPS), np.ones((DH_A, DH_A)))
    tri = np.tril(np.ones((TM, TM)))
    upper = np.triu(np.ones((LANES, LANES)))
    bf = lambda a: jnp.asarray(a, BF16)
    return dict(gsum=bf(gsum), tri=bf(tri), upper=bf(upper))


def _bf16_ceil(x):
    y = x.astype(BF16).astype(F32)
    return jnp.where(y < x, y * (1.0 + 2.0 ** -7), y)


def kernel(x_prompt, x_sample, cache_a_k, cache_a_v, cache_b_k, cache_b_v, cache_b_logf, page_table,
           c_prompt, c_sample, w_ada, b_ada, w_in, b_f, g_q_a, g_k_a, g_q_b, g_k_b,
           lambda_q1, lambda_k1, lambda_q2, lambda_k2, g_sub_a, w_out_a, w_out_b, w_o):
    assert w_ada.shape[0] == 1, "single-layer step"
    batch, seq, _ = x_prompt.shape
    n_seq, n_tok, _ = x_sample.shape
    n_pool, page = cache_a_k.shape[1], cache_a_k.shape[2]
    layer = 0
    lam_init = 0.8 - 0.6 * math.exp(-0.3 * layer)
    lam = (jnp.exp(jnp.sum(lambda_q1[layer] * lambda_k1[layer]))
           - jnp.exp(jnp.sum(lambda_q2[layer] * lambda_k2[layer])) + lam_init).reshape(1).astype(F32)

    w = w_in[layer]
    sec = np.cumsum((0, 512, 512, 512, 512, 512, 512, 512, H_B, 512, 1024, 1024))
    cols = lambda k: w[:, sec[k]:sec[k + 1]]
    wq = jnp.concatenate([cols(0), cols(4)], axis=1).astype(BF16)
    wn = jnp.concatenate([cols(1), cols(5), cols(2), cols(6), cols(3), cols(8), cols(9), cols(10)],
                         axis=1).astype(BF16)
    wf = jnp.pad(cols(7), ((0, 0), (0, LANES - H_B))).astype(BF16)
    gq = jnp.concatenate([jnp.tile(g_q_a[layer], N_GROUPS) * DH_A ** -0.5,
                          jnp.tile(g_q_b[layer], N_GROUPS) * DH_B ** -0.5])
    gk = jnp.concatenate([jnp.tile(g_k_a[layer], N_GROUPS), jnp.tile(g_k_b[layer], N_GROUPS)])
    wts = dict(_ones_constants(),
               wq=wq, wqkt=jnp.concatenate([wq, wn[:, 0:1024], wn[:, 1536:2048]], axis=1).T,
               wn=wn, wn_prompt=jnp.concatenate([wn[:, 1024:1536], wn[:, 2048:]], axis=1),

               wf=wf, bf=jnp.pad(b_f[layer], (0, LANES - H_B)).reshape(1, LANES),
               gqk_col=jnp.concatenate([gq, gk]).reshape(-1, 1), gq_row=gq.reshape(1, -1), gk_row=gk.reshape(1, -1),
               gsub=(jnp.tile(g_sub_a[layer], H_A) * (1.0 - lam_init)).reshape(1, W_A),
               woa=w_out_a[layer].astype(BF16), wob=w_out_b[layer].astype(BF16), wo=w_o[layer].astype(BF16))
    bound = lambda gq_, gk_: _bf16_ceil(8.1 * jnp.max(jnp.abs(gq_ * gk_)))
    mshift = jnp.stack([bound(g_q_a[layer], g_k_a[layer]), bound(g_q_b[layer], g_k_b[layer])]).astype(F32)

    n_c = batch + n_seq
    c_all = jnp.pad(jnp.concatenate([c_prompt, c_sample], axis=0), ((0, -n_c % 8), (0, 0)))
    mod = _ada(c_all, w_ada[layer], b_ada[layer])
    shift, scale, gres = (mod[:, k * D_MODEL:(k + 1) * D_MODEL] for k in range(3))

    xp2 = x_prompt.reshape(batch * seq, D_MODEL)
    p3 = lambda a: a[:batch].reshape(batch, 1, D_MODEL)
    (kta, va, ktb, vtbo, lft, qta, kaa, vta, qtb, kab, vtb, sga, sgb, sma, smb) = _proj_prompt(
        xp2, p3(scale), p3(shift), mshift, wts, batch, seq)
    online = jnp.max(mshift) > FAST_PATH_MAX_BOUND
    attend = lambda *a: lax.cond(online, lambda: _attention(*a, True), lambda: _attention(*a, False))
    oa = attend(qta, kaa, vta, lam, True).reshape(batch * seq, W_A)
    ob = attend(qtb, kab, vtb, lam, False).reshape(batch * seq, W_B)
    nb = seq // TM_MERGE
    gres_p_spec = pl.BlockSpec((None, 1, D_MODEL), lambda i: (i // nb, 0, 0))
    yp = _merge(xp2, oa, ob, sga, sgb, sma, smb, p3(gres), gres_p_spec, wts).reshape(batch, seq, D_MODEL)

    xs2 = x_sample.reshape(n_seq * n_tok, D_MODEL)
    rep = lambda a: jnp.repeat(a[batch:n_c], n_tok, axis=0)
    (qa_s, qb_s, ka_s, va_s, kb_s, vb_s, lf_s, sga_s, sgb_s, sma_s, smb_s) = _proj_sample(
        xs2, rep(scale), rep(shift), wts)
    pools = [jnp.transpose(cache_a_k[layer], (0, 2, 3, 4, 1)).reshape(n_pool, 512, page),
             cache_a_v[layer].reshape(n_pool, page * H_A, DV_A),
             jnp.transpose(cache_b_k[layer], (0, 2, 3, 1)).reshape(n_pool, 512, page),
             jnp.transpose(cache_b_v[layer], (0, 2, 3, 1)).reshape(n_pool, 512, page),
             jnp.swapaxes(cache_b_logf[layer], 1, 2)]
    lfn_t = jnp.pad(jnp.swapaxes(lf_s.reshape(n_seq // 2, 2 * n_tok, H_B), 1, 2),
                    ((0, 0), (0, 0), (0, page - 2 * n_tok)))
    oa_s, ob_s = _decode(page_table, lam, qa_s, qb_s, pools, [ka_s, va_s, kb_s, vb_s], lfn_t, wts["upper"])
    gres_s_spec = pl.BlockSpec((TM_MERGE, D_MODEL), lambda i: (i, 0))
    ys = _merge(xs2, oa_s, ob_s, sga_s, sgb_s, sma_s, smb_s,
                rep(gres), gres_s_spec, wts).reshape(n_seq, n_tok, D_MODEL)

    return (yp, ys,
            jnp.moveaxis(kta.reshape(1, batch, 2, H_A, DH_A, seq), -1, 2), va.reshape(1, batch, seq, H_A, DV_A),
            jnp.moveaxis(ktb.reshape(1, batch, H_B, DH_B, seq), -1, 2),
            jnp.moveaxis(vtbo.reshape(1, batch, H_B, DH_B, seq), -1, 2),
            jnp.moveaxis(lft.reshape(1, batch, H_B, seq), -1, 2),
            ka_s.reshape(1, n_seq, n_tok, 2, H_A, DH_A), va_s.reshape(1, n_seq, n_tok, H_A, DV_A),
            kb_s.reshape(1, n_seq, n_tok, H_B, DH_B), vb_s.reshape(1, n_seq, n_tok, H_B, DH_B),
            lf_s.reshape(1, n_seq, n_tok, H_B))
```

```python
import functools
import math

import numpy as np
import jax
import jax.numpy as jnp
from jax import lax
from jax.experimental import pallas as pl
from jax.experimental.pallas import tpu as pltpu

F32 = jnp.float32
BF16 = jnp.bfloat16

D_MODEL = 1024
H_A, DH_A, DV_A = 4, 64, 128
H_B, DH_B = 8, 64
W_A = H_A * DV_A
W_B = H_B * DH_B
N_GROUPS = 8
EPS = 1e-6
NEG = -1e30
SLOPES = tuple(2.0 ** (-8.0 * (h + 1) / H_A) for h in range(H_A))
LANES = 128

TM = 256
TM_MERGE = 512
TQ = 512
TK = 512
VT_CHUNK = TM
PAGES_PER_STEP = 8
FAST_PATH_MAX_BOUND = 30.0
VMEM_LIMIT = 56 * 1024 * 1024


def _dot(a, b):
    return jnp.dot(a, b, preferred_element_type=F32)


def _dot_nt(a, b):
    return lax.dot_general(a, b, (((1,), (1,)), ((), ())), preferred_element_type=F32)


def _split3(x):
    hi = x.astype(BF16)
    r1 = x - hi.astype(F32)
    mid = r1.astype(BF16)
    lo = (r1 - mid.astype(F32)).astype(BF16)
    return hi, mid, lo


def _silu(x):
    return x * jax.nn.sigmoid(x)


def _const_spec(shape):
    nd = len(shape)
    return pl.BlockSpec(shape, lambda *_: (0,) * nd, pipeline_mode=pl.Buffered(1))


def _ada_kernel(c_ref, w_ref, b_ref, o_ref):
    c = c_ref[...]
    o_ref[...] = _dot(_silu(c).astype(BF16), w_ref[...].astype(BF16)) + b_ref[...]


def _ada(c_all, w_ada, b_ada):
    rows = c_all.shape[0]
    n = w_ada.shape[1]
    bn = 1024
    return pl.pallas_call(
        _ada_kernel,
        grid=(n // bn,),
        in_specs=[pl.BlockSpec((rows, D_MODEL), lambda j: (0, 0)),
                  pl.BlockSpec((D_MODEL, bn), lambda j: (0, j)),
                  pl.BlockSpec((1, bn), lambda j: (0, j))],
        out_specs=pl.BlockSpec((rows, bn), lambda j: (0, j)),
        out_shape=jax.ShapeDtypeStruct((rows, n), F32),
        name="ada",
    )(c_all, w_ada, b_ada.reshape(1, n))


def _hidden(x_ref, scale_ref, shift_ref):
    x = x_ref[...]
    ms = jnp.mean(x * x, axis=-1, keepdims=True)
    h = x * lax.rsqrt(ms + EPS) * (1.0 + scale_ref[...]) + shift_ref[...]
    return h.astype(BF16)


def _group_rms(z, g_ref):
    zz = z * z
    hi = zz.astype(BF16)
    lo = (zz - hi.astype(F32)).astype(BF16)
    ss = _dot(hi, g_ref[...]) + _dot(lo, g_ref[...])
    return z * lax.rsqrt(ss * (1.0 / DH_A) + EPS)


def _log_sigmoid(z):
    return jnp.minimum(z, 0.0) - jnp.log1p(jnp.exp(-jnp.abs(z)))


def _logf(hb, wf_ref, bf_ref):
    z = _dot(hb, wf_ref[...]) + bf_ref[...]
    lane = lax.broadcasted_iota(jnp.int32, z.shape, 1)
    return jnp.where(lane < H_B, _log_sigmoid(z), 0.0)


def _proj_prompt_kernel(nb, mshift_ref, x_ref, scale_ref, shift_ref, wqkt_ref, wn_ref, wf_ref, bf_ref,
                        gqk_ref, tri_ref,
                        kta_ref, va_ref, ktb_ref, vtbo_ref, lft_ref,
                        qta_ref, kaa_ref, vta_ref, qtb_ref, kab_ref, vtb_ref,
                        sga_ref, sgb_ref, sma_ref, smb_ref, carry_ref):
    tm = x_ref.shape[0]
    ib = pl.program_id(0) % nb
    hb = _hidden(x_ref, scale_ref, shift_ref)

    lf = _logf(hb, wf_ref, bf_ref)
    lft_ref[0] = lf.T[0:H_B, :]
    hi, mid, lo = _split3(lf)
    tri = tri_ref[...]
    f_local = _dot(tri, hi) + _dot(tri, mid) + _dot(tri, lo)

    @pl.when(ib == 0)
    def _():
        carry_ref[...] = jnp.zeros_like(carry_ref)

    f_nat = f_local + carry_ref[...]
    carry_ref[...] = f_nat[tm - 1:tm, :]
    f_t = f_nat.T[0:H_B, :]

    n_g = 4 * N_GROUPS
    n_qk = n_g * DH_A
    qkt = jnp.concatenate([_dot_nt(wqkt_ref[0:n_qk // 2, :], hb),
                           _dot_nt(wqkt_ref[n_qk // 2:n_qk, :], hb)], axis=0)
    g3 = qkt.reshape(n_g, DH_A, tm)
    ssq = jnp.sum(g3 * g3, axis=1, keepdims=True)
    g3 = (qkt * gqk_ref[...]).reshape(n_g, DH_A, tm) * lax.rsqrt(ssq * (1.0 / DH_A) + EPS)
    kta_ref[0] = g3[2 * N_GROUPS:3 * N_GROUPS].reshape(N_GROUPS * DH_A, tm)
    ktb_ref[0] = g3[3 * N_GROUPS:4 * N_GROUPS].reshape(N_GROUPS * DH_A, tm)

    r8 = lax.broadcasted_iota(jnp.int32, (8, tm), 0)
    pos = ib * tm + lax.broadcasted_iota(jnp.int32, (8, tm), 1)
    pos_hi = (pos >> 7).astype(F32)
    pos_lo = (pos & 127).astype(F32)
    zeros_tail = jnp.zeros((LANES - DH_A - 8, tm), F32)
    m_a = mshift_ref[0]
    m_b = mshift_ref[1]

    def operand(group, aug):
        return jnp.concatenate([g3[group], aug, zeros_tail], axis=0)

    k_aug_a = jnp.where(r8 == 2, pos_hi, jnp.where(r8 == 3, pos_lo,
              jnp.where((r8 <= 1) | (r8 == 4), 1.0, 0.0)))
    for g in range(N_GROUPS):
        slope = SLOPES[g % H_A]
        q_aug = jnp.where(r8 == 0, -slope * 128.0 * pos_hi,
                jnp.where(r8 == 1, -slope * pos_lo,
                jnp.where(r8 == 2, slope * 128.0,
                jnp.where(r8 == 3, slope,
                jnp.where(r8 == 4, -m_a, 0.0)))))
        qta_ref[0, g * LANES:(g + 1) * LANES, :] = operand(g, q_aug).astype(BF16)
        kaa_ref[0, :, g * LANES:(g + 1) * LANES] = operand(2 * N_GROUPS + g, k_aug_a).T.astype(BF16)
    ft_hi, ft_mid, ft_lo = _split3(f_t)
    for g in range(N_GROUPS):
        bc = lambda v: jnp.broadcast_to(v[g:g + 1, :].astype(F32), (8, tm))
        q_aug = jnp.where(r8 == 0, bc(ft_hi),
                jnp.where(r8 == 1, bc(ft_mid),
                jnp.where(r8 == 2, bc(ft_lo),
                jnp.where(r8 <= 5, -1.0,
                jnp.where(r8 == 6, -m_b, 0.0)))))
        k_aug = jnp.where(r8 <= 2, 1.0,
                jnp.where(r8 == 3, bc(ft_hi),
                jnp.where(r8 == 4, bc(ft_mid),
                jnp.where(r8 == 5, bc(ft_lo),
                jnp.where(r8 == 6, 1.0, 0.0)))))
        qtb_ref[0, g * LANES:(g + 1) * LANES, :] = operand(N_GROUPS + g, q_aug).astype(BF16)
        kab_ref[0, :, g * LANES:(g + 1) * LANES] = operand(3 * N_GROUPS + g, k_aug).T.astype(BF16)

    va = _dot(hb, wn_ref[:, 0:512])
    for h in range(H_A):
        va_ref[pl.ds(h, tm, stride=H_A), :] = va[:, h * DV_A:(h + 1) * DV_A]
    vta_ref[0, 0] = va.T.astype(BF16)
    vbt = _dot_nt(wqkt_ref[n_qk:, :], hb)
    vtbo_ref[0] = vbt
    vtb_ref[0, 0] = vbt.astype(BF16)

    sga_ref[...] = _silu(_dot(hb, wn_ref[:, 512:1024])).astype(BF16)
    sgb_ref[...] = _silu(_dot(hb, wn_ref[:, 1024:1536])).astype(BF16)
    sma_ref[...] = jax.nn.sigmoid(_dot(hb, wn_ref[:, 1536:2560])).astype(BF16)
    smb_ref[...] = jax.nn.sigmoid(_dot(hb, wn_ref[:, 2560:3584])).astype(BF16)


def _proj_prompt(x2, scale, shift, mshift, wts, batch, seq):
    rows = batch * seq
    nb = seq // TM
    nk = seq // VT_CHUNK
    row = lambda w: pl.BlockSpec((TM, w), lambda i: (i, 0))
    mod = pl.BlockSpec((None, 1, D_MODEL), lambda i: (i // nb, 0, 0))
    consts = [wts["wqkt"], wts["wn_prompt"], wts["wf"], wts["bf"], wts["gqk_col"], wts["tri"]]
    seq_t = lambda r: jax.ShapeDtypeStruct((batch, r, seq), F32)
    out_shape = [seq_t(512), jax.ShapeDtypeStruct((rows * H_A, DV_A), F32), seq_t(512), seq_t(512),
                 seq_t(H_B)] + [
        jax.ShapeDtypeStruct((batch, 1024, seq), BF16), jax.ShapeDtypeStruct((batch, seq, 1024), BF16),
        jax.ShapeDtypeStruct((batch, nk, W_A, VT_CHUNK), BF16),
        jax.ShapeDtypeStruct((batch, 1024, seq), BF16), jax.ShapeDtypeStruct((batch, seq, 1024), BF16),
        jax.ShapeDtypeStruct((batch, nk, W_B, VT_CHUNK), BF16),
        jax.ShapeDtypeStruct((rows, 512), BF16), jax.ShapeDtypeStruct((rows, 512), BF16),
        jax.ShapeDtypeStruct((rows, 1024), BF16), jax.ShapeDtypeStruct((rows, 1024), BF16)]
    qt_spec = pl.BlockSpec((1, 1024, TM), lambda i: (i // nb, 0, i % nb))
    k_spec = pl.BlockSpec((1, TM, 1024), lambda i: (i // nb, i % nb, 0))
    vt_spec = pl.BlockSpec((1, TM // VT_CHUNK, 512, VT_CHUNK), lambda i: (i // nb, i % nb, 0, 0))
    out_t = lambda r: pl.BlockSpec((1, r, TM), lambda i: (i // nb, 0, i % nb))
    out_specs = [out_t(512), pl.BlockSpec((TM * H_A, DV_A), lambda i: (i, 0)), out_t(512), out_t(512),
                 out_t(H_B)] + [
        qt_spec, k_spec, vt_spec, qt_spec, k_spec, vt_spec, row(512), row(512), row(1024), row(1024)]
    return pl.pallas_call(
        functools.partial(_proj_prompt_kernel, nb),
        grid=(rows // TM,),
        in_specs=[pl.BlockSpec(memory_space=pltpu.SMEM), row(D_MODEL), mod, mod]
                 + [_const_spec(c.shape) for c in consts],
        out_specs=out_specs,
        out_shape=out_shape,
        scratch_shapes=[pltpu.VMEM((1, LANES), F32)],
        compiler_params=pltpu.CompilerParams(dimension_semantics=("arbitrary",),
                                             vmem_limit_bytes=VMEM_LIMIT),
        name="proj_prompt",
    )(mshift, x2, scale, shift, *consts)


def _proj_sample_kernel(x_ref, scale_ref, shift_ref, wq_ref, wn_ref, wf_ref, bf_ref, gq_ref, gk_ref, g_ref,
                        qa_ref, qb_ref, ka_ref, va_ref, kb_ref, vb_ref, lf_ref,
                        sga_ref, sgb_ref, sma_ref, smb_ref):
    hb = _hidden(x_ref, scale_ref, shift_ref)
    lf_ref[...] = _logf(hb, wf_ref, bf_ref)[:, :H_B]
    qa_ref[...] = _group_rms(_dot(hb, wq_ref[:, 0:512]), g_ref) * gq_ref[:, 0:512]
    qb_ref[...] = _group_rms(_dot(hb, wq_ref[:, 512:1024]), g_ref) * gq_ref[:, 512:1024]
    ka_ref[...] = _group_rms(_dot(hb, wn_ref[:, 0:512]), g_ref) * gk_ref[:, 0:512]
    kb_ref[...] = _group_rms(_dot(hb, wn_ref[:, 512:1024]), g_ref) * gk_ref[:, 512:1024]
    va_ref[...] = _dot(hb, wn_ref[:, 1024:1536])
    vb_ref[...] = _dot(hb, wn_ref[:, 1536:2048])
    sga_ref[...] = _silu(_dot(hb, wn_ref[:, 2048:2560])).astype(BF16)
    sgb_ref[...] = _silu(_dot(hb, wn_ref[:, 2560:3072])).astype(BF16)
    sma_ref[...] = jax.nn.sigmoid(_dot(hb, wn_ref[:, 3072:4096])).astype(BF16)
    smb_ref[...] = jax.nn.sigmoid(_dot(hb, wn_ref[:, 4096:5120])).astype(BF16)


def _proj_sample(x2, scale_rows, shift_rows, wts):
    rows = x2.shape[0]
    row = lambda w: pl.BlockSpec((TM, w), lambda i: (i, 0))
    consts = [wts["wq"], wts["wn"], wts["wf"], wts["bf"], wts["gq_row"], wts["gk_row"], wts["gsum"]]
    out_shape = [jax.ShapeDtypeStruct((rows, 512), F32)] * 6 + [jax.ShapeDtypeStruct((rows, H_B), F32)] + [
        jax.ShapeDtypeStruct((rows, 512), BF16), jax.ShapeDtypeStruct((rows, 512), BF16),
        jax.ShapeDtypeStruct((rows, 1024), BF16), jax.ShapeDtypeStruct((rows, 1024), BF16)]
    out_specs = [row(512)] * 6 + [row(H_B), row(512), row(512), row(1024), row(1024)]
    return pl.pallas_call(
        _proj_sample_kernel,
        grid=(rows // TM,),
        in_specs=[row(D_MODEL), row(D_MODEL), row(D_MODEL)] + [_const_spec(c.shape) for c in consts],
        out_specs=out_specs,
        out_shape=out_shape,
        compiler_params=pltpu.CompilerParams(dimension_semantics=("arbitrary",),
                                             vmem_limit_bytes=VMEM_LIMIT),
        name="proj_sample",
    )(x2, scale_rows, shift_rows, *consts)


def _attn_kernel(shared_v, online, lam_ref, qt0_ref, qt1_ref, k0_ref, k1_ref, vt_ref, o_ref,
                 acc_ref, l_ref, p_ref, p2_ref, p3_ref, p4_ref, pd_ref):
    i = pl.program_id(2)
    qts = (qt0_ref[0], qt1_ref[0])
    k_refs = (k0_ref, k1_ref)
    dv = acc_ref.shape[1]
    n_chunks = TK // VT_CHUNK
    acc_ref[...] = jnp.zeros_like(acc_ref)

    def pv(u, chunk, p):
        blk = vt_ref[0, chunk]
        v = blk if shared_v else blk[u * dv:(u + 1) * dv, :]
        return _dot(v, p)

    def scores_to_probs(j, masked, dst_ref):
        for u in range(2):
            kblk = k_refs[u][0, pl.ds(pl.multiple_of(j * TK, TK), TK), :]
            s = _dot(kblk, qts[u])
            if masked:
                kpos = lax.broadcasted_iota(jnp.int32, s.shape, 0)
                qpos = lax.broadcasted_iota(jnp.int32, s.shape, 1)
                s = jnp.where(kpos <= qpos, s, NEG)
            p = jnp.exp(s)
            l_ref[u] += jnp.sum(p.reshape(TK // 8, 8, TQ), axis=0)
            dst_ref[u] = p.astype(BF16)

    def probs_times_values(j, src_ref):
        for u in range(2):
            upd = None
            for c in range(n_chunks):
                d = pv(u, j * n_chunks + c, src_ref[u, c * VT_CHUNK:(c + 1) * VT_CHUNK, :])
                upd = d if upd is None else upd + d
            acc_ref[u] += upd

    def step_online(j, carry, masked):
        out = []
        for u in range(2):
            kblk = k_refs[u][0, pl.ds(pl.multiple_of(j * TK, TK), TK), :]
            s = _dot(kblk, qts[u])
            if masked:
                kpos = lax.broadcasted_iota(jnp.int32, s.shape, 0)
                qpos = lax.broadcasted_iota(jnp.int32, s.shape, 1)
                s = jnp.where(kpos <= qpos, s, NEG)
            m, l = carry[u]
            m_new = jnp.maximum(m, jnp.max(s, axis=0, keepdims=True))
            alpha = jnp.exp(m - m_new)
            p = jnp.exp(s - m_new)
            l = alpha * l + jnp.sum(p, axis=0, keepdims=True)
            p = p.astype(BF16)
            upd = None
            for c in range(n_chunks):
                d = pv(u, j * n_chunks + c, p[c * VT_CHUNK:(c + 1) * VT_CHUNK, :])
                upd = d if upd is None else upd + d
            acc_ref[u] = alpha * acc_ref[u] + upd
            out.append((m_new, l))
        return tuple(out)

    if online:
        init = tuple((jnp.full((1, TQ), NEG, F32), jnp.zeros((1, TQ), F32)) for _ in range(2))
        carry = lax.fori_loop(0, i, lambda j, c: step_online(j, c, False), init)
        carry = step_online(i, carry, True)
        l0, l1 = carry[0][1], carry[1][1]
    else:
        l_ref[...] = jnp.zeros_like(l_ref)

        @pl.when(i == 0)
        def _():
            scores_to_probs(0, True, pd_ref)
            probs_times_values(0, pd_ref)

        @pl.when(i > 0)
        def _():
            scores_to_probs(0, False, p_ref)

            def pair(j):
                scores_to_probs(j, False, p2_ref)
                probs_times_values(j - 1, p_ref)
                scores_to_probs(j + 1, False, p_ref)
                probs_times_values(j, p2_ref)

            def two_pairs(jj, _):
                j = 4 * jj + 1
                scores_to_probs(j, False, p2_ref)
                probs_times_values(j - 1, p_ref)
                scores_to_probs(j + 1, False, p3_ref)
                probs_times_values(j, p2_ref)
                scores_to_probs(j + 2, False, p4_ref)
                probs_times_values(j + 1, p3_ref)
                scores_to_probs(j + 3, False, p_ref)
                probs_times_values(j + 2, p4_ref)
                return 0

            n_pairs = (i - 1) // 2
            lax.fori_loop(0, n_pairs // 2, two_pairs, 0)
            pl.when(n_pairs % 2 == 1)(lambda: pair(2 * n_pairs - 1))

            @pl.when(i % 2 == 0)
            def _():
                scores_to_probs(i - 1, False, p2_ref)
                probs_times_values(i - 2, p_ref)
                scores_to_probs(i, True, pd_ref)
                probs_times_values(i - 1, p2_ref)
                probs_times_values(i, pd_ref)

            @pl.when(i % 2 == 1)
            def _():
                scores_to_probs(i, True, pd_ref)
                probs_times_values(i - 1, p_ref)
                probs_times_values(i, pd_ref)

        l0, l1 = (jnp.sum(l_ref[u], axis=0, keepdims=True) for u in range(2))

    o0 = acc_ref[0] * (1.0 / l0)
    o1 = acc_ref[1] * (1.0 / l1)
    o = o0 - lam_ref[0] * o1 if shared_v else jnp.concatenate([o0, o1], axis=0)
    o_ref[0] = o.T


def _attention(qt, kaug, vt, lam, shared_v, online):
    batch, _, seq = qt.shape
    n_steps = 4
    if shared_v:
        g0 = lambda h: h
        g1 = lambda h: H_A + h
    else:
        g0 = lambda h: 2 * h
        g1 = lambda h: 2 * h + 1
    dv = DV_A if shared_v else DH_B
    qspec = lambda g: pl.BlockSpec((1, LANES, TQ), lambda b, h, i: (b, g(h), i))
    kspec = lambda g: pl.BlockSpec((1, seq, LANES), lambda b, h, i: (b, 0, g(h)))
    return pl.pallas_call(
        functools.partial(_attn_kernel, shared_v, online),
        grid=(batch, n_steps, seq // TQ),
        in_specs=[pl.BlockSpec(memory_space=pltpu.SMEM), qspec(g0), qspec(g1), kspec(g0), kspec(g1),
                  pl.BlockSpec((1, seq // VT_CHUNK, LANES, VT_CHUNK), lambda b, h, i: (b, 0, h, 0))],
        out_specs=pl.BlockSpec((1, TQ, LANES), lambda b, h, i: (b, i, h)),
        out_shape=jax.ShapeDtypeStruct((batch, seq, 512), F32),
        scratch_shapes=[pltpu.VMEM((2, dv, TQ), F32), pltpu.VMEM((2, 8, TQ), F32),
                        ] + [pltpu.VMEM((2, TK, TQ), BF16)] * 5,
        compiler_params=pltpu.CompilerParams(
            dimension_semantics=("arbitrary", "arbitrary", "arbitrary"), vmem_limit_bytes=VMEM_LIMIT),
        name=("attn_a" if shared_v else "attn_b") + ("_online" if online else ""),
    )(lam, qt, qt, kaug, kaug, vt)


def _decode_kernel(past, pps, pt_ref, lam_ref, qa_ref, qb_ref,
                   kta_hbm, va_hbm, ktb_hbm, vtb_hbm, lft_hbm,
                   kan_ref, van_ref, kbn_ref, vbn_ref, lfn_ref, upper_ref, oa_ref, ob_ref,
                   qbd_ref, m_ref, l_ref, acca_ref, accb_ref, fc_ref,
                   kta_buf, va_buf, ktb_buf, vtb_buf, lft_buf, sem):
    step_i = pl.program_id(1)
    n_steps = pl.num_programs(1)
    g = pl.program_id(0) * n_steps + step_i
    slot = g % 2
    pools = ((kta_hbm, kta_buf), (va_hbm, va_buf), (ktb_hbm, ktb_buf), (vtb_hbm, vtb_buf), (lft_hbm, lft_buf))

    def page_copies(gg, dst_slot):
        out = []
        for j in range(pps):
            pg = pt_ref[gg * pps + j]
            for k, (hbm, buf) in enumerate(pools):
                out.append(pltpu.make_async_copy(hbm.at[pg], buf.at[dst_slot, j], sem.at[dst_slot, k]))
        return out

    @pl.when(g == 0)
    def _():
        for c in page_copies(0, 0):
            c.start()

    @pl.when(g + 1 < pl.num_programs(0) * n_steps)
    def _():
        for c in page_copies(g + 1, 1 - slot):
            c.start()

    for c in page_copies(g, slot):
        c.wait()
    kta_refs, va_refs, ktb_refs, vtb_refs, lft_refs = (
        [buf.at[slot, j] for j in range(pps)] for _, buf in pools)
    rows = 32
    n_tok = 4
    pair_rows = kan_ref.shape[0]
    off = (pl.program_id(0) % 2) * n_tok
    page = LANES

    def row_ids(shape):
        r = lax.broadcasted_iota(jnp.int32, shape, 0)
        return ((r & 3, ((r >> 2) & 1) * H_A + (r >> 3), r >> 3),
                (r >> 3, r & 7, r & 7))

    def row_slope(head):
        return jnp.where(head == 0, SLOPES[0], jnp.where(head == 1, SLOPES[1],
               jnp.where(head == 2, SLOPES[2], SLOPES[3])))

    @pl.when(step_i == 0)
    def _():
        col_group = lax.broadcasted_iota(jnp.int32, (rows, 512), 1) >> 6
        for u, q_ref in enumerate((qa_ref, qb_ref)):
            tok, group, _ = row_ids((rows, 512))[u]
            qbd = jnp.zeros((rows, 512), F32)
            for t in range(n_tok):
                q_row = jnp.broadcast_to(q_ref[pl.ds(off + t, 1), :], (rows, 512))
                qbd = jnp.where((tok == t) & (group == col_group), q_row, qbd)
            qbd_ref[u] = qbd.astype(BF16)
        m_ref[...] = jnp.full(m_ref.shape, NEG, F32)
        l_ref[...] = jnp.zeros_like(l_ref)
        acca_ref[...] = jnp.zeros_like(acca_ref)
        accb_ref[...] = jnp.zeros_like(accb_ref)
        fc_ref[...] = jnp.zeros_like(fc_ref)

    def softmax_update(u, s):
        m_prev = m_ref[u][:, 0:1]
        l_prev = l_ref[u][:, 0:1]
        m_new = jnp.maximum(m_prev, jnp.max(s, axis=1, keepdims=True))
        alpha = jnp.exp(m_prev - m_new)
        pr = jnp.exp(s - m_new)
        l_new = alpha * l_prev + jnp.sum(pr, axis=1, keepdims=True)
        m_ref[u] = jnp.broadcast_to(m_new, (rows, LANES))
        l_ref[u] = jnp.broadcast_to(l_new, (rows, LANES))
        return alpha, pr

    is_last = step_i == n_steps - 1
    width = pps * page
    lane = lax.broadcasted_iota(jnp.int32, (rows, width), 1)
    tn = lax.broadcasted_iota(jnp.int32, (rows, page), 1)
    (tok_a, _, head_a), (tok_b, _, _) = row_ids((rows, page))
    pad_rows = lambda ref: jnp.concatenate(
        [ref[...], jnp.zeros((page - pair_rows, 512), F32)], axis=0).astype(BF16)

    def valid(tok):
        return is_last & (tn >= off) & (tn - off <= tok) & (tn < off + n_tok)

    dist = (past + row_ids((rows, width))[0][0] - (step_i * width + lane)).astype(F32)
    s_a = _dot(qbd_ref[0], jnp.concatenate([r[...].astype(BF16) for r in kta_refs], axis=1))
    s_a = s_a - row_slope(row_ids((rows, width))[0][2]) * dist
    s_an = _dot_nt(qbd_ref[0], pad_rows(kan_ref)) - row_slope(head_a) * (tok_a - (tn - off)).astype(F32)
    alpha, pr = softmax_update(0, jnp.concatenate([s_a, jnp.where(valid(tok_a), s_an, NEG)], axis=1))
    pr = pr.astype(BF16)
    v_all = jnp.concatenate(
        [jnp.concatenate([r[pl.ds(h, page, stride=H_A), :].astype(BF16) for h in range(H_A)], axis=1)
         for r in va_refs], axis=0)
    acca_ref[...] = (alpha * acca_ref[...] + _dot(pr[:, 0:width], v_all)
                     + _dot(pr[:, width:], pad_rows(van_ref)))

    parts = []
    for j in range(pps):
        parts.extend(v.astype(F32) for v in _split3(lft_refs[j][...]))
        parts.append(jnp.zeros((8, page), F32))
    cs = _dot(jnp.concatenate(parts, axis=0).astype(BF16), upper_ref[...])
    f_carry = fc_ref[...][:, 0:1]
    f_pages = []
    for j in range(pps):
        local = cs[32 * j:32 * j + 8] + cs[32 * j + 8:32 * j + 16] + cs[32 * j + 16:32 * j + 24]
        f_page = local + f_carry
        f_carry = f_carry + local[:, page - 1:page]
        f_pages.append(jnp.concatenate([f_page] * n_tok, axis=0))
    fc_ref[...] = jnp.broadcast_to(f_carry, (8, LANES))
    lfn = lfn_ref[0]
    t8 = lax.broadcasted_iota(jnp.int32, (8, page), 1)
    f_new = jnp.broadcast_to(f_carry, (8, page))
    for t in range(n_tok):
        lf_t = jnp.sum(jnp.where(t8 == off + t, lfn, 0.0), axis=1, keepdims=True)
        f_new = f_new + jnp.where(t8 >= off + t, lf_t, 0.0)
    s_b = _dot(qbd_ref[1], jnp.concatenate([r[...].astype(BF16) for r in ktb_refs], axis=1))
    s_b = s_b - jnp.concatenate(f_pages, axis=1)
    s_bn = _dot_nt(qbd_ref[1], pad_rows(kbn_ref)) - jnp.concatenate([f_new] * n_tok, axis=0)
    alpha, pr = softmax_update(1, jnp.concatenate([s_b, jnp.where(valid(tok_b), s_bn, NEG)], axis=1))
    pr = pr.astype(BF16)
    vt_all = jnp.concatenate([r[...].astype(BF16) for r in vtb_refs], axis=1)
    accb_ref[...] = (alpha * accb_ref[...] + _dot_nt(pr[:, 0:width], vt_all)
                     + _dot(pr[:, width:], pad_rows(vbn_ref)))

    @pl.when(is_last)
    def _():
        on_a = acca_ref[...] * (1.0 / l_ref[0][:, 0:1])
        head_b = row_ids((rows, 512))[1][2]
        col_head = lax.broadcasted_iota(jnp.int32, (rows, 512), 1) >> 6
        on_b = jnp.where(head_b == col_head, accb_ref[...] * (1.0 / l_ref[1][:, 0:1]), 0.0)

        def write(first):
            for h in range(H_A):
                blk = on_a[8 * h:8 * h + 8, h * DV_A:(h + 1) * DV_A]
                oa_ref[first:first + n_tok, h * DV_A:(h + 1) * DV_A] = (
                    blk[0:n_tok] - lam_ref[0] * blk[n_tok:2 * n_tok])
            for t in range(n_tok):
                ob_ref[first + t:first + t + 1, :] = jnp.sum(on_b[8 * t:8 * t + 8], axis=0, keepdims=True)

        pl.when(off == 0)(lambda: write(0))
        pl.when(off != 0)(lambda: write(n_tok))


def _decode(page_table, lam, qa, qb, pools, news, lfn_t, upper):
    n_seq, n_pages = page_table.shape
    page = pools[0].shape[2]
    pps = PAGES_PER_STEP
    n_rows = qa.shape[0]
    past = n_pages * page
    pair_spec = pl.BlockSpec((2 * (n_rows // n_seq), 512), lambda n, s, pt: (n // 2, 0))
    page_bufs = [pltpu.VMEM((2, pps) + pool.shape[1:], F32) for pool in pools]
    grid_spec = pltpu.PrefetchScalarGridSpec(
        num_scalar_prefetch=1,
        grid=(n_seq, n_pages // pps),
        in_specs=[pl.BlockSpec(memory_space=pltpu.SMEM), pair_spec, pair_spec]
                 + [pl.BlockSpec(memory_space=pl.ANY)] * len(pools)
                 + [pair_spec] * 4
                 + [pl.BlockSpec((1, H_B, page), lambda n, s, pt: (n // 2, 0, 0)),
                    pl.BlockSpec((page, page), lambda n, s, pt: (0, 0))],
        out_specs=[pair_spec, pair_spec],
        scratch_shapes=[pltpu.VMEM((2, 32, 512), BF16), pltpu.VMEM((2, 32, LANES), F32),
                        pltpu.VMEM((2, 32, LANES), F32), pltpu.VMEM((32, 512), F32),
                        pltpu.VMEM((32, 512), F32), pltpu.VMEM((H_B, LANES), F32)]
                       + page_bufs + [pltpu.SemaphoreType.DMA((2, len(pools)))],
    )
    return pl.pallas_call(
        functools.partial(_decode_kernel, past, pps),
        grid_spec=grid_spec,
        out_shape=[jax.ShapeDtypeStruct((n_rows, 512), F32)] * 2,
        compiler_params=pltpu.CompilerParams(dimension_semantics=("arbitrary", "arbitrary"),
                                             vmem_limit_bytes=VMEM_LIMIT),
        name="decode",
    )(page_table.reshape(-1), lam, qa, qb, *pools, *news, lfn_t, upper)


def _merge_kernel(x_ref, oa_ref, ob_ref, sga_ref, sgb_ref, sma_ref, smb_ref, gres_ref, gsub_ref,
                  woa_ref, wob_ref, wo_ref, y_ref):
    oa = oa_ref[...]
    heads = []
    for h in range(H_A):
        oh = oa[:, h * DV_A:(h + 1) * DV_A]
        heads.append(oh * lax.rsqrt(jnp.mean(oh * oh, axis=-1, keepdims=True) + EPS))
    oa = jnp.concatenate(heads, axis=1) * gsub_ref[...]
    ya = _dot((oa * sga_ref[...].astype(F32)).astype(BF16), woa_ref[...])
    yb = _dot((ob_ref[...] * sgb_ref[...].astype(F32)).astype(BF16), wob_ref[...])
    mix = sma_ref[...].astype(F32) * ya + smb_ref[...].astype(F32) * yb
    y = _dot(mix.astype(BF16), wo_ref[...])
    y_ref[...] = x_ref[...] + gres_ref[...] * y


def _merge(x2, oa, ob, sga, sgb, sma, smb, gres, gres_spec, wts):
    rows = x2.shape[0]
    row = lambda w: pl.BlockSpec((TM_MERGE, w), lambda i: (i, 0))
    consts = [wts["gsub"], wts["woa"], wts["wob"], wts["wo"]]
    return pl.pallas_call(
        _merge_kernel,
        grid=(rows // TM_MERGE,),
        in_specs=[row(D_MODEL), row(512), row(512), row(512), row(512), row(1024), row(1024), gres_spec]
                 + [_const_spec(c.shape) for c in consts],
        out_specs=row(D_MODEL),
        out_shape=jax.ShapeDtypeStruct((rows, D_MODEL), F32),
        compiler_params=pltpu.CompilerParams(dimension_semantics=("arbitrary",),
                                             vmem_limit_bytes=VMEM_LIMIT),
        name="merge",
    )(x2, oa, ob, sga, sgb, sma, smb, gres, *consts)


def _ones_constants():
    gsum = np.kron(np.eye(N_GROUPS), np.ones((DH_A, DH_A)))
    tri = np.tril(np.ones((TM, TM)))
    upper = np.triu(np.ones((LANES, LANES)))
    bf = lambda a: jnp.asarray(a, BF16)
    return dict(gsum=bf(gsum), tri=bf(tri), upper=bf(upper))


def _bf16_ceil(x):
    y = x.astype(BF16).astype(F32)
    return jnp.where(y < x, y * (1.0 + 2.0 ** -7), y)


def kernel(x_prompt, x_sample, cache_a_k, cache_a_v, cache_b_k, cache_b_v, cache_b_logf, page_table,
           c_prompt, c_sample, w_ada, b_ada, w_in, b_f, g_q_a, g_k_a, g_q_b, g_k_b,
           lambda_q1, lambda_k1, lambda_q2, lambda_k2, g_sub_a, w_out_a, w_out_b, w_o):
    assert w_ada.shape[0] == 1, "single-layer step"
    batch, seq, _ = x_prompt.shape
    n_seq, n_tok, _ = x_sample.shape
    n_pool, page = cache_a_k.shape[1], cache_a_k.shape[2]
    layer = 0
    lam_init = 0.8 - 0.6 * math.exp(-0.3 * layer)
    lam = (jnp.exp(jnp.sum(lambda_q1[layer] * lambda_k1[layer]))
           - jnp.exp(jnp.sum(lambda_q2[layer] * lambda_k2[layer])) + lam_init).reshape(1).astype(F32)

    w = w_in[layer]
    sec = np.cumsum((0, 512, 512, 512, 512, 512, 512, 512, H_B, 512, 1024, 1024))
    cols = lambda k: w[:, sec[k]:sec[k + 1]]
    wq = jnp.concatenate([cols(0), cols(4)], axis=1).astype(BF16)
    wn = jnp.concatenate([cols(1), cols(5), cols(2), cols(6), cols(3), cols(8), cols(9), cols(10)],
                         axis=1).astype(BF16)
    wf = jnp.pad(cols(7), ((0, 0), (0, LANES - H_B))).astype(BF16)
    gq = jnp.concatenate([jnp.tile(g_q_a[layer], N_GROUPS) * DH_A ** -0.5,
                          jnp.tile(g_q_b[layer], N_GROUPS) * DH_B ** -0.5])
    gk = jnp.concatenate([jnp.tile(g_k_a[layer], N_GROUPS), jnp.tile(g_k_b[layer], N_GROUPS)])
    wts = dict(_ones_constants(),
               wq=wq, wqkt=jnp.concatenate([wq, wn[:, 0:1024], wn[:, 1536:2048]], axis=1).T,
               wn=wn, wn_prompt=jnp.concatenate([wn[:, 1024:1536], wn[:, 2048:]], axis=1),

               wf=wf, bf=jnp.pad(b_f[layer], (0, LANES - H_B)).reshape(1, LANES),
               gqk_col=jnp.concatenate([gq, gk]).reshape(-1, 1), gq_row=gq.reshape(1, -1), gk_row=gk.reshape(1, -1),
               gsub=(jnp.tile(g_sub_a[layer], H_A) * (1.0 - lam_init)).reshape(1, W_A),
               woa=w_out_a[layer].astype(BF16), wob=w_out_b[layer].astype(BF16), wo=w_o[layer].astype(BF16))
    bound = lambda gq_, gk_: _bf16_ceil(8.1 * jnp.max(jnp.abs(gq_ * gk_)))
    mshift = jnp.stack([bound(g_q_a[layer], g_k_a[layer]), bound(g_q_b[layer], g_k_b[layer])]).astype(F32)

    n_c = batch + n_seq
    c_all = jnp.pad(jnp.concatenate([c_prompt, c_sample], axis=0), ((0, -n_c % 8), (0, 0)))
    mod = _ada(c_all, w_ada[layer], b_ada[layer])
    shift, scale, gres = (mod[:, k * D_MODEL:(k + 1) * D_MODEL] for k in range(3))

    xp2 = x_prompt.reshape(batch * seq, D_MODEL)
    p3 = lambda a: a[:batch].reshape(batch, 1, D_MODEL)
    (kta, va, ktb, vtbo, lft, qta, kaa, vta, qtb, kab, vtb, sga, sgb, sma, smb) = _proj_prompt(
        xp2, p3(scale), p3(shift), mshift, wts, batch, seq)
    online = jnp.max(mshift) > FAST_PATH_MAX_BOUND
    attend = lambda *a: lax.cond(online, lambda: _attention(*a, True), lambda: _attention(*a, False))
    oa = attend(qta, kaa, vta, lam, True).reshape(batch * seq, W_A)
    ob = attend(qtb, kab, vtb, lam, False).reshape(batch * seq, W_B)
    nb = seq // TM_MERGE
    gres_p_spec = pl.BlockSpec((None, 1, D_MODEL), lambda i: (i // nb, 0, 0))
    yp = _merge(xp2, oa, ob, sga, sgb, sma, smb, p3(gres), gres_p_spec, wts).reshape(batch, seq, D_MODEL)

    xs2 = x_sample.reshape(n_seq * n_tok, D_MODEL)
    rep = lambda a: jnp.repeat(a[batch:n_c], n_tok, axis=0)
    (qa_s, qb_s, ka_s, va_s, kb_s, vb_s, lf_s, sga_s, sgb_s, sma_s, smb_s) = _proj_sample(
        xs2, rep(scale), rep(shift), wts)
    pools = [jnp.transpose(cache_a_k[layer], (0, 2, 3, 4, 1)).reshape(n_pool, 512, page),
             cache_a_v[layer].reshape(n_pool, page * H_A, DV_A),
             jnp.transpose(cache_b_k[layer], (0, 2, 3, 1)).reshape(n_pool, 512, page),
             jnp.transpose(cache_b_v[layer], (0, 2, 3, 1)).reshape(n_pool, 512, page),
             jnp.swapaxes(cache_b_logf[layer], 1, 2)]
    lfn_t = jnp.pad(jnp.swapaxes(lf_s.reshape(n_seq // 2, 2 * n_tok, H_B), 1, 2),
                    ((0, 0), (0, 0), (0, page - 2 * n_tok)))
    oa_s, ob_s = _decode(page_table, lam, qa_s, qb_s, pools, [ka_s, va_s, kb_s, vb_s], lfn_t, wts["upper"])
    gres_s_spec = pl.BlockSpec((TM_MERGE, D_MODEL), lambda i: (i, 0))
    ys = _merge(xs2, oa_s, ob_s, sga_s, sgb_s, sma_s, smb_s,
                rep(gres), gres_s_spec, wts).reshape(n_seq, n_tok, D_MODEL)

    return (yp, ys,
            jnp.moveaxis(kta.reshape(1, batch, 2, H_A, DH_A, seq), -1, 2), va.reshape(1, batch, seq, H_A, DV_A),
            jnp.moveaxis(ktb.reshape(1, batch, H_B, DH_B, seq), -1, 2),
            jnp.moveaxis(vtbo.reshape(1, batch, H_B, DH_B, seq), -1, 2),
            jnp.moveaxis(lft.reshape(1, batch, H_B, seq), -1, 2),
            ka_s.reshape(1, n_seq, n_tok, 2, H_A, DH_A), va_s.reshape(1, n_seq, n_tok, H_A, DV_A),
            kb_s.reshape(1, n_seq, n_tok, H_B, DH_B), vb_s.reshape(1, n_seq, n_tok, H_B, DH_B),
            lf_s.reshape(1, n_seq, n_tok, H_B))
```

```python
import functools
import math

import numpy as np
import jax
import jax.numpy as jnp
from jax import lax
from jax.experimental import pallas as pl
from jax.experimental.pallas import tpu as pltpu

F32 = jnp.float32
BF16 = jnp.bfloat16

D_MODEL = 1024
H_A, DH_A, DV_A = 4, 64, 128
H_B, DH_B = 8, 64
W_A = H_A * DV_A
W_B = H_B * DH_B
N_GROUPS = 8
EPS = 1e-6
NEG = -1e30
SLOPES = tuple(2.0 ** (-8.0 * (h + 1) / H_A) for h in range(H_A))
LANES = 128

TM = 256
TM_MERGE = 512
TQ = 512
TK = 512
VT_CHUNK = TM
PAGES_PER_STEP = 8
FAST_PATH_MAX_BOUND = 30.0
VMEM_LIMIT = 56 * 1024 * 1024


def _dot(a, b):
    return jnp.dot(a, b, preferred_element_type=F32)


def _dot_nt(a, b):
    return lax.dot_general(a, b, (((1,), (1,)), ((), ())), preferred_element_type=F32)


def _split3(x):
    hi = x.astype(BF16)
    r1 = x - hi.astype(F32)
    mid = r1.astype(BF16)
    lo = (r1 - mid.astype(F32)).astype(BF16)
    return hi, mid, lo


def _silu(x):
    return x * jax.nn.sigmoid(x)


def _const_spec(shape):
    nd = len(shape)
    return pl.BlockSpec(shape, lambda *_: (0,) * nd, pipeline_mode=pl.Buffered(1))


def _ada_kernel(c_ref, w_ref, b_ref, o_ref):
    c = c_ref[...]
    o_ref[...] = _dot(_silu(c).astype(BF16), w_ref[...].astype(BF16)) + b_ref[...]


def _ada(c_all, w_ada, b_ada):
    rows = c_all.shape[0]
    n = w_ada.shape[1]
    bn = 1024
    return pl.pallas_call(
        _ada_kernel,
        grid=(n // bn,),
        in_specs=[pl.BlockSpec((rows, D_MODEL), lambda j: (0, 0)),
                  pl.BlockSpec((D_MODEL, bn), lambda j: (0, j)),
                  pl.BlockSpec((1, bn), lambda j: (0, j))],
        out_specs=pl.BlockSpec((rows, bn), lambda j: (0, j)),
        out_shape=jax.ShapeDtypeStruct((rows, n), F32),
        name="ada",
    )(c_all, w_ada, b_ada.reshape(1, n))


def _hidden(x_ref, scale_ref, shift_ref):
    x = x_ref[...]
    ms = jnp.mean(x * x, axis=-1, keepdims=True)
    h = x * lax.rsqrt(ms + EPS) * (1.0 + scale_ref[...]) + shift_ref[...]
    return h.astype(BF16)


def _group_rms(z, g_ref):
    zz = z * z
    hi = zz.astype(BF16)
    lo = (zz - hi.astype(F32)).astype(BF16)
    ss = _dot(hi, g_ref[...]) + _dot(lo, g_ref[...])
    return z * lax.rsqrt(ss * (1.0 / DH_A) + EPS)


def _log_sigmoid(z):
    return jnp.minimum(z, 0.0) - jnp.log1p(jnp.exp(-jnp.abs(z)))


def _logf(hb, wf_ref, bf_ref):
    z = _dot(hb, wf_ref[...]) + bf_ref[...]
    lane = lax.broadcasted_iota(jnp.int32, z.shape, 1)
    return jnp.where(lane < H_B, _log_sigmoid(z), 0.0)


def _proj_prompt_kernel(nb, mshift_ref, x_ref, scale_ref, shift_ref, wqkt_ref, wn_ref, wf_ref, bf_ref,
                        gqk_ref, tri_ref,
                        kta_ref, va_ref, ktb_ref, vtbo_ref, lft_ref,
                        qta_ref, kaa_ref, vta_ref, qtb_ref, kab_ref, vtb_ref,
                        sga_ref, sgb_ref, sma_ref, smb_ref, carry_ref):
    tm = x_ref.shape[0]
    ib = pl.program_id(0) % nb
    hb = _hidden(x_ref, scale_ref, shift_ref)

    lf = _logf(hb, wf_ref, bf_ref)
    lft_ref[0] = lf.T[0:H_B, :]
    hi, mid, lo = _split3(lf)
    tri = tri_ref[...]
    f_local = _dot(tri, hi) + _dot(tri, mid) + _dot(tri, lo)

    @pl.when(ib == 0)
    def _():
        carry_ref[...] = jnp.zeros_like(carry_ref)

    f_nat = f_local + carry_ref[...]
    carry_ref[...] = f_nat[tm - 1:tm, :]
    f_t = f_nat.T[0:H_B, :]

    n_g = 4 * N_GROUPS
    n_qk = n_g * DH_A
    qkt = jnp.concatenate([_dot_nt(wqkt_ref[0:n_qk // 2, :], hb),
                           _dot_nt(wqkt_ref[n_qk // 2:n_qk, :], hb)], axis=0)
    g3 = qkt.reshape(n_g, DH_A, tm)
    ssq = jnp.sum(g3 * g3, axis=1, keepdims=True)
    g3 = (qkt * gqk_ref[...]).reshape(n_g, DH_A, tm) * lax.rsqrt(ssq * (1.0 / DH_A) + EPS)
    kta_ref[0] = g3[2 * N_GROUPS:3 * N_GROUPS].reshape(N_GROUPS * DH_A, tm)
    ktb_ref[0] = g3[3 * N_GROUPS:4 * N_GROUPS].reshape(N_GROUPS * DH_A, tm)

    r8 = lax.broadcasted_iota(jnp.int32, (8, tm), 0)
    pos = ib * tm + lax.broadcasted_iota(jnp.int32, (8, tm), 1)
    pos_hi = (pos >> 7).astype(F32)
    pos_lo = (pos & 127).astype(F32)
    zeros_tail = jnp.zeros((LANES - DH_A - 8, tm), F32)
    m_a = mshift_ref[0]
    m_b = mshift_ref[1]

    def operand(group, aug):
        return jnp.concatenate([g3[group], aug, zeros_tail], axis=0)

    k_aug_a = jnp.where(r8 == 2, pos_hi, jnp.where(r8 == 3, pos_lo,
              jnp.where((r8 <= 1) | (r8 == 4), 1.0, 0.0)))
    for g in range(N_GROUPS):
        slope = SLOPES[g % H_A]
        q_aug = jnp.where(r8 == 0, -slope * 128.0 * pos_hi,
                jnp.where(r8 == 1, -slope * pos_lo,
                jnp.where(r8 == 2, slope * 128.0,
                jnp.where(r8 == 3, slope,
                jnp.where(r8 == 4, -m_a, 0.0)))))
        qta_ref[0, g * LANES:(g + 1) * LANES, :] = operand(g, q_aug).astype(BF16)
        kaa_ref[0, :, g * LANES:(g + 1) * LANES] = operand(2 * N_GROUPS + g, k_aug_a).T.astype(BF16)
    ft_hi, ft_mid, ft_lo = _split3(f_t)
    for g in range(N_GROUPS):
        bc = lambda v: jnp.broadcast_to(v[g:g + 1, :].astype(F32), (8, tm))
        q_aug = jnp.where(r8 == 0, bc(ft_hi),
                jnp.where(r8 == 1, bc(ft_mid),
                jnp.where(r8 == 2, bc(ft_lo),
                jnp.where(r8 <= 5, -1.0,
                jnp.where(r8 == 6, -m_b, 0.0)))))
        k_aug = jnp.where(r8 <= 2, 1.0,
                jnp.where(r8 == 3, bc(ft_hi),
                jnp.where(r8 == 4, bc(ft_mid),
                jnp.where(r8 == 5, bc(ft_lo),
                jnp.where(r8 == 6, 1.0, 0.0)))))
        qtb_ref[0, g * LANES:(g + 1) * LANES, :] = operand(N_GROUPS + g, q_aug).astype(BF16)
        kab_ref[0, :, g * LANES:(g + 1) * LANES] = operand(3 * N_GROUPS + g, k_aug).T.astype(BF16)

    va = _dot(hb, wn_ref[:, 0:512])
    for h in range(H_A):
        va_ref[pl.ds(h, tm, stride=H_A), :] = va[:, h * DV_A:(h + 1) * DV_A]
    vta_ref[0, 0] = va.T.astype(BF16)
    vbt = _dot_nt(wqkt_ref[n_qk:, :], hb)
    vtbo_ref[0] = vbt
    vtb_ref[0, 0] = vbt.astype(BF16)

    sga_ref[...] = _silu(_dot(hb, wn_ref[:, 512:1024])).astype(BF16)
    sgb_ref[...] = _silu(_dot(hb, wn_ref[:, 1024:1536])).astype(BF16)
    sma_ref[...] = jax.nn.sigmoid(_dot(hb, wn_ref[:, 1536:2560])).astype(BF16)
    smb_ref[...] = jax.nn.sigmoid(_dot(hb, wn_ref[:, 2560:3584])).astype(BF16)


def _proj_prompt(x2, scale, shift, mshift, wts, batch, seq):
    rows = batch * seq
    nb = seq // TM
    nk = seq // VT_CHUNK
    row = lambda w: pl.BlockSpec((TM, w), lambda i: (i, 0))
    mod = pl.BlockSpec((None, 1, D_MODEL), lambda i: (i // nb, 0, 0))
    consts = [wts["wqkt"], wts["wn_prompt"], wts["wf"], wts["bf"], wts["gqk_col"], wts["tri"]]
    seq_t = lambda r: jax.ShapeDtypeStruct((batch, r, seq), F32)
    out_shape = [seq_t(512), jax.ShapeDtypeStruct((rows * H_A, DV_A), F32), seq_t(512), seq_t(512),
                 seq_t(H_B)] + [
        jax.ShapeDtypeStruct((batch, 1024, seq), BF16), jax.ShapeDtypeStruct((batch, seq, 1024), BF16),
        jax.ShapeDtypeStruct((batch, nk, W_A, VT_CHUNK), BF16),
        jax.ShapeDtypeStruct((batch, 1024, seq), BF16), jax.ShapeDtypeStruct((batch, seq, 1024), BF16),
        jax.ShapeDtypeStruct((batch, nk, W_B, VT_CHUNK), BF16),
        jax.ShapeDtypeStruct((rows, 512), BF16), jax.ShapeDtypeStruct((rows, 512), BF16),
        jax.ShapeDtypeStruct((rows, 1024), BF16), jax.ShapeDtypeStruct((rows, 1024), BF16)]
    qt_spec = pl.BlockSpec((1, 1024, TM), lambda i: (i // nb, 0, i % nb))
    k_spec = pl.BlockSpec((1, TM, 1024), lambda i: (i // nb, i % nb, 0))
    vt_spec = pl.BlockSpec((1, TM // VT_CHUNK, 512, VT_CHUNK), lambda i: (i // nb, i % nb, 0, 0))
    out_t = lambda r: pl.BlockSpec((1, r, TM), lambda i: (i // nb, 0, i % nb))
    out_specs = [out_t(512), pl.BlockSpec((TM * H_A, DV_A), lambda i: (i, 0)), out_t(512), out_t(512),
                 out_t(H_B)] + [
        qt_spec, k_spec, vt_spec, qt_spec, k_spec, vt_spec, row(512), row(512), row(1024), row(1024)]
    return pl.pallas_call(
        functools.partial(_proj_prompt_kernel, nb),
        grid=(rows // TM,),
        in_specs=[pl.BlockSpec(memory_space=pltpu.SMEM), row(D_MODEL), mod, mod]
                 + [_const_spec(c.shape) for c in consts],
        out_specs=out_specs,
        out_shape=out_shape,
        scratch_shapes=[pltpu.VMEM((1, LANES), F32)],
        compiler_params=pltpu.CompilerParams(dimension_semantics=("arbitrary",),
                                             vmem_limit_bytes=VMEM_LIMIT),
        name="proj_prompt",
    )(mshift, x2, scale, shift, *consts)


def _proj_sample_kernel(x_ref, scale_ref, shift_ref, wq_ref, wn_ref, wf_ref, bf_ref, gq_ref, gk_ref, g_ref,
                        qa_ref, qb_ref, ka_ref, va_ref, kb_ref, vb_ref, lf_ref,
                        sga_ref, sgb_ref, sma_ref, smb_ref):
    hb = _hidden(x_ref, scale_ref, shift_ref)
    lf_ref[...] = _logf(hb, wf_ref, bf_ref)[:, :H_B]
    qa_ref[...] = _group_rms(_dot(hb, wq_ref[:, 0:512]), g_ref) * gq_ref[:, 0:512]
    qb_ref[...] = _group_rms(_dot(hb, wq_ref[:, 512:1024]), g_ref) * gq_ref[:, 512:1024]
    ka_ref[...] = _group_rms(_dot(hb, wn_ref[:, 0:512]), g_ref) * gk_ref[:, 0:512]
    kb_ref[...] = _group_rms(_dot(hb, wn_ref[:, 512:1024]), g_ref) * gk_ref[:, 512:1024]
    va_ref[...] = _dot(hb, wn_ref[:, 1024:1536])
    vb_ref[...] = _dot(hb, wn_ref[:, 1536:2048])
    sga_ref[...] = _silu(_dot(hb, wn_ref[:, 2048:2560])).astype(BF16)
    sgb_ref[...] = _silu(_dot(hb, wn_ref[:, 2560:3072])).astype(BF16)
    sma_ref[...] = jax.nn.sigmoid(_dot(hb, wn_ref[:, 3072:4096])).astype(BF16)
    smb_ref[...] = jax.nn.sigmoid(_dot(hb, wn_ref[:, 4096:5120])).astype(BF16)


def _proj_sample(x2, scale_rows, shift_rows, wts):
    rows = x2.shape[0]
    row = lambda w: pl.BlockSpec((TM, w), lambda i: (i, 0))
    consts = [wts["wq"], wts["wn"], wts["wf"], wts["bf"], wts["gq_row"], wts["gk_row"], wts["gsum"]]
    out_shape = [jax.ShapeDtypeStruct((rows, 512), F32)] * 6 + [jax.ShapeDtypeStruct((rows, H_B), F32)] + [
        jax.ShapeDtypeStruct((rows, 512), BF16), jax.ShapeDtypeStruct((rows, 512), BF16),
        jax.ShapeDtypeStruct((rows, 1024), BF16), jax.ShapeDtypeStruct((rows, 1024), BF16)]
    out_specs = [row(512)] * 6 + [row(H_B), row(512), row(512), row(1024), row(1024)]
    return pl.pallas_call(
        _proj_sample_kernel,
        grid=(rows // TM,),
        in_specs=[row(D_MODEL), row(D_MODEL), row(D_MODEL)] + [_const_spec(c.shape) for c in consts],
        out_specs=out_specs,
        out_shape=out_shape,
        compiler_params=pltpu.CompilerParams(dimension_semantics=("arbitrary",),
                                             vmem_limit_bytes=VMEM_LIMIT),
        name="proj_sample",
    )(x2, scale_rows, shift_rows, *consts)


def _attn_kernel(shared_v, online, lam_ref, qt0_ref, qt1_ref, k0_ref, k1_ref, vt_ref, o_ref,
                 acc_ref, l_ref, p_ref, p2_ref, p3_ref, p4_ref, pd_ref):
    i = pl.program_id(2)
    qts = (qt0_ref[0], qt1_ref[0])
    k_refs = (k0_ref, k1_ref)
    dv = acc_ref.shape[1]
    n_chunks = TK // VT_CHUNK
    acc_ref[...] = jnp.zeros_like(acc_ref)

    def pv(u, chunk, p):
        blk = vt_ref[0, chunk]
        v = blk if shared_v else blk[u * dv:(u + 1) * dv, :]
        return _dot(v, p)

    def scores_to_probs(j, masked, dst_ref):
        for u in range(2):
            kblk = k_refs[u][0, pl.ds(pl.multiple_of(j * TK, TK), TK), :]
            s = _dot(kblk, qts[u])
            if masked:
                kpos = lax.broadcasted_iota(jnp.int32, s.shape, 0)
                qpos = lax.broadcasted_iota(jnp.int32, s.shape, 1)
                s = jnp.where(kpos <= qpos, s, NEG)
            p = jnp.exp(s)
            l_ref[u] += jnp.sum(p.reshape(TK // 8, 8, TQ), axis=0)
            dst_ref[u] = p.astype(BF16)

    def probs_times_values(j, src_ref):
        for u in range(2):
            upd = None
            for c in range(n_chunks):
                d = pv(u, j * n_chunks + c, src_ref[u, c * VT_CHUNK:(c + 1) * VT_CHUNK, :])
                upd = d if upd is None else upd + d
            acc_ref[u] += upd

    def step_online(j, carry, masked):
        out = []
        for u in range(2):
            kblk = k_refs[u][0, pl.ds(pl.multiple_of(j * TK, TK), TK), :]
            s = _dot(kblk, qts[u])
            if masked:
                kpos = lax.broadcasted_iota(jnp.int32, s.shape, 0)
                qpos = lax.broadcasted_iota(jnp.int32, s.shape, 1)
                s = jnp.where(kpos <= qpos, s, NEG)
            m, l = carry[u]
            m_new = jnp.maximum(m, jnp.max(s, axis=0, keepdims=True))
            alpha = jnp.exp(m - m_new)
            p = jnp.exp(s - m_new)
            l = alpha * l + jnp.sum(p, axis=0, keepdims=True)
            p = p.astype(BF16)
            upd = None
            for c in range(n_chunks):
                d = pv(u, j * n_chunks + c, p[c * VT_CHUNK:(c + 1) * VT_CHUNK, :])
                upd = d if upd is None else upd + d
            acc_ref[u] = alpha * acc_ref[u] + upd
            out.append((m_new, l))
        return tuple(out)

    if online:
        init = tuple((jnp.full((1, TQ), NEG, F32), jnp.zeros((1, TQ), F32)) for _ in range(2))
        carry = lax.fori_loop(0, i, lambda j, c: step_online(j, c, False), init)
        carry = step_online(i, carry, True)
        l0, l1 = carry[0][1], carry[1][1]
    else:
        l_ref[...] = jnp.zeros_like(l_ref)

        @pl.when(i == 0)
        def _():
            scores_to_probs(0, True, pd_ref)
            probs_times_values(0, pd_ref)

        @pl.when(i > 0)
        def _():
            scores_to_probs(0, False, p_ref)

            def pair(j):
                scores_to_probs(j, False, p2_ref)
                probs_times_values(j - 1, p_ref)
                scores_to_probs(j + 1, False, p_ref)
                probs_times_values(j, p2_ref)

            def two_pairs(jj, _):
                j = 4 * jj + 1
                scores_to_probs(j, False, p2_ref)
                probs_times_values(j - 1, p_ref)
                scores_to_probs(j + 1, False, p3_ref)
                probs_times_values(j, p2_ref)
                scores_to_probs(j + 2, False, p4_ref)
                probs_times_values(j + 1, p3_ref)
                scores_to_probs(j + 3, False, p_ref)
                probs_times_values(j + 2, p4_ref)
                return 0

            n_pairs = (i - 1) // 2
            lax.fori_loop(0, n_pairs // 2, two_pairs, 0)
            pl.when(n_pairs % 2 == 1)(lambda: pair(2 * n_pairs - 1))

            @pl.when(i % 2 == 0)
            def _():
                scores_to_probs(i - 1, False, p2_ref)
                probs_times_values(i - 2, p_ref)
                scores_to_probs(i, True, pd_ref)
                probs_times_values(i - 1, p2_ref)
                probs_times_values(i, pd_ref)

            @pl.when(i % 2 == 1)
            def _():
                scores_to_probs(i, True, pd_ref)
                probs_times_values(i - 1, p_ref)
                probs_times_values(i, pd_ref)

        l0, l1 = (jnp.sum(l_ref[u], axis=0, keepdims=True) for u in range(2))

    o0 = acc_ref[0] * (1.0 / l0)
    o1 = acc_ref[1] * (1.0 / l1)
    o = o0 - lam_ref[0] * o1 if shared_v else jnp.concatenate([o0, o1], axis=0)
    o_ref[0] = o.T.astype(o_ref.dtype)


def _attention(qt, kaug, vt, lam, shared_v, online):
    batch, _, seq = qt.shape
    n_steps = 4
    if shared_v:
        g0 = lambda h: h
        g1 = lambda h: H_A + h
    else:
        g0 = lambda h: 2 * h
        g1 = lambda h: 2 * h + 1
    dv = DV_A if shared_v else DH_B
    qspec = lambda g: pl.BlockSpec((1, LANES, TQ), lambda b, h, i: (b, g(h), i))
    kspec = lambda g: pl.BlockSpec((1, seq, LANES), lambda b, h, i: (b, 0, g(h)))
    return pl.pallas_call(
        functools.partial(_attn_kernel, shared_v, online),
        grid=(batch, n_steps, seq // TQ),
        in_specs=[pl.BlockSpec(memory_space=pltpu.SMEM), qspec(g0), qspec(g1), kspec(g0), kspec(g1),
                  pl.BlockSpec((1, seq // VT_CHUNK, LANES, VT_CHUNK), lambda b, h, i: (b, 0, h, 0))],
        out_specs=pl.BlockSpec((1, TQ, LANES), lambda b, h, i: (b, i, h)),
        out_shape=jax.ShapeDtypeStruct((batch, seq, 512), BF16),
        scratch_shapes=[pltpu.VMEM((2, dv, TQ), F32), pltpu.VMEM((2, 8, TQ), F32),
                        ] + [pltpu.VMEM((2, TK, TQ), BF16)] * 5,
        compiler_params=pltpu.CompilerParams(
            dimension_semantics=("arbitrary", "arbitrary", "arbitrary"), vmem_limit_bytes=VMEM_LIMIT),
        name=("attn_a" if shared_v else "attn_b") + ("_online" if online else ""),
    )(lam, qt, qt, kaug, kaug, vt)


def _decode_kernel(past, pps, pt_ref, lam_ref, qa_ref, qb_ref,
                   kta_hbm, va_hbm, ktb_hbm, vtb_hbm, lft_hbm,
                   kan_ref, van_ref, kbn_ref, vbn_ref, lfn_ref, upper_ref, oa_ref, ob_ref,
                   qbd_ref, m_ref, l_ref, acca_ref, accb_ref, fc_ref,
                   kta_buf, va_buf, ktb_buf, vtb_buf, lft_buf, sem):
    step_i = pl.program_id(1)
    n_steps = pl.num_programs(1)
    g = pl.program_id(0) * n_steps + step_i
    slot = g % 2
    pools = ((kta_hbm, kta_buf), (va_hbm, va_buf), (ktb_hbm, ktb_buf), (vtb_hbm, vtb_buf), (lft_hbm, lft_buf))

    def page_copies(gg, dst_slot):
        out = []
        for j in range(pps):
            pg = pt_ref[gg * pps + j]
            for k, (hbm, buf) in enumerate(pools):
                out.append(pltpu.make_async_copy(hbm.at[pg], buf.at[dst_slot, j], sem.at[dst_slot, k]))
        return out

    @pl.when(g == 0)
    def _():
        for c in page_copies(0, 0):
            c.start()

    @pl.when(g + 1 < pl.num_programs(0) * n_steps)
    def _():
        for c in page_copies(g + 1, 1 - slot):
            c.start()

    for c in page_copies(g, slot):
        c.wait()
    kta_refs, va_refs, ktb_refs, vtb_refs, lft_refs = (
        [buf.at[slot, j] for j in range(pps)] for _, buf in pools)
    rows = 32
    n_tok = 4
    pair_rows = kan_ref.shape[0]
    off = (pl.program_id(0) % 2) * n_tok
    page = LANES

    def row_ids(shape):
        r = lax.broadcasted_iota(jnp.int32, shape, 0)
        return ((r & 3, ((r >> 2) & 1) * H_A + (r >> 3), r >> 3),
                (r >> 3, r & 7, r & 7))

    def row_slope(head):
        return jnp.where(head == 0, SLOPES[0], jnp.where(head == 1, SLOPES[1],
               jnp.where(head == 2, SLOPES[2], SLOPES[3])))

    @pl.when(step_i == 0)
    def _():
        col_group = lax.broadcasted_iota(jnp.int32, (rows, 512), 1) >> 6
        for u, q_ref in enumerate((qa_ref, qb_ref)):
            tok, group, _ = row_ids((rows, 512))[u]
            qbd = jnp.zeros((rows, 512), F32)
            for t in range(n_tok):
                q_row = jnp.broadcast_to(q_ref[pl.ds(off + t, 1), :], (rows, 512))
                qbd = jnp.where((tok == t) & (group == col_group), q_row, qbd)
            qbd_ref[u] = qbd.astype(BF16)
        m_ref[...] = jnp.full(m_ref.shape, NEG, F32)
        l_ref[...] = jnp.zeros_like(l_ref)
        acca_ref[...] = jnp.zeros_like(acca_ref)
        accb_ref[...] = jnp.zeros_like(accb_ref)
        fc_ref[...] = jnp.zeros_like(fc_ref)

    def softmax_update(u, s):
        m_prev = m_ref[u][:, 0:1]
        l_prev = l_ref[u][:, 0:1]
        m_new = jnp.maximum(m_prev, jnp.max(s, axis=1, keepdims=True))
        alpha = jnp.exp(m_prev - m_new)
        pr = jnp.exp(s - m_new)
        l_new = alpha * l_prev + jnp.sum(pr, axis=1, keepdims=True)
        m_ref[u] = jnp.broadcast_to(m_new, (rows, LANES))
        l_ref[u] = jnp.broadcast_to(l_new, (rows, LANES))
        return alpha, pr

    is_last = step_i == n_steps - 1
    width = pps * page
    lane = lax.broadcasted_iota(jnp.int32, (rows, width), 1)
    tn = lax.broadcasted_iota(jnp.int32, (rows, page), 1)
    (tok_a, _, head_a), (tok_b, _, _) = row_ids((rows, page))
    pad_rows = lambda ref: jnp.concatenate(
        [ref[...], jnp.zeros((page - pair_rows, 512), F32)], axis=0).astype(BF16)

    def valid(tok):
        return is_last & (tn >= off) & (tn - off <= tok) & (tn < off + n_tok)

    dist = (past + row_ids((rows, width))[0][0] - (step_i * width + lane)).astype(F32)
    s_a = _dot(qbd_ref[0], jnp.concatenate([r[...].astype(BF16) for r in kta_refs], axis=1))
    s_a = s_a - row_slope(row_ids((rows, width))[0][2]) * dist
    s_an = _dot_nt(qbd_ref[0], pad_rows(kan_ref)) - row_slope(head_a) * (tok_a - (tn - off)).astype(F32)
    alpha, pr = softmax_update(0, jnp.concatenate([s_a, jnp.where(valid(tok_a), s_an, NEG)], axis=1))
    pr = pr.astype(BF16)
    v_all = jnp.concatenate(
        [jnp.concatenate([r[pl.ds(h, page, stride=H_A), :].astype(BF16) for h in range(H_A)], axis=1)
         for r in va_refs], axis=0)
    acca_ref[...] = (alpha * acca_ref[...] + _dot(pr[:, 0:width], v_all)
                     + _dot(pr[:, width:], pad_rows(van_ref)))

    parts = []
    for j in range(pps):
        parts.extend(v.astype(F32) for v in _split3(lft_refs[j][...]))
        parts.append(jnp.zeros((8, page), F32))
    cs = _dot(jnp.concatenate(parts, axis=0).astype(BF16), upper_ref[...])
    f_carry = fc_ref[...][:, 0:1]
    f_pages = []
    for j in range(pps):
        local = cs[32 * j:32 * j + 8] + cs[32 * j + 8:32 * j + 16] + cs[32 * j + 16:32 * j + 24]
        f_page = local + f_carry
        f_carry = f_carry + local[:, page - 1:page]
        f_pages.append(jnp.concatenate([f_page] * n_tok, axis=0))
    fc_ref[...] = jnp.broadcast_to(f_carry, (8, LANES))
    lfn = lfn_ref[0]
    t8 = lax.broadcasted_iota(jnp.int32, (8, page), 1)
    f_new = jnp.broadcast_to(f_carry, (8, page))
    for t in range(n_tok):
        lf_t = jnp.sum(jnp.where(t8 == off + t, lfn, 0.0), axis=1, keepdims=True)
        f_new = f_new + jnp.where(t8 >= off + t, lf_t, 0.0)
    s_b = _dot(qbd_ref[1], jnp.concatenate([r[...].astype(BF16) for r in ktb_refs], axis=1))
    s_b = s_b - jnp.concatenate(f_pages, axis=1)
    s_bn = _dot_nt(qbd_ref[1], pad_rows(kbn_ref)) - jnp.concatenate([f_new] * n_tok, axis=0)
    alpha, pr = softmax_update(1, jnp.concatenate([s_b, jnp.where(valid(tok_b), s_bn, NEG)], axis=1))
    pr = pr.astype(BF16)
    vt_all = jnp.concatenate([r[...].astype(BF16) for r in vtb_refs], axis=1)
    accb_ref[...] = (alpha * accb_ref[...] + _dot_nt(pr[:, 0:width], vt_all)
                     + _dot(pr[:, width:], pad_rows(vbn_ref)))

    @pl.when(is_last)
    def _():
        on_a = acca_ref[...] * (1.0 / l_ref[0][:, 0:1])
        head_b = row_ids((rows, 512))[1][2]
        col_head = lax.broadcasted_iota(jnp.int32, (rows, 512), 1) >> 6
        on_b = jnp.where(head_b == col_head, accb_ref[...] * (1.0 / l_ref[1][:, 0:1]), 0.0)

        def write(first):
            for h in range(H_A):
                blk = on_a[8 * h:8 * h + 8, h * DV_A:(h + 1) * DV_A]
                oa_ref[first:first + n_tok, h * DV_A:(h + 1) * DV_A] = (
                    blk[0:n_tok] - lam_ref[0] * blk[n_tok:2 * n_tok])
            for t in range(n_tok):
                ob_ref[first + t:first + t + 1, :] = jnp.sum(on_b[8 * t:8 * t + 8], axis=0, keepdims=True)

        pl.when(off == 0)(lambda: write(0))
        pl.when(off != 0)(lambda: write(n_tok))


def _decode(page_table, lam, qa, qb, pools, news, lfn_t, upper):
    n_seq, n_pages = page_table.shape
    page = pools[0].shape[2]
    pps = PAGES_PER_STEP
    n_rows = qa.shape[0]
    past = n_pages * page
    pair_spec = pl.BlockSpec((2 * (n_rows // n_seq), 512), lambda n, s, pt: (n // 2, 0))
    page_bufs = [pltpu.VMEM((2, pps) + pool.shape[1:], F32) for pool in pools]
    grid_spec = pltpu.PrefetchScalarGridSpec(
        num_scalar_prefetch=1,
        grid=(n_seq, n_pages // pps),
        in_specs=[pl.BlockSpec(memory_space=pltpu.SMEM), pair_spec, pair_spec]
                 + [pl.BlockSpec(memory_space=pl.ANY)] * len(pools)
                 + [pair_spec] * 4
                 + [pl.BlockSpec((1, H_B, page), lambda n, s, pt: (n // 2, 0, 0)),
                    pl.BlockSpec((page, page), lambda n, s, pt: (0, 0))],
        out_specs=[pair_spec, pair_spec],
        scratch_shapes=[pltpu.VMEM((2, 32, 512), BF16), pltpu.VMEM((2, 32, LANES), F32),
                        pltpu.VMEM((2, 32, LANES), F32), pltpu.VMEM((32, 512), F32),
                        pltpu.VMEM((32, 512), F32), pltpu.VMEM((H_B, LANES), F32)]
                       + page_bufs + [pltpu.SemaphoreType.DMA((2, len(pools)))],
    )
    return pl.pallas_call(
        functools.partial(_decode_kernel, past, pps),
        grid_spec=grid_spec,
        out_shape=[jax.ShapeDtypeStruct((n_rows, 512), F32)] * 2,
        compiler_params=pltpu.CompilerParams(dimension_semantics=("arbitrary", "arbitrary"),
                                             vmem_limit_bytes=VMEM_LIMIT),
        name="decode",
    )(page_table.reshape(-1), lam, qa, qb, *pools, *news, lfn_t, upper)


def _merge_kernel(x_ref, oa_ref, ob_ref, sga_ref, sgb_ref, sma_ref, smb_ref, gres_ref, gsub_ref,
                  woa_ref, wob_ref, wo_ref, y_ref):
    oa = oa_ref[...].astype(F32)
    heads = []
    for h in range(H_A):
        oh = oa[:, h * DV_A:(h + 1) * DV_A]
        heads.append(oh * lax.rsqrt(jnp.mean(oh * oh, axis=-1, keepdims=True) + EPS))
    oa = jnp.concatenate(heads, axis=1) * gsub_ref[...]
    ya = _dot((oa * sga_ref[...].astype(F32)).astype(BF16), woa_ref[...])
    yb = _dot((ob_ref[...].astype(F32) * sgb_ref[...].astype(F32)).astype(BF16), wob_ref[...])
    mix = sma_ref[...].astype(F32) * ya + smb_ref[...].astype(F32) * yb
    y = _dot(mix.astype(BF16), wo_ref[...])
    y_ref[...] = x_ref[...] + gres_ref[...] * y


def _merge(x2, oa, ob, sga, sgb, sma, smb, gres, gres_spec, wts):
    rows = x2.shape[0]
    row = lambda w: pl.BlockSpec((TM_MERGE, w), lambda i: (i, 0))
    consts = [wts["gsub"], wts["woa"], wts["wob"], wts["wo"]]
    return pl.pallas_call(
        _merge_kernel,
        grid=(rows // TM_MERGE,),
        in_specs=[row(D_MODEL), row(512), row(512), row(512), row(512), row(1024), row(1024), gres_spec]
                 + [_const_spec(c.shape) for c in consts],
        out_specs=row(D_MODEL),
        out_shape=jax.ShapeDtypeStruct((rows, D_MODEL), F32),
        compiler_params=pltpu.CompilerParams(dimension_semantics=("arbitrary",),
                                             vmem_limit_bytes=VMEM_LIMIT),
        name="merge",
    )(x2, oa, ob, sga, sgb, sma, smb, gres, *consts)


def _ones_constants():
    gsum = np.kron(np.eye(N_GROUPS), np.ones((DH_A, DH_A)))
    tri = np.tril(np.ones((TM, TM)))
    upper = np.triu(np.ones((LANES, LANES)))
    bf = lambda a: jnp.asarray(a, BF16)
    return dict(gsum=bf(gsum), tri=bf(tri), upper=bf(upper))


def _bf16_ceil(x):
    y = x.astype(BF16).astype(F32)
    return jnp.where(y < x, y * (1.0 + 2.0 ** -7), y)


def kernel(x_prompt, x_sample, cache_a_k, cache_a_v, cache_b_k, cache_b_v, cache_b_logf, page_table,
           c_prompt, c_sample, w_ada, b_ada, w_in, b_f, g_q_a, g_k_a, g_q_b, g_k_b,
           lambda_q1, lambda_k1, lambda_q2, lambda_k2, g_sub_a, w_out_a, w_out_b, w_o):
    assert w_ada.shape[0] == 1, "single-layer step"
    batch, seq, _ = x_prompt.shape
    n_seq, n_tok, _ = x_sample.shape
    n_pool, page = cache_a_k.shape[1], cache_a_k.shape[2]
    layer = 0
    lam_init = 0.8 - 0.6 * math.exp(-0.3 * layer)
    lam = (jnp.exp(jnp.sum(lambda_q1[layer] * lambda_k1[layer]))
           - jnp.exp(jnp.sum(lambda_q2[layer] * lambda_k2[layer])) + lam_init).reshape(1).astype(F32)

    w = w_in[layer]
    sec = np.cumsum((0, 512, 512, 512, 512, 512, 512, 512, H_B, 512, 1024, 1024))
    cols = lambda k: w[:, sec[k]:sec[k + 1]]
    wq = jnp.concatenate([cols(0), cols(4)], axis=1).astype(BF16)
    wn = jnp.concatenate([cols(1), cols(5), cols(2), cols(6), cols(3), cols(8), cols(9), cols(10)],
                         axis=1).astype(BF16)
    wf = jnp.pad(cols(7), ((0, 0), (0, LANES - H_B))).astype(BF16)
    gq = jnp.concatenate([jnp.tile(g_q_a[layer], N_GROUPS) * DH_A ** -0.5,
                          jnp.tile(g_q_b[layer], N_GROUPS) * DH_B ** -0.5])
    gk = jnp.concatenate([jnp.tile(g_k_a[layer], N_GROUPS), jnp.tile(g_k_b[layer], N_GROUPS)])
    wts = dict(_ones_constants(),
               wq=wq, wqkt=jnp.concatenate([wq, wn[:, 0:1024], wn[:, 1536:2048]], axis=1).T,
               wn=wn, wn_prompt=jnp.concatenate([wn[:, 1024:1536], wn[:, 2048:]], axis=1),

               wf=wf, bf=jnp.pad(b_f[layer], (0, LANES - H_B)).reshape(1, LANES),
               gqk_col=jnp.concatenate([gq, gk]).reshape(-1, 1), gq_row=gq.reshape(1, -1), gk_row=gk.reshape(1, -1),
               gsub=(jnp.tile(g_sub_a[layer], H_A) * (1.0 - lam_init)).reshape(1, W_A),
               woa=w_out_a[layer].astype(BF16), wob=w_out_b[layer].astype(BF16), wo=w_o[layer].astype(BF16))
    bound = lambda gq_, gk_: _bf16_ceil(8.1 * jnp.max(jnp.abs(gq_ * gk_)))
    mshift = jnp.stack([bound(g_q_a[layer], g_k_a[layer]), bound(g_q_b[layer], g_k_b[layer])]).astype(F32)

    n_c = batch + n_seq
    c_all = jnp.pad(jnp.concatenate([c_prompt, c_sample], axis=0), ((0, -n_c % 8), (0, 0)))
    mod = _ada(c_all, w_ada[layer], b_ada[layer])
    shift, scale, gres = (mod[:, k * D_MODEL:(k + 1) * D_MODEL] for k in range(3))

    xp2 = x_prompt.reshape(batch * seq, D_MODEL)
    p3 = lambda a: a[:batch].reshape(batch, 1, D_MODEL)
    (kta, va, ktb, vtbo, lft, qta, kaa, vta, qtb, kab, vtb, sga, sgb, sma, smb) = _proj_prompt(
        xp2, p3(scale), p3(shift), mshift, wts, batch, seq)
    online = jnp.max(mshift) > FAST_PATH_MAX_BOUND
    attend = lambda *a: lax.cond(online, lambda: _attention(*a, True), lambda: _attention(*a, False))
    oa = attend(qta, kaa, vta, lam, True).reshape(batch * seq, W_A)
    ob = attend(qtb, kab, vtb, lam, False).reshape(batch * seq, W_B)
    nb = seq // TM_MERGE
    gres_p_spec = pl.BlockSpec((None, 1, D_MODEL), lambda i: (i // nb, 0, 0))
    yp = _merge(xp2, oa, ob, sga, sgb, sma, smb, p3(gres), gres_p_spec, wts).reshape(batch, seq, D_MODEL)

    xs2 = x_sample.reshape(n_seq * n_tok, D_MODEL)
    rep = lambda a: jnp.repeat(a[batch:n_c], n_tok, axis=0)
    (qa_s, qb_s, ka_s, va_s, kb_s, vb_s, lf_s, sga_s, sgb_s, sma_s, smb_s) = _proj_sample(
        xs2, rep(scale), rep(shift), wts)
    pools = [jnp.transpose(cache_a_k[layer], (0, 2, 3, 4, 1)).reshape(n_pool, 512, page),
             cache_a_v[layer].reshape(n_pool, page * H_A, DV_A),
             jnp.transpose(cache_b_k[layer], (0, 2, 3, 1)).reshape(n_pool, 512, page),
             jnp.transpose(cache_b_v[layer], (0, 2, 3, 1)).reshape(n_pool, 512, page),
             jnp.swapaxes(cache_b_logf[layer], 1, 2)]
    lfn_t = jnp.pad(jnp.swapaxes(lf_s.reshape(n_seq // 2, 2 * n_tok, H_B), 1, 2),
                    ((0, 0), (0, 0), (0, page - 2 * n_tok)))
    oa_s, ob_s = _decode(page_table, lam, qa_s, qb_s, pools, [ka_s, va_s, kb_s, vb_s], lfn_t, wts["upper"])
    gres_s_spec = pl.BlockSpec((TM_MERGE, D_MODEL), lambda i: (i, 0))
    ys = _merge(xs2, oa_s, ob_s, sga_s, sgb_s, sma_s, smb_s,
                rep(gres), gres_s_spec, wts).reshape(n_seq, n_tok, D_MODEL)

    return (yp, ys,
            jnp.moveaxis(kta.reshape(1, batch, 2, H_A, DH_A, seq), -1, 2), va.reshape(1, batch, seq, H_A, DV_A),
            jnp.moveaxis(ktb.reshape(1, batch, H_B, DH_B, seq), -1, 2),
            jnp.moveaxis(vtbo.reshape(1, batch, H_B, DH_B, seq), -1, 2),
            jnp.moveaxis(lft.reshape(1, batch, H_B, seq), -1, 2),
            ka_s.reshape(1, n_seq, n_tok, 2, H_A, DH_A), va_s.reshape(1, n_seq, n_tok, H_A, DV_A),
            kb_s.reshape(1, n_seq, n_tok, H_B, DH_B), vb_s.reshape(1, n_seq, n_tok, H_B, DH_B),
            lf_s.reshape(1, n_seq, n_tok, H_B))
```

```python
import functools
import math

import numpy as np
import jax
import jax.numpy as jnp
from jax import lax
from jax.experimental import pallas as pl
from jax.experimental.pallas import tpu as pltpu

F32 = jnp.float32
BF16 = jnp.bfloat16

D_MODEL = 1024
H_A, DH_A, DV_A = 4, 64, 128
H_B, DH_B = 8, 64
W_A = H_A * DV_A
W_B = H_B * DH_B
N_GROUPS = 8
EPS = 1e-6
NEG = -1e30
SLOPES = tuple(2.0 ** (-8.0 * (h + 1) / H_A) for h in range(H_A))
LANES = 128

TM = 256
TM_MERGE = 512
TQ = 512
TK = 512
VT_CHUNK = TM
PAGES_PER_STEP = 16
FAST_PATH_MAX_BOUND = 30.0
VMEM_LIMIT = 56 * 1024 * 1024


def _dot(a, b):
    return jnp.dot(a, b, preferred_element_type=F32)


def _dot_nt(a, b):
    return lax.dot_general(a, b, (((1,), (1,)), ((), ())), preferred_element_type=F32)


def _split3(x):
    hi = x.astype(BF16)
    r1 = x - hi.astype(F32)
    mid = r1.astype(BF16)
    lo = (r1 - mid.astype(F32)).astype(BF16)
    return hi, mid, lo


def _silu(x):
    return x * jax.nn.sigmoid(x)


def _const_spec(shape):
    nd = len(shape)
    return pl.BlockSpec(shape, lambda *_: (0,) * nd, pipeline_mode=pl.Buffered(1))


def _ada_kernel(c_ref, w_ref, b_ref, o_ref):
    c = c_ref[...]
    o_ref[...] = _dot(_silu(c).astype(BF16), w_ref[...].astype(BF16)) + b_ref[...]


def _ada(c_all, w_ada, b_ada):
    rows = c_all.shape[0]
    n = w_ada.shape[1]
    bn = 1024
    return pl.pallas_call(
        _ada_kernel,
        grid=(n // bn,),
        in_specs=[pl.BlockSpec((rows, D_MODEL), lambda j: (0, 0)),
                  pl.BlockSpec((D_MODEL, bn), lambda j: (0, j)),
                  pl.BlockSpec((1, bn), lambda j: (0, j))],
        out_specs=pl.BlockSpec((rows, bn), lambda j: (0, j)),
        out_shape=jax.ShapeDtypeStruct((rows, n), F32),
        name="ada",
    )(c_all, w_ada, b_ada.reshape(1, n))


def _hidden(x_ref, scale_ref, shift_ref):
    x = x_ref[...]
    ms = jnp.mean(x * x, axis=-1, keepdims=True)
    h = x * lax.rsqrt(ms + EPS) * (1.0 + scale_ref[...]) + shift_ref[...]
    return h.astype(BF16)


def _group_rms(z, g_ref):
    zz = z * z
    hi = zz.astype(BF16)
    lo = (zz - hi.astype(F32)).astype(BF16)
    ss = _dot(hi, g_ref[...]) + _dot(lo, g_ref[...])
    return z * lax.rsqrt(ss * (1.0 / DH_A) + EPS)


def _log_sigmoid(z):
    return jnp.minimum(z, 0.0) - jnp.log1p(jnp.exp(-jnp.abs(z)))


def _logf(hb, wf_ref, bf_ref):
    z = _dot(hb, wf_ref[...]) + bf_ref[...]
    lane = lax.broadcasted_iota(jnp.int32, z.shape, 1)
    return jnp.where(lane < H_B, _log_sigmoid(z), 0.0)


def _proj_prompt_kernel(nb, mshift_ref, x_ref, scale_ref, shift_ref, wqkt_ref, wn_ref, wf_ref, bf_ref,
                        gqk_ref, tri_ref,
                        kta_ref, va_ref, ktb_ref, vtbo_ref, lft_ref,
                        qta_ref, kaa_ref, vta_ref, qtb_ref, kab_ref, vtb_ref,
                        sga_ref, sgb_ref, sma_ref, smb_ref, carry_ref):
    tm = x_ref.shape[0]
    ib = pl.program_id(0) % nb
    hb = _hidden(x_ref, scale_ref, shift_ref)

    lf = _logf(hb, wf_ref, bf_ref)
    lft_ref[0] = lf.T[0:H_B, :]
    hi, mid, lo = _split3(lf)
    tri = tri_ref[...]
    f_local = _dot(tri, hi) + _dot(tri, mid) + _dot(tri, lo)

    @pl.when(ib == 0)
    def _():
        carry_ref[...] = jnp.zeros_like(carry_ref)

    f_nat = f_local + carry_ref[...]
    carry_ref[...] = f_nat[tm - 1:tm, :]
    f_t = f_nat.T[0:H_B, :]

    n_g = 4 * N_GROUPS
    n_qk = n_g * DH_A
    qkt = jnp.concatenate([_dot_nt(wqkt_ref[0:n_qk // 2, :], hb),
                           _dot_nt(wqkt_ref[n_qk // 2:n_qk, :], hb)], axis=0)
    g3 = qkt.reshape(n_g, DH_A, tm)
    ssq = jnp.sum(g3 * g3, axis=1, keepdims=True)
    g3 = (qkt * gqk_ref[...]).reshape(n_g, DH_A, tm) * lax.rsqrt(ssq * (1.0 / DH_A) + EPS)
    kta_ref[0] = g3[2 * N_GROUPS:3 * N_GROUPS].reshape(N_GROUPS * DH_A, tm)
    ktb_ref[0] = g3[3 * N_GROUPS:4 * N_GROUPS].reshape(N_GROUPS * DH_A, tm)

    r8 = lax.broadcasted_iota(jnp.int32, (8, tm), 0)
    pos = ib * tm + lax.broadcasted_iota(jnp.int32, (8, tm), 1)
    pos_hi = (pos >> 7).astype(F32)
    pos_lo = (pos & 127).astype(F32)
    zeros_tail = jnp.zeros((LANES - DH_A - 8, tm), F32)
    m_a = mshift_ref[0]
    m_b = mshift_ref[1]

    def operand(group, aug):
        return jnp.concatenate([g3[group], aug, zeros_tail], axis=0)

    k_aug_a = jnp.where(r8 == 2, pos_hi, jnp.where(r8 == 3, pos_lo,
              jnp.where((r8 <= 1) | (r8 == 4), 1.0, 0.0)))
    for g in range(N_GROUPS):
        slope = SLOPES[g % H_A]
        q_aug = jnp.where(r8 == 0, -slope * 128.0 * pos_hi,
                jnp.where(r8 == 1, -slope * pos_lo,
                jnp.where(r8 == 2, slope * 128.0,
                jnp.where(r8 == 3, slope,
                jnp.where(r8 == 4, -m_a, 0.0)))))
        qta_ref[0, g * LANES:(g + 1) * LANES, :] = operand(g, q_aug).astype(BF16)
        kaa_ref[0, :, g * LANES:(g + 1) * LANES] = operand(2 * N_GROUPS + g, k_aug_a).T.astype(BF16)
    ft_hi, ft_mid, ft_lo = _split3(f_t)
    for g in range(N_GROUPS):
        bc = lambda v: jnp.broadcast_to(v[g:g + 1, :].astype(F32), (8, tm))
        q_aug = jnp.where(r8 == 0, bc(ft_hi),
                jnp.where(r8 == 1, bc(ft_mid),
                jnp.where(r8 == 2, bc(ft_lo),
                jnp.where(r8 <= 5, -1.0,
                jnp.where(r8 == 6, -m_b, 0.0)))))
        k_aug = jnp.where(r8 <= 2, 1.0,
                jnp.where(r8 == 3, bc(ft_hi),
                jnp.where(r8 == 4, bc(ft_mid),
                jnp.where(r8 == 5, bc(ft_lo),
                jnp.where(r8 == 6, 1.0, 0.0)))))
        qtb_ref[0, g * LANES:(g + 1) * LANES, :] = operand(N_GROUPS + g, q_aug).astype(BF16)
        kab_ref[0, :, g * LANES:(g + 1) * LANES] = operand(3 * N_GROUPS + g, k_aug).T.astype(BF16)

    va = _dot(hb, wn_ref[:, 0:512])
    for h in range(H_A):
        va_ref[pl.ds(h, tm, stride=H_A), :] = va[:, h * DV_A:(h + 1) * DV_A]
    vta_ref[0, 0] = va.T.astype(BF16)
    vbt = _dot_nt(wqkt_ref[n_qk:, :], hb)
    vtbo_ref[0] = vbt
    vtb_ref[0, 0] = vbt.astype(BF16)

    sga_ref[...] = _silu(_dot(hb, wn_ref[:, 512:1024])).astype(BF16)
    sgb_ref[...] = _silu(_dot(hb, wn_ref[:, 1024:1536])).astype(BF16)
    sma_ref[...] = jax.nn.sigmoid(_dot(hb, wn_ref[:, 1536:2560])).astype(BF16)
    smb_ref[...] = jax.nn.sigmoid(_dot(hb, wn_ref[:, 2560:3584])).astype(BF16)


def _proj_prompt(x2, scale, shift, mshift, wts, batch, seq):
    rows = batch * seq
    nb = seq // TM
    nk = seq // VT_CHUNK
    row = lambda w: pl.BlockSpec((TM, w), lambda i: (i, 0))
    mod = pl.BlockSpec((None, 1, D_MODEL), lambda i: (i // nb, 0, 0))
    consts = [wts["wqkt"], wts["wn_prompt"], wts["wf"], wts["bf"], wts["gqk_col"], wts["tri"]]
    seq_t = lambda r: jax.ShapeDtypeStruct((batch, r, seq), F32)
    out_shape = [seq_t(512), jax.ShapeDtypeStruct((rows * H_A, DV_A), F32), seq_t(512), seq_t(512),
                 seq_t(H_B)] + [
        jax.ShapeDtypeStruct((batch, 1024, seq), BF16), jax.ShapeDtypeStruct((batch, seq, 1024), BF16),
        jax.ShapeDtypeStruct((batch, nk, W_A, VT_CHUNK), BF16),
        jax.ShapeDtypeStruct((batch, 1024, seq), BF16), jax.ShapeDtypeStruct((batch, seq, 1024), BF16),
        jax.ShapeDtypeStruct((batch, nk, W_B, VT_CHUNK), BF16),
        jax.ShapeDtypeStruct((rows, 512), BF16), jax.ShapeDtypeStruct((rows, 512), BF16),
        jax.ShapeDtypeStruct((rows, 1024), BF16), jax.ShapeDtypeStruct((rows, 1024), BF16)]
    qt_spec = pl.BlockSpec((1, 1024, TM), lambda i: (i // nb, 0, i % nb))
    k_spec = pl.BlockSpec((1, TM, 1024), lambda i: (i // nb, i % nb, 0))
    vt_spec = pl.BlockSpec((1, TM // VT_CHUNK, 512, VT_CHUNK), lambda i: (i // nb, i % nb, 0, 0))
    out_t = lambda r: pl.BlockSpec((1, r, TM), lambda i: (i // nb, 0, i % nb))
    out_specs = [out_t(512), pl.BlockSpec((TM * H_A, DV_A), lambda i: (i, 0)), out_t(512), out_t(512),
                 out_t(H_B)] + [
        qt_spec, k_spec, vt_spec, qt_spec, k_spec, vt_spec, row(512), row(512), row(1024), row(1024)]
    return pl.pallas_call(
        functools.partial(_proj_prompt_kernel, nb),
        grid=(rows // TM,),
        in_specs=[pl.BlockSpec(memory_space=pltpu.SMEM), row(D_MODEL), mod, mod]
                 + [_const_spec(c.shape) for c in consts],
        out_specs=out_specs,
        out_shape=out_shape,
        scratch_shapes=[pltpu.VMEM((1, LANES), F32)],
        compiler_params=pltpu.CompilerParams(dimension_semantics=("arbitrary",),
                                             vmem_limit_bytes=VMEM_LIMIT),
        name="proj_prompt",
    )(mshift, x2, scale, shift, *consts)


def _proj_sample_kernel(x_ref, scale_ref, shift_ref, wq_ref, wn_ref, wf_ref, bf_ref, gq_ref, gk_ref, g_ref,
                        qa_ref, qb_ref, ka_ref, va_ref, kb_ref, vb_ref, lf_ref,
                        sga_ref, sgb_ref, sma_ref, smb_ref):
    hb = _hidden(x_ref, scale_ref, shift_ref)
    lf_ref[...] = _logf(hb, wf_ref, bf_ref)[:, :H_B]
    qa_ref[...] = _group_rms(_dot(hb, wq_ref[:, 0:512]), g_ref) * gq_ref[:, 0:512]
    qb_ref[...] = _group_rms(_dot(hb, wq_ref[:, 512:1024]), g_ref) * gq_ref[:, 512:1024]
    ka_ref[...] = _group_rms(_dot(hb, wn_ref[:, 0:512]), g_ref) * gk_ref[:, 0:512]
    kb_ref[...] = _group_rms(_dot(hb, wn_ref[:, 512:1024]), g_ref) * gk_ref[:, 512:1024]
    va_ref[...] = _dot(hb, wn_ref[:, 1024:1536])
    vb_ref[...] = _dot(hb, wn_ref[:, 1536:2048])
    sga_ref[...] = _silu(_dot(hb, wn_ref[:, 2048:2560])).astype(BF16)
    sgb_ref[...] = _silu(_dot(hb, wn_ref[:, 2560:3072])).astype(BF16)
    sma_ref[...] = jax.nn.sigmoid(_dot(hb, wn_ref[:, 3072:4096])).astype(BF16)
    smb_ref[...] = jax.nn.sigmoid(_dot(hb, wn_ref[:, 4096:5120])).astype(BF16)


def _proj_sample(x2, scale_rows, shift_rows, wts):
    rows = x2.shape[0]
    row = lambda w: pl.BlockSpec((TM, w), lambda i: (i, 0))
    consts = [wts["wq"], wts["wn"], wts["wf"], wts["bf"], wts["gq_row"], wts["gk_row"], wts["gsum"]]
    out_shape = [jax.ShapeDtypeStruct((rows, 512), F32)] * 6 + [jax.ShapeDtypeStruct((rows, H_B), F32)] + [
        jax.ShapeDtypeStruct((rows, 512), BF16), jax.ShapeDtypeStruct((rows, 512), BF16),
        jax.ShapeDtypeStruct((rows, 1024), BF16), jax.ShapeDtypeStruct((rows, 1024), BF16)]
    out_specs = [row(512)] * 6 + [row(H_B), row(512), row(512), row(1024), row(1024)]
    return pl.pallas_call(
        _proj_sample_kernel,
        grid=(rows // TM,),
        in_specs=[row(D_MODEL), row(D_MODEL), row(D_MODEL)] + [_const_spec(c.shape) for c in consts],
        out_specs=out_specs,
        out_shape=out_shape,
        compiler_params=pltpu.CompilerParams(dimension_semantics=("arbitrary",),
                                             vmem_limit_bytes=VMEM_LIMIT),
        name="proj_sample",
    )(x2, scale_rows, shift_rows, *consts)


def _attn_kernel(shared_v, online, lam_ref, qt0_ref, qt1_ref, k0_ref, k1_ref, vt_ref, o_ref,
                 acc_ref, l_ref, p_ref, p2_ref, p3_ref, p4_ref, pd_ref):
    i = pl.program_id(2)
    qts = (qt0_ref[0], qt1_ref[0])
    k_refs = (k0_ref, k1_ref)
    dv = acc_ref.shape[1]
    n_chunks = TK // VT_CHUNK
    acc_ref[...] = jnp.zeros_like(acc_ref)

    def pv(u, chunk, p):
        blk = vt_ref[0, chunk]
        v = blk if shared_v else blk[u * dv:(u + 1) * dv, :]
        return _dot(v, p)

    def scores_to_probs(j, masked, dst_ref):
        for u in range(2):
            kblk = k_refs[u][0, pl.ds(pl.multiple_of(j * TK, TK), TK), :]
            s = _dot(kblk, qts[u])
            if masked:
                kpos = lax.broadcasted_iota(jnp.int32, s.shape, 0)
                qpos = lax.broadcasted_iota(jnp.int32, s.shape, 1)
                s = jnp.where(kpos <= qpos, s, NEG)
            p = jnp.exp(s)
            l_ref[u] += jnp.sum(p.reshape(TK // 8, 8, TQ), axis=0)
            dst_ref[u] = p.astype(BF16)

    def probs_times_values(j, src_ref):
        for u in range(2):
            upd = None
            for c in range(n_chunks):
                d = pv(u, j * n_chunks + c, src_ref[u, c * VT_CHUNK:(c + 1) * VT_CHUNK, :])
                upd = d if upd is None else upd + d
            acc_ref[u] += upd

    def step_online(j, carry, masked):
        out = []
        for u in range(2):
            kblk = k_refs[u][0, pl.ds(pl.multiple_of(j * TK, TK), TK), :]
            s = _dot(kblk, qts[u])
            if masked:
                kpos = lax.broadcasted_iota(jnp.int32, s.shape, 0)
                qpos = lax.broadcasted_iota(jnp.int32, s.shape, 1)
                s = jnp.where(kpos <= qpos, s, NEG)
            m, l = carry[u]
            m_new = jnp.maximum(m, jnp.max(s, axis=0, keepdims=True))
            alpha = jnp.exp(m - m_new)
            p = jnp.exp(s - m_new)
            l = alpha * l + jnp.sum(p, axis=0, keepdims=True)
            p = p.astype(BF16)
            upd = None
            for c in range(n_chunks):
                d = pv(u, j * n_chunks + c, p[c * VT_CHUNK:(c + 1) * VT_CHUNK, :])
                upd = d if upd is None else upd + d
            acc_ref[u] = alpha * acc_ref[u] + upd
            out.append((m_new, l))
        return tuple(out)

    if online:
        init = tuple((jnp.full((1, TQ), NEG, F32), jnp.zeros((1, TQ), F32)) for _ in range(2))
        carry = lax.fori_loop(0, i, lambda j, c: step_online(j, c, False), init)
        carry = step_online(i, carry, True)
        l0, l1 = carry[0][1], carry[1][1]
    else:
        l_ref[...] = jnp.zeros_like(l_ref)

        @pl.when(i == 0)
        def _():
            scores_to_probs(0, True, pd_ref)
            probs_times_values(0, pd_ref)

        @pl.when(i > 0)
        def _():
            scores_to_probs(0, False, p_ref)

            def pair(j):
                scores_to_probs(j, False, p2_ref)
                probs_times_values(j - 1, p_ref)
                scores_to_probs(j + 1, False, p_ref)
                probs_times_values(j, p2_ref)

            def two_pairs(jj, _):
                j = 4 * jj + 1
                scores_to_probs(j, False, p2_ref)
                probs_times_values(j - 1, p_ref)
                scores_to_probs(j + 1, False, p3_ref)
                probs_times_values(j, p2_ref)
                scores_to_probs(j + 2, False, p4_ref)
                probs_times_values(j + 1, p3_ref)
                scores_to_probs(j + 3, False, p_ref)
                probs_times_values(j + 2, p4_ref)
                return 0

            n_pairs = (i - 1) // 2
            lax.fori_loop(0, n_pairs // 2, two_pairs, 0)
            pl.when(n_pairs % 2 == 1)(lambda: pair(2 * n_pairs - 1))

            @pl.when(i % 2 == 0)
            def _():
                scores_to_probs(i - 1, False, p2_ref)
                probs_times_values(i - 2, p_ref)
                scores_to_probs(i, True, pd_ref)
                probs_times_values(i - 1, p2_ref)
                probs_times_values(i, pd_ref)

            @pl.when(i % 2 == 1)
            def _():
                scores_to_probs(i, True, pd_ref)
                probs_times_values(i - 1, p_ref)
                probs_times_values(i, pd_ref)

        l0, l1 = (jnp.sum(l_ref[u], axis=0, keepdims=True) for u in range(2))

    o0 = acc_ref[0] * (1.0 / l0)
    o1 = acc_ref[1] * (1.0 / l1)
    o = o0 - lam_ref[0] * o1 if shared_v else jnp.concatenate([o0, o1], axis=0)
    o_ref[0] = o.T


def _attention(qt, kaug, vt, lam, shared_v, online):
    batch, _, seq = qt.shape
    n_steps = 4
    if shared_v:
        g0 = lambda h: h
        g1 = lambda h: H_A + h
    else:
        g0 = lambda h: 2 * h
        g1 = lambda h: 2 * h + 1
    dv = DV_A if shared_v else DH_B
    qspec = lambda g: pl.BlockSpec((1, LANES, TQ), lambda b, h, i: (b, g(h), i))
    kspec = lambda g: pl.BlockSpec((1, seq, LANES), lambda b, h, i: (b, 0, g(h)))
    return pl.pallas_call(
        functools.partial(_attn_kernel, shared_v, online),
        grid=(batch, n_steps, seq // TQ),
        in_specs=[pl.BlockSpec(memory_space=pltpu.SMEM), qspec(g0), qspec(g1), kspec(g0), kspec(g1),
                  pl.BlockSpec((1, seq // VT_CHUNK, LANES, VT_CHUNK), lambda b, h, i: (b, 0, h, 0))],
        out_specs=pl.BlockSpec((1, TQ, LANES), lambda b, h, i: (b, i, h)),
        out_shape=jax.ShapeDtypeStruct((batch, seq, 512), F32),
        scratch_shapes=[pltpu.VMEM((2, dv, TQ), F32), pltpu.VMEM((2, 8, TQ), F32),
                        ] + [pltpu.VMEM((2, TK, TQ), BF16)] * 5,
        compiler_params=pltpu.CompilerParams(
            dimension_semantics=("arbitrary", "arbitrary", "arbitrary"), vmem_limit_bytes=VMEM_LIMIT),
        name=("attn_a" if shared_v else "attn_b") + ("_online" if online else ""),
    )(lam, qt, qt, kaug, kaug, vt)


def _decode_kernel(past, pps, pt_ref, lam_ref, qa_ref, qb_ref,
                   kta_hbm, va_hbm, ktb_hbm, vtb_hbm, lft_hbm,
                   kan_ref, van_ref, kbn_ref, vbn_ref, lfn_ref, upper_ref, oa_ref, ob_ref,
                   qbd_ref, m_ref, l_ref, acca_ref, accb_ref, fc_ref,
                   kta_buf, va_buf, ktb_buf, vtb_buf, lft_buf, sem):
    step_i = pl.program_id(1)
    n_steps = pl.num_programs(1)
    g = pl.program_id(0) * n_steps + step_i
    slot = g % 2
    pools = ((kta_hbm, kta_buf), (va_hbm, va_buf), (ktb_hbm, ktb_buf), (vtb_hbm, vtb_buf), (lft_hbm, lft_buf))

    def page_copies(gg, dst_slot):
        out = []
        for j in range(pps):
            pg = pt_ref[gg * pps + j]
            for k, (hbm, buf) in enumerate(pools):
                out.append(pltpu.make_async_copy(hbm.at[pg], buf.at[dst_slot, j], sem.at[dst_slot, k]))
        return out

    @pl.when(g == 0)
    def _():
        for c in page_copies(0, 0):
            c.start()

    @pl.when(g + 1 < pl.num_programs(0) * n_steps)
    def _():
        for c in page_copies(g + 1, 1 - slot):
            c.start()

    for c in page_copies(g, slot):
        c.wait()
    kta_refs, va_refs, ktb_refs, vtb_refs, lft_refs = (
        [buf.at[slot, j] for j in range(pps)] for _, buf in pools)
    rows = 32
    n_tok = 4
    pair_rows = kan_ref.shape[0]
    off = (pl.program_id(0) % 2) * n_tok
    page = LANES

    def row_ids(shape):
        r = lax.broadcasted_iota(jnp.int32, shape, 0)
        return ((r & 3, ((r >> 2) & 1) * H_A + (r >> 3), r >> 3),
                (r >> 3, r & 7, r & 7))

    def row_slope(head):
        return jnp.where(head == 0, SLOPES[0], jnp.where(head == 1, SLOPES[1],
               jnp.where(head == 2, SLOPES[2], SLOPES[3])))

    @pl.when(step_i == 0)
    def _():
        col_group = lax.broadcasted_iota(jnp.int32, (rows, 512), 1) >> 6
        for u, q_ref in enumerate((qa_ref, qb_ref)):
            tok, group, _ = row_ids((rows, 512))[u]
            qbd = jnp.zeros((rows, 512), F32)
            for t in range(n_tok):
                q_row = jnp.broadcast_to(q_ref[pl.ds(off + t, 1), :], (rows, 512))
                qbd = jnp.where((tok == t) & (group == col_group), q_row, qbd)
            qbd_ref[u] = qbd.astype(BF16)
        m_ref[...] = jnp.full(m_ref.shape, NEG, F32)
        l_ref[...] = jnp.zeros_like(l_ref)
        acca_ref[...] = jnp.zeros_like(acca_ref)
        accb_ref[...] = jnp.zeros_like(accb_ref)
        fc_ref[...] = jnp.zeros_like(fc_ref)

    def softmax_update(u, s):
        m_prev = m_ref[u][:, 0:1]
        l_prev = l_ref[u][:, 0:1]
        m_new = jnp.maximum(m_prev, jnp.max(s, axis=1, keepdims=True))
        alpha = jnp.exp(m_prev - m_new)
        pr = jnp.exp(s - m_new)
        l_new = alpha * l_prev + jnp.sum(pr, axis=1, keepdims=True)
        m_ref[u] = jnp.broadcast_to(m_new, (rows, LANES))
        l_ref[u] = jnp.broadcast_to(l_new, (rows, LANES))
        return alpha, pr

    is_last = step_i == n_steps - 1
    width = pps * page
    lane = lax.broadcasted_iota(jnp.int32, (rows, width), 1)
    tn = lax.broadcasted_iota(jnp.int32, (rows, page), 1)
    (tok_a, _, head_a), (tok_b, _, _) = row_ids((rows, page))
    pad_rows = lambda ref: jnp.concatenate(
        [ref[...], jnp.zeros((page - pair_rows, 512), F32)], axis=0).astype(BF16)

    def valid(tok):
        return is_last & (tn >= off) & (tn - off <= tok) & (tn < off + n_tok)

    dist = (past + row_ids((rows, width))[0][0] - (step_i * width + lane)).astype(F32)
    s_a = _dot(qbd_ref[0], jnp.concatenate([r[...].astype(BF16) for r in kta_refs], axis=1))
    s_a = s_a - row_slope(row_ids((rows, width))[0][2]) * dist
    s_an = _dot_nt(qbd_ref[0], pad_rows(kan_ref)) - row_slope(head_a) * (tok_a - (tn - off)).astype(F32)
    alpha, pr = softmax_update(0, jnp.concatenate([s_a, jnp.where(valid(tok_a), s_an, NEG)], axis=1))
    pr = pr.astype(BF16)
    v_all = jnp.concatenate(
        [jnp.concatenate([r[pl.ds(h, page, stride=H_A), :].astype(BF16) for h in range(H_A)], axis=1)
         for r in va_refs], axis=0)
    acca_ref[...] = (alpha * acca_ref[...] + _dot(pr[:, 0:width], v_all)
                     + _dot(pr[:, width:], pad_rows(van_ref)))

    parts = []
    for j in range(pps):
        parts.extend(v.astype(F32) for v in _split3(lft_refs[j][...]))
        parts.append(jnp.zeros((8, page), F32))
    cs = _dot(jnp.concatenate(parts, axis=0).astype(BF16), upper_ref[...])
    f_carry = fc_ref[...][:, 0:1]
    f_pages = []
    for j in range(pps):
        local = cs[32 * j:32 * j + 8] + cs[32 * j + 8:32 * j + 16] + cs[32 * j + 16:32 * j + 24]
        f_page = local + f_carry
        f_carry = f_carry + local[:, page - 1:page]
        f_pages.append(jnp.concatenate([f_page] * n_tok, axis=0))
    fc_ref[...] = jnp.broadcast_to(f_carry, (8, LANES))
    lfn = lfn_ref[0]
    t8 = lax.broadcasted_iota(jnp.int32, (8, page), 1)
    f_new = jnp.broadcast_to(f_carry, (8, page))
    for t in range(n_tok):
        lf_t = jnp.sum(jnp.where(t8 == off + t, lfn, 0.0), axis=1, keepdims=True)
        f_new = f_new + jnp.where(t8 >= off + t, lf_t, 0.0)
    s_b = _dot(qbd_ref[1], jnp.concatenate([r[...].astype(BF16) for r in ktb_refs], axis=1))
    s_b = s_b - jnp.concatenate(f_pages, axis=1)
    s_bn = _dot_nt(qbd_ref[1], pad_rows(kbn_ref)) - jnp.concatenate([f_new] * n_tok, axis=0)
    alpha, pr = softmax_update(1, jnp.concatenate([s_b, jnp.where(valid(tok_b), s_bn, NEG)], axis=1))
    pr = pr.astype(BF16)
    vt_all = jnp.concatenate([r[...].astype(BF16) for r in vtb_refs], axis=1)
    accb_ref[...] = (alpha * accb_ref[...] + _dot_nt(pr[:, 0:width], vt_all)
                     + _dot(pr[:, width:], pad_rows(vbn_ref)))

    @pl.when(is_last)
    def _():
        on_a = acca_ref[...] * (1.0 / l_ref[0][:, 0:1])
        head_b = row_ids((rows, 512))[1][2]
        col_head = lax.broadcasted_iota(jnp.int32, (rows, 512), 1) >> 6
        on_b = jnp.where(head_b == col_head, accb_ref[...] * (1.0 / l_ref[1][:, 0:1]), 0.0)

        def write(first):
            for h in range(H_A):
                blk = on_a[8 * h:8 * h + 8, h * DV_A:(h + 1) * DV_A]
                oa_ref[first:first + n_tok, h * DV_A:(h + 1) * DV_A] = (
                    blk[0:n_tok] - lam_ref[0] * blk[n_tok:2 * n_tok])
            for t in range(n_tok):
                ob_ref[first + t:first + t + 1, :] = jnp.sum(on_b[8 * t:8 * t + 8], axis=0, keepdims=True)

        pl.when(off == 0)(lambda: write(0))
        pl.when(off != 0)(lambda: write(n_tok))


def _decode(page_table, lam, qa, qb, pools, news, lfn_t, upper):
    n_seq, n_pages = page_table.shape
    page = pools[0].shape[2]
    pps = PAGES_PER_STEP
    n_rows = qa.shape[0]
    past = n_pages * page
    pair_spec = pl.BlockSpec((2 * (n_rows // n_seq), 512), lambda n, s, pt: (n // 2, 0))
    page_bufs = [pltpu.VMEM((2, pps) + pool.shape[1:], F32) for pool in pools]
    grid_spec = pltpu.PrefetchScalarGridSpec(
        num_scalar_prefetch=1,
        grid=(n_seq, n_pages // pps),
        in_specs=[pl.BlockSpec(memory_space=pltpu.SMEM), pair_spec, pair_spec]
                 + [pl.BlockSpec(memory_space=pl.ANY)] * len(pools)
                 + [pair_spec] * 4
                 + [pl.BlockSpec((1, H_B, page), lambda n, s, pt: (n // 2, 0, 0)),
                    pl.BlockSpec((page, page), lambda n, s, pt: (0, 0))],
        out_specs=[pair_spec, pair_spec],
        scratch_shapes=[pltpu.VMEM((2, 32, 512), BF16), pltpu.VMEM((2, 32, LANES), F32),
                        pltpu.VMEM((2, 32, LANES), F32), pltpu.VMEM((32, 512), F32),
                        pltpu.VMEM((32, 512), F32), pltpu.VMEM((H_B, LANES), F32)]
                       + page_bufs + [pltpu.SemaphoreType.DMA((2, len(pools)))],
    )
    return pl.pallas_call(
        functools.partial(_decode_kernel, past, pps),
        grid_spec=grid_spec,
        out_shape=[jax.ShapeDtypeStruct((n_rows, 512), F32)] * 2,
        compiler_params=pltpu.CompilerParams(dimension_semantics=("arbitrary", "arbitrary"),
                                             vmem_limit_bytes=VMEM_LIMIT),
        name="decode",
    )(page_table.reshape(-1), lam, qa, qb, *pools, *news, lfn_t, upper)


def _merge_kernel(x_ref, oa_ref, ob_ref, sga_ref, sgb_ref, sma_ref, smb_ref, gres_ref, gsub_ref,
                  woa_ref, wob_ref, wo_ref, y_ref):
    oa = oa_ref[...]
    heads = []
    for h in range(H_A):
        oh = oa[:, h * DV_A:(h + 1) * DV_A]
        heads.append(oh * lax.rsqrt(jnp.mean(oh * oh, axis=-1, keepdims=True) + EPS))
    oa = jnp.concatenate(heads, axis=1) * gsub_ref[...]
    ya = _dot((oa * sga_ref[...].astype(F32)).astype(BF16), woa_ref[...])
    yb = _dot((ob_ref[...] * sgb_ref[...].astype(F32)).astype(BF16), wob_ref[...])
    mix = sma_ref[...].astype(F32) * ya + smb_ref[...].astype(F32) * yb
    y = _dot(mix.astype(BF16), wo_ref[...])
    y_ref[...] = x_ref[...] + gres_ref[...] * y


def _merge(x2, oa, ob, sga, sgb, sma, smb, gres, gres_spec, wts):
    rows = x2.shape[0]
    row = lambda w: pl.BlockSpec((TM_MERGE, w), lambda i: (i, 0))
    consts = [wts["gsub"], wts["woa"], wts["wob"], wts["wo"]]
    return pl.pallas_call(
        _merge_kernel,
        grid=(rows // TM_MERGE,),
        in_specs=[row(D_MODEL), row(512), row(512), row(512), row(512), row(1024), row(1024), gres_spec]
                 + [_const_spec(c.shape) for c in consts],
        out_specs=row(D_MODEL),
        out_shape=jax.ShapeDtypeStruct((rows, D_MODEL), F32),
        compiler_params=pltpu.CompilerParams(dimension_semantics=("arbitrary",),
                                             vmem_limit_bytes=VMEM_LIMIT),
        name="merge",
    )(x2, oa, ob, sga, sgb, sma, smb, gres, *consts)


def _ones_constants():
    gsum = np.kron(np.eye(N_GROUPS), np.ones((DH_A, DH_A)))
    tri = np.tril(np.ones((TM, TM)))
    upper = np.triu(np.ones((LANES, LANES)))
    bf = lambda a: jnp.asarray(a, BF16)
    return dict(gsum=bf(gsum), tri=bf(tri), upper=bf(upper))


def _bf16_ceil(x):
    y = x.astype(BF16).astype(F32)
    return jnp.where(y < x, y * (1.0 + 2.0 ** -7), y)


def kernel(x_prompt, x_sample, cache_a_k, cache_a_v, cache_b_k, cache_b_v, cache_b_logf, page_table,
           c_prompt, c_sample, w_ada, b_ada, w_in, b_f, g_q_a, g_k_a, g_q_b, g_k_b,
           lambda_q1, lambda_k1, lambda_q2, lambda_k2, g_sub_a, w_out_a, w_out_b, w_o):
    assert w_ada.shape[0] == 1, "single-layer step"
    batch, seq, _ = x_prompt.shape
    n_seq, n_tok, _ = x_sample.shape
    n_pool, page = cache_a_k.shape[1], cache_a_k.shape[2]
    layer = 0
    lam_init = 0.8 - 0.6 * math.exp(-0.3 * layer)
    lam = (jnp.exp(jnp.sum(lambda_q1[layer] * lambda_k1[layer]))
           - jnp.exp(jnp.sum(lambda_q2[layer] * lambda_k2[layer])) + lam_init).reshape(1).astype(F32)

    w = w_in[layer]
    sec = np.cumsum((0, 512, 512, 512, 512, 512, 512, 512, H_B, 512, 1024, 1024))
    cols = lambda k: w[:, sec[k]:sec[k + 1]]
    wq = jnp.concatenate([cols(0), cols(4)], axis=1).astype(BF16)
    wn = jnp.concatenate([cols(1), cols(5), cols(2), cols(6), cols(3), cols(8), cols(9), cols(10)],
                         axis=1).astype(BF16)
    wf = jnp.pad(cols(7), ((0, 0), (0, LANES - H_B))).astype(BF16)
    gq = jnp.concatenate([jnp.tile(g_q_a[layer], N_GROUPS) * DH_A ** -0.5,
                          jnp.tile(g_q_b[layer], N_GROUPS) * DH_B ** -0.5])
    gk = jnp.concatenate([jnp.tile(g_k_a[layer], N_GROUPS), jnp.tile(g_k_b[layer], N_GROUPS)])
    wts = dict(_ones_constants(),
               wq=wq, wqkt=jnp.concatenate([wq, wn[:, 0:1024], wn[:, 1536:2048]], axis=1).T,
               wn=wn, wn_prompt=jnp.concatenate([wn[:, 1024:1536], wn[:, 2048:]], axis=1),

               wf=wf, bf=jnp.pad(b_f[layer], (0, LANES - H_B)).reshape(1, LANES),
               gqk_col=jnp.concatenate([gq, gk]).reshape(-1, 1), gq_row=gq.reshape(1, -1), gk_row=gk.reshape(1, -1),
               gsub=(jnp.tile(g_sub_a[layer], H_A) * (1.0 - lam_init)).reshape(1, W_A),
               woa=w_out_a[layer].astype(BF16), wob=w_out_b[layer].astype(BF16), wo=w_o[layer].astype(BF16))
    bound = lambda gq_, gk_: _bf16_ceil(8.1 * jnp.max(jnp.abs(gq_ * gk_)))
    mshift = jnp.stack([bound(g_q_a[layer], g_k_a[layer]), bound(g_q_b[layer], g_k_b[layer])]).astype(F32)

    n_c = batch + n_seq
    c_all = jnp.pad(jnp.concatenate([c_prompt, c_sample], axis=0), ((0, -n_c % 8), (0, 0)))
    mod = _ada(c_all, w_ada[layer], b_ada[layer])
    shift, scale, gres = (mod[:, k * D_MODEL:(k + 1) * D_MODEL] for k in range(3))

    xp2 = x_prompt.reshape(batch * seq, D_MODEL)
    p3 = lambda a: a[:batch].reshape(batch, 1, D_MODEL)
    (kta, va, ktb, vtbo, lft, qta, kaa, vta, qtb, kab, vtb, sga, sgb, sma, smb) = _proj_prompt(
        xp2, p3(scale), p3(shift), mshift, wts, batch, seq)
    online = jnp.max(mshift) > FAST_PATH_MAX_BOUND
    attend = lambda *a: lax.cond(online, lambda: _attention(*a, True), lambda: _attention(*a, False))
    oa = attend(qta, kaa, vta, lam, True).reshape(batch * seq, W_A)
    ob = attend(qtb, kab, vtb, lam, False).reshape(batch * seq, W_B)
    nb = seq // TM_MERGE
    gres_p_spec = pl.BlockSpec((None, 1, D_MODEL), lambda i: (i // nb, 0, 0))
    yp = _merge(xp2, oa, ob, sga, sgb, sma, smb, p3(gres), gres_p_spec, wts).reshape(batch, seq, D_MODEL)

    xs2 = x_sample.reshape(n_seq * n_tok, D_MODEL)
    rep = lambda a: jnp.repeat(a[batch:n_c], n_tok, axis=0)
    (qa_s, qb_s, ka_s, va_s, kb_s, vb_s, lf_s, sga_s, sgb_s, sma_s, smb_s) = _proj_sample(
        xs2, rep(scale), rep(shift), wts)
    pools = [jnp.transpose(cache_a_k[layer], (0, 2, 3, 4, 1)).reshape(n_pool, 512, page),
             cache_a_v[layer].reshape(n_pool, page * H_A, DV_A),
             jnp.transpose(cache_b_k[layer], (0, 2, 3, 1)).reshape(n_pool, 512, page),
             jnp.transpose(cache_b_v[layer], (0, 2, 3, 1)).reshape(n_pool, 512, page),
             jnp.swapaxes(cache_b_logf[layer], 1, 2)]
    lfn_t = jnp.pad(jnp.swapaxes(lf_s.reshape(n_seq // 2, 2 * n_tok, H_B), 1, 2),
                    ((0, 0), (0, 0), (0, page - 2 * n_tok)))
    oa_s, ob_s = _decode(page_table, lam, qa_s, qb_s, pools, [ka_s, va_s, kb_s, vb_s], lfn_t, wts["upper"])
    gres_s_spec = pl.BlockSpec((TM_MERGE, D_MODEL), lambda i: (i, 0))
    ys = _merge(xs2, oa_s, ob_s, sga_s, sgb_s, sma_s, smb_s,
                rep(gres), gres_s_spec, wts).reshape(n_seq, n_tok, D_MODEL)

    return (yp, ys,
            jnp.moveaxis(kta.reshape(1, batch, 2, H_A, DH_A, seq), -1, 2), va.reshape(1, batch, seq, H_A, DV_A),
            jnp.moveaxis(ktb.reshape(1, batch, H_B, DH_B, seq), -1, 2),
            jnp.moveaxis(vtbo.reshape(1, batch, H_B, DH_B, seq), -1, 2),
            jnp.moveaxis(lft.reshape(1, batch, H_B, seq), -1, 2),
            ka_s.reshape(1, n_seq, n_tok, 2, H_A, DH_A), va_s.reshape(1, n_seq, n_tok, H_A, DV_A),
            kb_s.reshape(1, n_seq, n_tok, H_B, DH_B), vb_s.reshape(1, n_seq, n_tok, H_B, DH_B),
            lf_s.reshape(1, n_seq, n_tok, H_B))
```

```python
import functools
import math

import numpy as np
import jax
import jax.numpy as jnp
from jax import lax
from jax.experimental import pallas as pl
from jax.experimental.pallas import tpu as pltpu

F32 = jnp.float32
BF16 = jnp.bfloat16

D_MODEL = 1024
H_A, DH_A, DV_A = 4, 64, 128
H_B, DH_B = 8, 64
W_A = H_A * DV_A
W_B = H_B * DH_B
N_GROUPS = 8
EPS = 1e-6
NEG = -1e30
SLOPES = tuple(2.0 ** (-8.0 * (h + 1) / H_A) for h in range(H_A))
LANES = 128

TM = 256
TM_MERGE = 512
TQ = 512
TK = 512
VT_CHUNK = TM
PAGES_PER_STEP = 16
FAST_PATH_MAX_BOUND = 30.0
VMEM_LIMIT = 56 * 1024 * 1024


def _dot(a, b):
    return jnp.dot(a, b, preferred_element_type=F32)


def _dot_nt(a, b):
    return lax.dot_general(a, b, (((1,), (1,)), ((), ())), preferred_element_type=F32)


def _split3(x):
    hi = x.astype(BF16)
    r1 = x - hi.astype(F32)
    mid = r1.astype(BF16)
    lo = (r1 - mid.astype(F32)).astype(BF16)
    return hi, mid, lo


def _silu(x):
    return x * jax.nn.sigmoid(x)


def _const_spec(shape):
    nd = len(shape)
    return pl.BlockSpec(shape, lambda *_: (0,) * nd, pipeline_mode=pl.Buffered(1))


def _ada_kernel(c_ref, w_ref, b_ref, o_ref):
    c = c_ref[...]
    o_ref[...] = _dot(_silu(c).astype(BF16), w_ref[...].astype(BF16)) + b_ref[...]


def _ada(c_all, w_ada, b_ada):
    rows = c_all.shape[0]
    n = w_ada.shape[1]
    bn = 1024
    return pl.pallas_call(
        _ada_kernel,
        grid=(n // bn,),
        in_specs=[pl.BlockSpec((rows, D_MODEL), lambda j: (0, 0)),
                  pl.BlockSpec((D_MODEL, bn), lambda j: (0, j)),
                  pl.BlockSpec((1, bn), lambda j: (0, j))],
        out_specs=pl.BlockSpec((rows, bn), lambda j: (0, j)),
        out_shape=jax.ShapeDtypeStruct((rows, n), F32),
        name="ada",
    )(c_all, w_ada, b_ada.reshape(1, n))


def _hidden(x_ref, scale_ref, shift_ref):
    x = x_ref[...]
    ms = jnp.mean(x * x, axis=-1, keepdims=True)
    h = x * lax.rsqrt(ms + EPS) * (1.0 + scale_ref[...]) + shift_ref[...]
    return h.astype(BF16)


def _group_rms(z, g_ref):
    zz = z * z
    hi = zz.astype(BF16)
    lo = (zz - hi.astype(F32)).astype(BF16)
    ss = _dot(hi, g_ref[...]) + _dot(lo, g_ref[...])
    return z * lax.rsqrt(ss * (1.0 / DH_A) + EPS)


def _log_sigmoid(z):
    return jnp.minimum(z, 0.0) - jnp.log1p(jnp.exp(-jnp.abs(z)))


def _logf(hb, wf_ref, bf_ref):
    z = _dot(hb, wf_ref[...]) + bf_ref[...]
    lane = lax.broadcasted_iota(jnp.int32, z.shape, 1)
    return jnp.where(lane < H_B, _log_sigmoid(z), 0.0)


def _proj_prompt_kernel(nb, mshift_ref, x_ref, scale_ref, shift_ref, wqkt_ref, wn_ref, wf_ref, bf_ref,
                        gqk_ref, tri_ref,
                        kta_ref, va_ref, ktb_ref, vtbo_ref, lft_ref,
                        qta_ref, kaa_ref, vta_ref, qtb_ref, kab_ref, vtb_ref,
                        sga_ref, sgb_ref, sma_ref, smb_ref, carry_ref):
    tm = x_ref.shape[0]
    ib = pl.program_id(0) % nb
    hb = _hidden(x_ref, scale_ref, shift_ref)

    lf = _logf(hb, wf_ref, bf_ref)
    lft_ref[0] = lf.T[0:H_B, :]
    hi, mid, lo = _split3(lf)
    tri = tri_ref[...]
    f_local = _dot(tri, hi) + _dot(tri, mid) + _dot(tri, lo)

    @pl.when(ib == 0)
    def _():
        carry_ref[...] = jnp.zeros_like(carry_ref)

    f_nat = f_local + carry_ref[...]
    carry_ref[...] = f_nat[tm - 1:tm, :]
    f_t = f_nat.T[0:H_B, :]

    n_g = 4 * N_GROUPS
    n_qk = n_g * DH_A
    qkt = jnp.concatenate([_dot_nt(wqkt_ref[0:n_qk // 2, :], hb),
                           _dot_nt(wqkt_ref[n_qk // 2:n_qk, :], hb)], axis=0)
    g3 = qkt.reshape(n_g, DH_A, tm)
    ssq = jnp.sum(g3 * g3, axis=1, keepdims=True)
    g3 = (qkt * gqk_ref[...]).reshape(n_g, DH_A, tm) * lax.rsqrt(ssq * (1.0 / DH_A) + EPS)
    kta_ref[0] = g3[2 * N_GROUPS:3 * N_GROUPS].reshape(N_GROUPS * DH_A, tm)
    ktb_ref[0] = g3[3 * N_GROUPS:4 * N_GROUPS].reshape(N_GROUPS * DH_A, tm)

    r8 = lax.broadcasted_iota(jnp.int32, (8, tm), 0)
    pos = ib * tm + lax.broadcasted_iota(jnp.int32, (8, tm), 1)
    pos_hi = (pos >> 7).astype(F32)
    pos_lo = (pos & 127).astype(F32)
    zeros_tail = jnp.zeros((LANES - DH_A - 8, tm), F32)
    m_a = mshift_ref[0]
    m_b = mshift_ref[1]

    def operand(group, aug):
        return jnp.concatenate([g3[group], aug, zeros_tail], axis=0)

    k_aug_a = jnp.where(r8 == 2, pos_hi, jnp.where(r8 == 3, pos_lo,
              jnp.where((r8 <= 1) | (r8 == 4), 1.0, 0.0)))
    for g in range(N_GROUPS):
        slope = SLOPES[g % H_A]
        q_aug = jnp.where(r8 == 0, -slope * 128.0 * pos_hi,
                jnp.where(r8 == 1, -slope * pos_lo,
                jnp.where(r8 == 2, slope * 128.0,
                jnp.where(r8 == 3, slope,
                jnp.where(r8 == 4, -m_a, 0.0)))))
        qta_ref[0, g * LANES:(g + 1) * LANES, :] = operand(g, q_aug).astype(BF16)
        kaa_ref[0, :, g * LANES:(g + 1) * LANES] = operand(2 * N_GROUPS + g, k_aug_a).T.astype(BF16)
    ft_hi, ft_mid, ft_lo = _split3(f_t)
    for g in range(N_GROUPS):
        bc = lambda v: jnp.broadcast_to(v[g:g + 1, :].astype(F32), (8, tm))
        q_aug = jnp.where(r8 == 0, bc(ft_hi),
                jnp.where(r8 == 1, bc(ft_mid),
                jnp.where(r8 == 2, bc(ft_lo),
                jnp.where(r8 <= 5, -1.0,
                jnp.where(r8 == 6, -m_b, 0.0)))))
        k_aug = jnp.where(r8 <= 2, 1.0,
                jnp.where(r8 == 3, bc(ft_hi),
                jnp.where(r8 == 4, bc(ft_mid),
                jnp.where(r8 == 5, bc(ft_lo),
                jnp.where(r8 == 6, 1.0, 0.0)))))
        qtb_ref[0, g * LANES:(g + 1) * LANES, :] = operand(N_GROUPS + g, q_aug).astype(BF16)
        kab_ref[0, :, g * LANES:(g + 1) * LANES] = operand(3 * N_GROUPS + g, k_aug).T.astype(BF16)

    va = _dot(hb, wn_ref[:, 0:512])
    for h in range(H_A):
        va_ref[pl.ds(h, tm, stride=H_A), :] = va[:, h * DV_A:(h + 1) * DV_A]
    vta_ref[0, 0] = va.T.astype(BF16)
    vbt = _dot_nt(wqkt_ref[n_qk:, :], hb)
    vtbo_ref[0] = vbt
    vtb_ref[0, 0] = vbt.astype(BF16)

    sga_ref[...] = _silu(_dot(hb, wn_ref[:, 512:1024])).astype(BF16)
    sgb_ref[...] = _silu(_dot(hb, wn_ref[:, 1024:1536])).astype(BF16)
    sma_ref[...] = jax.nn.sigmoid(_dot(hb, wn_ref[:, 1536:2560])).astype(BF16)
    smb_ref[...] = jax.nn.sigmoid(_dot(hb, wn_ref[:, 2560:3584])).astype(BF16)


def _proj_prompt(x2, scale, shift, mshift, wts, batch, seq):
    rows = batch * seq
    nb = seq // TM
    nk = seq // VT_CHUNK
    row = lambda w: pl.BlockSpec((TM, w), lambda i: (i, 0))
    mod = pl.BlockSpec((None, 1, D_MODEL), lambda i: (i // nb, 0, 0))
    consts = [wts["wqkt"], wts["wn_prompt"], wts["wf"], wts["bf"], wts["gqk_col"], wts["tri"]]
    seq_t = lambda r: jax.ShapeDtypeStruct((batch, r, seq), F32)
    out_shape = [seq_t(512), jax.ShapeDtypeStruct((rows * H_A, DV_A), F32), seq_t(512), seq_t(512),
                 seq_t(H_B)] + [
        jax.ShapeDtypeStruct((batch, 1024, seq), BF16), jax.ShapeDtypeStruct((batch, seq, 1024), BF16),
        jax.ShapeDtypeStruct((batch, nk, W_A, VT_CHUNK), BF16),
        jax.ShapeDtypeStruct((batch, 1024, seq), BF16), jax.ShapeDtypeStruct((batch, seq, 1024), BF16),
        jax.ShapeDtypeStruct((batch, nk, W_B, VT_CHUNK), BF16),
        jax.ShapeDtypeStruct((rows, 512), BF16), jax.ShapeDtypeStruct((rows, 512), BF16),
        jax.ShapeDtypeStruct((rows, 1024), BF16), jax.ShapeDtypeStruct((rows, 1024), BF16)]
    qt_spec = pl.BlockSpec((1, 1024, TM), lambda i: (i // nb, 0, i % nb))
    k_spec = pl.BlockSpec((1, TM, 1024), lambda i: (i // nb, i % nb, 0))
    vt_spec = pl.BlockSpec((1, TM // VT_CHUNK, 512, VT_CHUNK), lambda i: (i // nb, i % nb, 0, 0))
    out_t = lambda r: pl.BlockSpec((1, r, TM), lambda i: (i // nb, 0, i % nb))
    out_specs = [out_t(512), pl.BlockSpec((TM * H_A, DV_A), lambda i: (i, 0)), out_t(512), out_t(512),
                 out_t(H_B)] + [
        qt_spec, k_spec, vt_spec, qt_spec, k_spec, vt_spec, row(512), row(512), row(1024), row(1024)]
    return pl.pallas_call(
        functools.partial(_proj_prompt_kernel, nb),
        grid=(rows // TM,),
        in_specs=[pl.BlockSpec(memory_space=pltpu.SMEM), row(D_MODEL), mod, mod]
                 + [_const_spec(c.shape) for c in consts],
        out_specs=out_specs,
        out_shape=out_shape,
        scratch_shapes=[pltpu.VMEM((1, LANES), F32)],
        compiler_params=pltpu.CompilerParams(dimension_semantics=("arbitrary",),
                                             vmem_limit_bytes=VMEM_LIMIT),
        name="proj_prompt",
    )(mshift, x2, scale, shift, *consts)


def _proj_sample_kernel(x_ref, scale_ref, shift_ref, wq_ref, wn_ref, wf_ref, bf_ref, gq_ref, gk_ref, g_ref,
                        qa_ref, qb_ref, ka_ref, va_ref, kb_ref, vb_ref, lf_ref,
                        sga_ref, sgb_ref, sma_ref, smb_ref):
    hb = _hidden(x_ref, scale_ref, shift_ref)
    lf_ref[...] = _logf(hb, wf_ref, bf_ref)[:, :H_B]
    qa_ref[...] = _group_rms(_dot(hb, wq_ref[:, 0:512]), g_ref) * gq_ref[:, 0:512]
    qb_ref[...] = _group_rms(_dot(hb, wq_ref[:, 512:1024]), g_ref) * gq_ref[:, 512:1024]
    ka_ref[...] = _group_rms(_dot(hb, wn_ref[:, 0:512]), g_ref) * gk_ref[:, 0:512]
    kb_ref[...] = _group_rms(_dot(hb, wn_ref[:, 512:1024]), g_ref) * gk_ref[:, 512:1024]
    va_ref[...] = _dot(hb, wn_ref[:, 1024:1536])
    vb_ref[...] = _dot(hb, wn_ref[:, 1536:2048])
    sga_ref[...] = _silu(_dot(hb, wn_ref[:, 2048:2560])).astype(BF16)
    sgb_ref[...] = _silu(_dot(hb, wn_ref[:, 2560:3072])).astype(BF16)
    sma_ref[...] = jax.nn.sigmoid(_dot(hb, wn_ref[:, 3072:4096])).astype(BF16)
    smb_ref[...] = jax.nn.sigmoid(_dot(hb, wn_ref[:, 4096:5120])).astype(BF16)


def _proj_sample(x2, scale_rows, shift_rows, wts):
    rows = x2.shape[0]
    row = lambda w: pl.BlockSpec((TM, w), lambda i: (i, 0))
    consts = [wts["wq"], wts["wn"], wts["wf"], wts["bf"], wts["gq_row"], wts["gk_row"], wts["gsum"]]
    out_shape = [jax.ShapeDtypeStruct((rows, 512), F32)] * 6 + [jax.ShapeDtypeStruct((rows, H_B), F32)] + [
        jax.ShapeDtypeStruct((rows, 512), BF16), jax.ShapeDtypeStruct((rows, 512), BF16),
        jax.ShapeDtypeStruct((rows, 1024), BF16), jax.ShapeDtypeStruct((rows, 1024), BF16)]
    out_specs = [row(512)] * 6 + [row(H_B), row(512), row(512), row(1024), row(1024)]
    return pl.pallas_call(
        _proj_sample_kernel,
        grid=(rows // TM,),
        in_specs=[row(D_MODEL), row(D_MODEL), row(D_MODEL)] + [_const_spec(c.shape) for c in consts],
        out_specs=out_specs,
        out_shape=out_shape,
        compiler_params=pltpu.CompilerParams(dimension_semantics=("arbitrary",),
                                             vmem_limit_bytes=VMEM_LIMIT),
        name="proj_sample",
    )(x2, scale_rows, shift_rows, *consts)


def _attn_kernel(shared_v, online, lam_ref, qt0_ref, qt1_ref, k0_ref, k1_ref, vt_ref, o_ref,
                 acc_ref, l_ref, p_ref, p2_ref, p3_ref, p4_ref, pd_ref):
    i = pl.program_id(2)
    qts = (qt0_ref[0], qt1_ref[0])
    k_refs = (k0_ref, k1_ref)
    dv = acc_ref.shape[1]
    n_chunks = TK // VT_CHUNK
    acc_ref[...] = jnp.zeros_like(acc_ref)

    def pv(u, chunk, p):
        blk = vt_ref[0, chunk]
        v = blk if shared_v else blk[u * dv:(u + 1) * dv, :]
        return _dot(v, p)

    def scores_to_probs(j, masked, dst_ref):
        for u in range(2):
            kblk = k_refs[u][0, pl.ds(pl.multiple_of(j * TK, TK), TK), :]
            s = _dot(kblk, qts[u])
            if masked:
                kpos = lax.broadcasted_iota(jnp.int32, s.shape, 0)
                qpos = lax.broadcasted_iota(jnp.int32, s.shape, 1)
                s = jnp.where(kpos <= qpos, s, NEG)
            p = jnp.exp(s)
            l_ref[u] += jnp.sum(p.reshape(TK // 8, 8, TQ), axis=0)
            dst_ref[u] = p.astype(BF16)

    def probs_times_values(j, src_ref):
        for u in range(2):
            upd = None
            for c in range(n_chunks):
                d = pv(u, j * n_chunks + c, src_ref[u, c * VT_CHUNK:(c + 1) * VT_CHUNK, :])
                upd = d if upd is None else upd + d
            acc_ref[u] += upd

    def step_online(j, carry, masked):
        out = []
        for u in range(2):
            kblk = k_refs[u][0, pl.ds(pl.multiple_of(j * TK, TK), TK), :]
            s = _dot(kblk, qts[u])
            if masked:
                kpos = lax.broadcasted_iota(jnp.int32, s.shape, 0)
                qpos = lax.broadcasted_iota(jnp.int32, s.shape, 1)
                s = jnp.where(kpos <= qpos, s, NEG)
            m, l = carry[u]
            m_new = jnp.maximum(m, jnp.max(s, axis=0, keepdims=True))
            alpha = jnp.exp(m - m_new)
            p = jnp.exp(s - m_new)
            l = alpha * l + jnp.sum(p, axis=0, keepdims=True)
            p = p.astype(BF16)
            upd = None
            for c in range(n_chunks):
                d = pv(u, j * n_chunks + c, p[c * VT_CHUNK:(c + 1) * VT_CHUNK, :])
                upd = d if upd is None else upd + d
            acc_ref[u] = alpha * acc_ref[u] + upd
            out.append((m_new, l))
        return tuple(out)

    if online:
        init = tuple((jnp.full((1, TQ), NEG, F32), jnp.zeros((1, TQ), F32)) for _ in range(2))
        carry = lax.fori_loop(0, i, lambda j, c: step_online(j, c, False), init)
        carry = step_online(i, carry, True)
        l0, l1 = carry[0][1], carry[1][1]
    else:
        l_ref[...] = jnp.zeros_like(l_ref)

        @pl.when(i == 0)
        def _():
            scores_to_probs(0, True, pd_ref)
            probs_times_values(0, pd_ref)

        @pl.when(i > 0)
        def _():
            scores_to_probs(0, False, p_ref)

            def pair(j):
                scores_to_probs(j, False, p2_ref)
                probs_times_values(j - 1, p_ref)
                scores_to_probs(j + 1, False, p_ref)
                probs_times_values(j, p2_ref)

            def two_pairs(jj, _):
                j = 4 * jj + 1
                scores_to_probs(j, False, p2_ref)
                probs_times_values(j - 1, p_ref)
                scores_to_probs(j + 1, False, p3_ref)
                probs_times_values(j, p2_ref)
                scores_to_probs(j + 2, False, p4_ref)
                probs_times_values(j + 1, p3_ref)
                scores_to_probs(j + 3, False, p_ref)
                probs_times_values(j + 2, p4_ref)
                return 0

            n_pairs = (i - 1) // 2
            lax.fori_loop(0, n_pairs // 2, two_pairs, 0)
            pl.when(n_pairs % 2 == 1)(lambda: pair(2 * n_pairs - 1))

            @pl.when(i % 2 == 0)
            def _():
                scores_to_probs(i - 1, False, p2_ref)
                probs_times_values(i - 2, p_ref)
                scores_to_probs(i, True, pd_ref)
                probs_times_values(i - 1, p2_ref)
                probs_times_values(i, pd_ref)

            @pl.when(i % 2 == 1)
            def _():
                scores_to_probs(i, True, pd_ref)
                probs_times_values(i - 1, p_ref)
                probs_times_values(i, pd_ref)

        l0, l1 = (jnp.sum(l_ref[u], axis=0, keepdims=True) for u in range(2))

    o0 = acc_ref[0] * (1.0 / l0)
    o1 = acc_ref[1] * (1.0 / l1)
    o = o0 - lam_ref[0] * o1 if shared_v else jnp.concatenate([o0, o1], axis=0)
    o_ref[0] = o.T


def _attention(qt, kaug, vt, lam, shared_v, online):
    batch, _, seq = qt.shape
    n_steps = 4
    if shared_v:
        g0 = lambda h: h
        g1 = lambda h: H_A + h
    else:
        g0 = lambda h: 2 * h
        g1 = lambda h: 2 * h + 1
    dv = DV_A if shared_v else DH_B
    qspec = lambda g: pl.BlockSpec((1, LANES, TQ), lambda b, h, i: (b, g(h), i))
    kspec = lambda g: pl.BlockSpec((1, seq, LANES), lambda b, h, i: (b, 0, g(h)))
    return pl.pallas_call(
        functools.partial(_attn_kernel, shared_v, online),
        grid=(batch, n_steps, seq // TQ),
        in_specs=[pl.BlockSpec(memory_space=pltpu.SMEM), qspec(g0), qspec(g1), kspec(g0), kspec(g1),
                  pl.BlockSpec((1, seq // VT_CHUNK, LANES, VT_CHUNK), lambda b, h, i: (b, 0, h, 0))],
        out_specs=pl.BlockSpec((1, TQ, LANES), lambda b, h, i: (b, i, h)),
        out_shape=jax.ShapeDtypeStruct((batch, seq, 512), F32),
        scratch_shapes=[pltpu.VMEM((2, dv, TQ), F32), pltpu.VMEM((2, 8, TQ), F32),
                        ] + [pltpu.VMEM((2, TK, TQ), BF16)] * 5,
        compiler_params=pltpu.CompilerParams(
            dimension_semantics=("arbitrary", "arbitrary", "arbitrary"), vmem_limit_bytes=VMEM_LIMIT),
        name=("attn_a" if shared_v else "attn_b") + ("_online" if online else ""),
    )(lam, qt, qt, kaug, kaug, vt)


def _decode_kernel(past, pps, pt_ref, lam_ref, qa_ref, qb_ref,
                   kta_hbm, va_hbm, ktb_hbm, vtb_hbm, lft_hbm,
                   kan_ref, van_ref, kbn_ref, vbn_ref, lfn_ref, upper_ref, oa_ref, ob_ref,
                   qbd_ref, m_ref, l_ref, acca_ref, accb_ref, fc_ref,
                   kta_buf, va_buf, ktb_buf, vtb_buf, lft_buf, sem):
    step_i = pl.program_id(1)
    n_steps = pl.num_programs(1)
    g = pl.program_id(0) * n_steps + step_i
    slot = g % 2
    pools = ((kta_hbm, kta_buf), (va_hbm, va_buf), (ktb_hbm, ktb_buf), (vtb_hbm, vtb_buf), (lft_hbm, lft_buf))

    def page_copies(gg, dst_slot):
        out = []
        for j in range(pps):
            pg = pt_ref[gg * pps + j]
            for k, (hbm, buf) in enumerate(pools):
                out.append(pltpu.make_async_copy(hbm.at[pg], buf.at[dst_slot, j], sem.at[dst_slot, k]))
        return out

    def start_all(copies):
        for n, c in enumerate(copies):
            c.start(priority=n % 2)

    @pl.when(g == 0)
    def _():
        start_all(page_copies(0, 0))

    @pl.when(g + 1 < pl.num_programs(0) * n_steps)
    def _():
        start_all(page_copies(g + 1, 1 - slot))

    for c in page_copies(g, slot):
        c.wait()
    kta_refs, va_refs, ktb_refs, vtb_refs, lft_refs = (
        [buf.at[slot, j] for j in range(pps)] for _, buf in pools)
    rows = 32
    n_tok = 4
    pair_rows = kan_ref.shape[0]
    off = (pl.program_id(0) % 2) * n_tok
    page = LANES

    def row_ids(shape):
        r = lax.broadcasted_iota(jnp.int32, shape, 0)
        return ((r & 3, ((r >> 2) & 1) * H_A + (r >> 3), r >> 3),
                (r >> 3, r & 7, r & 7))

    def row_slope(head):
        return jnp.where(head == 0, SLOPES[0], jnp.where(head == 1, SLOPES[1],
               jnp.where(head == 2, SLOPES[2], SLOPES[3])))

    @pl.when(step_i == 0)
    def _():
        col_group = lax.broadcasted_iota(jnp.int32, (rows, 512), 1) >> 6
        for u, q_ref in enumerate((qa_ref, qb_ref)):
            tok, group, _ = row_ids((rows, 512))[u]
            qbd = jnp.zeros((rows, 512), F32)
            for t in range(n_tok):
                q_row = jnp.broadcast_to(q_ref[pl.ds(off + t, 1), :], (rows, 512))
                qbd = jnp.where((tok == t) & (group == col_group), q_row, qbd)
            qbd_ref[u] = qbd.astype(BF16)
        m_ref[...] = jnp.full(m_ref.shape, NEG, F32)
        l_ref[...] = jnp.zeros_like(l_ref)
        acca_ref[...] = jnp.zeros_like(acca_ref)
        accb_ref[...] = jnp.zeros_like(accb_ref)
        fc_ref[...] = jnp.zeros_like(fc_ref)

    def softmax_update(u, s):
        m_prev = m_ref[u][:, 0:1]
        l_prev = l_ref[u][:, 0:1]
        m_new = jnp.maximum(m_prev, jnp.max(s, axis=1, keepdims=True))
        alpha = jnp.exp(m_prev - m_new)
        pr = jnp.exp(s - m_new)
        l_new = alpha * l_prev + jnp.sum(pr, axis=1, keepdims=True)
        m_ref[u] = jnp.broadcast_to(m_new, (rows, LANES))
        l_ref[u] = jnp.broadcast_to(l_new, (rows, LANES))
        return alpha, pr

    is_last = step_i == n_steps - 1
    width = pps * page
    lane = lax.broadcasted_iota(jnp.int32, (rows, width), 1)
    tn = lax.broadcasted_iota(jnp.int32, (rows, page), 1)
    (tok_a, _, head_a), (tok_b, _, _) = row_ids((rows, page))
    pad_rows = lambda ref: jnp.concatenate(
        [ref[...], jnp.zeros((page - pair_rows, 512), F32)], axis=0).astype(BF16)

    def valid(tok):
        return is_last & (tn >= off) & (tn - off <= tok) & (tn < off + n_tok)

    dist = (past + row_ids((rows, width))[0][0] - (step_i * width + lane)).astype(F32)
    s_a = _dot(qbd_ref[0], jnp.concatenate([r[...].astype(BF16) for r in kta_refs], axis=1))
    s_a = s_a - row_slope(row_ids((rows, width))[0][2]) * dist
    s_an = _dot_nt(qbd_ref[0], pad_rows(kan_ref)) - row_slope(head_a) * (tok_a - (tn - off)).astype(F32)
    alpha, pr = softmax_update(0, jnp.concatenate([s_a, jnp.where(valid(tok_a), s_an, NEG)], axis=1))
    pr = pr.astype(BF16)
    v_all = jnp.concatenate(
        [jnp.concatenate([r[pl.ds(h, page, stride=H_A), :].astype(BF16) for h in range(H_A)], axis=1)
         for r in va_refs], axis=0)
    acca_ref[...] = (alpha * acca_ref[...] + _dot(pr[:, 0:width], v_all)
                     + _dot(pr[:, width:], pad_rows(van_ref)))

    parts = []
    for j in range(pps):
        parts.extend(v.astype(F32) for v in _split3(lft_refs[j][...]))
        parts.append(jnp.zeros((8, page), F32))
    cs = _dot(jnp.concatenate(parts, axis=0).astype(BF16), upper_ref[...])
    f_carry = fc_ref[...][:, 0:1]
    f_pages = []
    for j in range(pps):
        local = cs[32 * j:32 * j + 8] + cs[32 * j + 8:32 * j + 16] + cs[32 * j + 16:32 * j + 24]
        f_page = local + f_carry
        f_carry = f_carry + local[:, page - 1:page]
        f_pages.append(jnp.concatenate([f_page] * n_tok, axis=0))
    fc_ref[...] = jnp.broadcast_to(f_carry, (8, LANES))
    lfn = lfn_ref[0]
    t8 = lax.broadcasted_iota(jnp.int32, (8, page), 1)
    f_new = jnp.broadcast_to(f_carry, (8, page))
    for t in range(n_tok):
        lf_t = jnp.sum(jnp.where(t8 == off + t, lfn, 0.0), axis=1, keepdims=True)
        f_new = f_new + jnp.where(t8 >= off + t, lf_t, 0.0)
    s_b = _dot(qbd_ref[1], jnp.concatenate([r[...].astype(BF16) for r in ktb_refs], axis=1))
    s_b = s_b - jnp.concatenate(f_pages, axis=1)
    s_bn = _dot_nt(qbd_ref[1], pad_rows(kbn_ref)) - jnp.concatenate([f_new] * n_tok, axis=0)
    alpha, pr = softmax_update(1, jnp.concatenate([s_b, jnp.where(valid(tok_b), s_bn, NEG)], axis=1))
    pr = pr.astype(BF16)
    vt_all = jnp.concatenate([r[...].astype(BF16) for r in vtb_refs], axis=1)
    accb_ref[...] = (alpha * accb_ref[...] + _dot_nt(pr[:, 0:width], vt_all)
                     + _dot(pr[:, width:], pad_rows(vbn_ref)))

    @pl.when(is_last)
    def _():
        on_a = acca_ref[...] * (1.0 / l_ref[0][:, 0:1])
        head_b = row_ids((rows, 512))[1][2]
        col_head = lax.broadcasted_iota(jnp.int32, (rows, 512), 1) >> 6
        on_b = jnp.where(head_b == col_head, accb_ref[...] * (1.0 / l_ref[1][:, 0:1]), 0.0)

        def write(first):
            for h in range(H_A):
                blk = on_a[8 * h:8 * h + 8, h * DV_A:(h + 1) * DV_A]
                oa_ref[first:first + n_tok, h * DV_A:(h + 1) * DV_A] = (
                    blk[0:n_tok] - lam_ref[0] * blk[n_tok:2 * n_tok])
            for t in range(n_tok):
                ob_ref[first + t:first + t + 1, :] = jnp.sum(on_b[8 * t:8 * t + 8], axis=0, keepdims=True)

        pl.when(off == 0)(lambda: write(0))
        pl.when(off != 0)(lambda: write(n_tok))


def _decode(page_table, lam, qa, qb, pools, news, lfn_t, upper):
    n_seq, n_pages = page_table.shape
    page = pools[0].shape[2]
    pps = PAGES_PER_STEP
    n_rows = qa.shape[0]
    past = n_pages * page
    pair_spec = pl.BlockSpec((2 * (n_rows // n_seq), 512), lambda n, s, pt: (n // 2, 0))
    page_bufs = [pltpu.VMEM((2, pps) + pool.shape[1:], F32) for pool in pools]
    grid_spec = pltpu.PrefetchScalarGridSpec(
        num_scalar_prefetch=1,
        grid=(n_seq, n_pages // pps),
        in_specs=[pl.BlockSpec(memory_space=pltpu.SMEM), pair_spec, pair_spec]
                 + [pl.BlockSpec(memory_space=pl.ANY)] * len(pools)
                 + [pair_spec] * 4
                 + [pl.BlockSpec((1, H_B, page), lambda n, s, pt: (n // 2, 0, 0)),
                    pl.BlockSpec((page, page), lambda n, s, pt: (0, 0))],
        out_specs=[pair_spec, pair_spec],
        scratch_shapes=[pltpu.VMEM((2, 32, 512), BF16), pltpu.VMEM((2, 32, LANES), F32),
                        pltpu.VMEM((2, 32, LANES), F32), pltpu.VMEM((32, 512), F32),
                        pltpu.VMEM((32, 512), F32), pltpu.VMEM((H_B, LANES), F32)]
                       + page_bufs + [pltpu.SemaphoreType.DMA((2, len(pools)))],
    )
    return pl.pallas_call(
        functools.partial(_decode_kernel, past, pps),
        grid_spec=grid_spec,
        out_shape=[jax.ShapeDtypeStruct((n_rows, 512), F32)] * 2,
        compiler_params=pltpu.CompilerParams(dimension_semantics=("arbitrary", "arbitrary"),
                                             vmem_limit_bytes=VMEM_LIMIT),
        name="decode",
    )(page_table.reshape(-1), lam, qa, qb, *pools, *news, lfn_t, upper)


def _merge_kernel(x_ref, oa_ref, ob_ref, sga_ref, sgb_ref, sma_ref, smb_ref, gres_ref, gsub_ref,
                  woa_ref, wob_ref, wo_ref, y_ref):
    oa = oa_ref[...]
    heads = []
    for h in range(H_A):
        oh = oa[:, h * DV_A:(h + 1) * DV_A]
        heads.append(oh * lax.rsqrt(jnp.mean(oh * oh, axis=-1, keepdims=True) + EPS))
    oa = jnp.concatenate(heads, axis=1) * gsub_ref[...]
    ya = _dot((oa * sga_ref[...].astype(F32)).astype(BF16), woa_ref[...])
    yb = _dot((ob_ref[...] * sgb_ref[...].astype(F32)).astype(BF16), wob_ref[...])
    mix = sma_ref[...].astype(F32) * ya + smb_ref[...].astype(F32) * yb
    y = _dot(mix.astype(BF16), wo_ref[...])
    y_ref[...] = x_ref[...] + gres_ref[...] * y


def _merge(x2, oa, ob, sga, sgb, sma, smb, gres, gres_spec, wts):
    rows = x2.shape[0]
    row = lambda w: pl.BlockSpec((TM_MERGE, w), lambda i: (i, 0))
    consts = [wts["gsub"], wts["woa"], wts["wob"], wts["wo"]]
    return pl.pallas_call(
        _merge_kernel,
        grid=(rows // TM_MERGE,),
        in_specs=[row(D_MODEL), row(512), row(512), row(512), row(512), row(1024), row(1024), gres_spec]
                 + [_const_spec(c.shape) for c in consts],
        out_specs=row(D_MODEL),
        out_shape=jax.ShapeDtypeStruct((rows, D_MODEL), F32),
        compiler_params=pltpu.CompilerParams(dimension_semantics=("arbitrary",),
                                             vmem_limit_bytes=VMEM_LIMIT),
        name="merge",
    )(x2, oa, ob, sga, sgb, sma, smb, gres, *consts)


def _ones_constants():
    gsum = np.kron(np.eye(N_GROUPS), np.ones((DH_A, DH_A)))
    tri = np.tril(np.ones((TM, TM)))
    upper = np.triu(np.ones((LANES, LANES)))
    bf = lambda a: jnp.asarray(a, BF16)
    return dict(gsum=bf(gsum), tri=bf(tri), upper=bf(upper))


def _bf16_ceil(x):
    y = x.astype(BF16).astype(F32)
    return jnp.where(y < x, y * (1.0 + 2.0 ** -7), y)


def kernel(x_prompt, x_sample, cache_a_k, cache_a_v, cache_b_k, cache_b_v, cache_b_logf, page_table,
           c_prompt, c_sample, w_ada, b_ada, w_in, b_f, g_q_a, g_k_a, g_q_b, g_k_b,
           lambda_q1, lambda_k1, lambda_q2, lambda_k2, g_sub_a, w_out_a, w_out_b, w_o):
    assert w_ada.shape[0] == 1, "single-layer step"
    batch, seq, _ = x_prompt.shape
    n_seq, n_tok, _ = x_sample.shape
    n_pool, page = cache_a_k.shape[1], cache_a_k.shape[2]
    layer = 0
    lam_init = 0.8 - 0.6 * math.exp(-0.3 * layer)
    lam = (jnp.exp(jnp.sum(lambda_q1[layer] * lambda_k1[layer]))
           - jnp.exp(jnp.sum(lambda_q2[layer] * lambda_k2[layer])) + lam_init).reshape(1).astype(F32)

    w = w_in[layer]
    sec = np.cumsum((0, 512, 512, 512, 512, 512, 512, 512, H_B, 512, 1024, 1024))
    cols = lambda k: w[:, sec[k]:sec[k + 1]]
    wq = jnp.concatenate([cols(0), cols(4)], axis=1).astype(BF16)
    wn = jnp.concatenate([cols(1), cols(5), cols(2), cols(6), cols(3), cols(8), cols(9), cols(10)],
                         axis=1).astype(BF16)
    wf = jnp.pad(cols(7), ((0, 0), (0, LANES - H_B))).astype(BF16)
    gq = jnp.concatenate([jnp.tile(g_q_a[layer], N_GROUPS) * DH_A ** -0.5,
                          jnp.tile(g_q_b[layer], N_GROUPS) * DH_B ** -0.5])
    gk = jnp.concatenate([jnp.tile(g_k_a[layer], N_GROUPS), jnp.tile(g_k_b[layer], N_GROUPS)])
    wts = dict(_ones_constants(),
               wq=wq, wqkt=jnp.concatenate([wq, wn[:, 0:1024], wn[:, 1536:2048]], axis=1).T,
               wn=wn, wn_prompt=jnp.concatenate([wn[:, 1024:1536], wn[:, 2048:]], axis=1),

               wf=wf, bf=jnp.pad(b_f[layer], (0, LANES - H_B)).reshape(1, LANES),
               gqk_col=jnp.concatenate([gq, gk]).reshape(-1, 1), gq_row=gq.reshape(1, -1), gk_row=gk.reshape(1, -1),
               gsub=(jnp.tile(g_sub_a[layer], H_A) * (1.0 - lam_init)).reshape(1, W_A),
               woa=w_out_a[layer].astype(BF16), wob=w_out_b[layer].astype(BF16), wo=w_o[layer].astype(BF16))
    bound = lambda gq_, gk_: _bf16_ceil(8.1 * jnp.max(jnp.abs(gq_ * gk_)))
    mshift = jnp.stack([bound(g_q_a[layer], g_k_a[layer]), bound(g_q_b[layer], g_k_b[layer])]).astype(F32)

    n_c = batch + n_seq
    c_all = jnp.pad(jnp.concatenate([c_prompt, c_sample], axis=0), ((0, -n_c % 8), (0, 0)))
    mod = _ada(c_all, w_ada[layer], b_ada[layer])
    shift, scale, gres = (mod[:, k * D_MODEL:(k + 1) * D_MODEL] for k in range(3))

    xp2 = x_prompt.reshape(batch * seq, D_MODEL)
    p3 = lambda a: a[:batch].reshape(batch, 1, D_MODEL)
    (kta, va, ktb, vtbo, lft, qta, kaa, vta, qtb, kab, vtb, sga, sgb, sma, smb) = _proj_prompt(
        xp2, p3(scale), p3(shift), mshift, wts, batch, seq)
    online = jnp.max(mshift) > FAST_PATH_MAX_BOUND
    attend = lambda *a: lax.cond(online, lambda: _attention(*a, True), lambda: _attention(*a, False))
    oa = attend(qta, kaa, vta, lam, True).reshape(batch * seq, W_A)
    ob = attend(qtb, kab, vtb, lam, False).reshape(batch * seq, W_B)
    nb = seq // TM_MERGE
    gres_p_spec = pl.BlockSpec((None, 1, D_MODEL), lambda i: (i // nb, 0, 0))
    yp = _merge(xp2, oa, ob, sga, sgb, sma, smb, p3(gres), gres_p_spec, wts).reshape(batch, seq, D_MODEL)

    xs2 = x_sample.reshape(n_seq * n_tok, D_MODEL)
    rep = lambda a: jnp.repeat(a[batch:n_c], n_tok, axis=0)
    (qa_s, qb_s, ka_s, va_s, kb_s, vb_s, lf_s, sga_s, sgb_s, sma_s, smb_s) = _proj_sample(
        xs2, rep(scale), rep(shift), wts)
    pools = [jnp.transpose(cache_a_k[layer], (0, 2, 3, 4, 1)).reshape(n_pool, 512, page),
             cache_a_v[layer].reshape(n_pool, page * H_A, DV_A),
             jnp.transpose(cache_b_k[layer], (0, 2, 3, 1)).reshape(n_pool, 512, page),
             jnp.transpose(cache_b_v[layer], (0, 2, 3, 1)).reshape(n_pool, 512, page),
             jnp.swapaxes(cache_b_logf[layer], 1, 2)]
    lfn_t = jnp.pad(jnp.swapaxes(lf_s.reshape(n_seq // 2, 2 * n_tok, H_B), 1, 2),
                    ((0, 0), (0, 0), (0, page - 2 * n_tok)))
    oa_s, ob_s = _decode(page_table, lam, qa_s, qb_s, pools, [ka_s, va_s, kb_s, vb_s], lfn_t, wts["upper"])
    gres_s_spec = pl.BlockSpec((TM_MERGE, D_MODEL), lambda i: (i, 0))
    ys = _merge(xs2, oa_s, ob_s, sga_s, sgb_s, sma_s, smb_s,
                rep(gres), gres_s_spec, wts).reshape(n_seq, n_tok, D_MODEL)

    return (yp, ys,
            jnp.moveaxis(kta.reshape(1, batch, 2, H_A, DH_A, seq), -1, 2), va.reshape(1, batch, seq, H_A, DV_A),
            jnp.moveaxis(ktb.reshape(1, batch, H_B, DH_B, seq), -1, 2),
            jnp.moveaxis(vtbo.reshape(1, batch, H_B, DH_B, seq), -1, 2),
            jnp.moveaxis(lft.reshape(1, batch, H_B, seq), -1, 2),
            ka_s.reshape(1, n_seq, n_tok, 2, H_A, DH_A), va_s.reshape(1, n_seq, n_tok, H_A, DV_A),
            kb_s.reshape(1, n_seq, n_tok, H_B, DH_B), vb_s.reshape(1, n_seq, n_tok, H_B, DH_B),
            lf_s.reshape(1, n_seq, n_tok, H_B))
```
